```python
import math
import jax, jax.numpy as jnp
from jax import lax
import numpy as np

D_MODEL = 1024
BATCH = 4
SEQ = 4096
DEPTH = 4
DEC_BATCH = 128
DEC_SEQ = 4
PAST_LEN = 2048
PAGE_SIZE = 128

N_A_LAYERS = DEPTH // 2
N_B_LAYERS = DEPTH - N_A_LAYERS
SSM_GROUP = 16
SSM_GROUPS = D_MODEL // SSM_GROUP
SSM_STATE = 64
DT_MIN = 0.001
DT_MAX = 0.1
N_HEADS = 8
HEAD_DIM = D_MODEL // (2 * N_HEADS)
V_DIM = 2 * HEAD_DIM
Q_DIM = N_HEADS * 2 * HEAD_DIM
KV_DIM = Q_DIM + N_HEADS * V_DIM
D_FF = 4 * D_MODEL
NUM_BUCKETS = 32
MAX_DISTANCE = 128
Q_BLOCK = 128
N_MOD = 6
ALPHA = (2.0 * DEPTH) ** 0.25
BETA = (8.0 * DEPTH) ** -0.25
LN_EPS = 1e-5
NEG_INF = -1e30

kernel_name = "yoco_s5_diffattn_decoder_step"


def _layer_norm(x, g, b):
    xf = x.astype(jnp.float32)
    mu = jnp.mean(xf, -1, keepdims=True)
    var = jnp.mean(jnp.square(xf - mu), -1, keepdims=True)
    return ((xf - mu) * lax.rsqrt(var + LN_EPS) * g.astype(jnp.float32) + b.astype(jnp.float32)).astype(x.dtype)


def _rms_norm(x, g):
    xf = x.astype(jnp.float32)
    return (xf * lax.rsqrt(jnp.mean(jnp.square(xf), -1, keepdims=True) + LN_EPS) * g.astype(jnp.float32)).astype(x.dtype)


def _rel_bucket(rel):
    n = jnp.maximum(rel, 0)
    max_exact = NUM_BUCKETS // 2
    large = max_exact + (jnp.log(jnp.maximum(n, 1).astype(jnp.float32) / max_exact)
                         / math.log(MAX_DISTANCE / max_exact) * (NUM_BUCKETS - max_exact)).astype(jnp.int32)
    large = jnp.minimum(large, NUM_BUCKETS - 1)
    return jnp.where(n < max_exact, n, large)


def _rel_bias(rel_bias, q_pos, k_pos):
    b = rel_bias[_rel_bucket(q_pos[:, None] - k_pos[None, :])]
    return jnp.transpose(b, (2, 0, 1)).astype(jnp.float32)


def _ssm_combine(e1, e2):
    a1r, a1i, b1r, b1i = e1
    a2r, a2i, b2r, b2i = e2
    return (a2r * a1r - a2i * a1i, a2r * a1i + a2i * a1r,
            a2r * b1r - a2i * b1i + b2r, a2r * b1i + a2i * b1r + b2i)


def _ssm_mixer(u, h0_re, h0_im, lam_re, lam_im, log_dt, b_re, b_im, c_re, c_im, d_skip, w_glu_a, w_glu_b):
    f32 = jnp.float32
    nb, t = u.shape[0], u.shape[1]
    uf = u.astype(f32).reshape(nb, t, SSM_GROUPS, SSM_GROUP)
    dt = jnp.exp(log_dt.astype(f32))[:, None]
    lr = lam_re.astype(f32)
    li = lam_im.astype(f32)
    mag = jnp.exp(lr * dt)
    ar = mag * jnp.cos(li * dt)
    ai = mag * jnp.sin(li * dt)
    den = lr * lr + li * li
    er = ((ar - 1.0) * lr + ai * li) / den
    ei = (ai * lr - (ar - 1.0) * li) / den
    br = b_re.astype(f32)
    bi = b_im.astype(f32)
    bbr = er[..., None] * br - ei[..., None] * bi
    bbi = er[..., None] * bi + ei[..., None] * br
    bu_r = jnp.einsum('btgh,gph->btgp', uf, bbr)
    bu_i = jnp.einsum('btgh,gph->btgp', uf, bbi)
    h0r = h0_re.astype(f32)
    h0i = h0_im.astype(f32)
    bu_r = bu_r.at[:, 0].add(ar * h0r - ai * h0i)
    bu_i = bu_i.at[:, 0].add(ar * h0i + ai * h0r)
    a_r = jnp.broadcast_to(ar, (t,) + ar.shape)
    a_i = jnp.broadcast_to(ai, (t,) + ai.shape)

    def scan_one(xr, xi):
        _, _, sr, si = lax.associative_scan(_ssm_combine, (a_r, a_i, xr, xi))
        return sr, si

    sr, si = jax.vmap(scan_one)(bu_r, bu_i)
    y = (jnp.einsum('btgp,ghp->btgh', sr, c_re.astype(f32))
         - jnp.einsum('btgp,ghp->btgh', si, c_im.astype(f32))
         + d_skip.astype(f32).reshape(SSM_GROUPS, SSM_GROUP) * uf)
    y = y.reshape(nb, t, D_MODEL).astype(u.dtype)
    g = jax.nn.gelu(y)
    out = (g @ w_glu_a) * jax.nn.sigmoid(g @ w_glu_b)
    return out, sr[:, -1].astype(h0_re.dtype), si[:, -1].astype(h0_im.dtype)


def _diff_weights(logits, lam):
    p = jax.nn.softmax(logits, axis=-1)
    return p[:, :, 0] - lam * p[:, :, 1]


def _diff_attn_prompt(q, k, v, rel_bias, lam):
    nb, t = q.shape[0], q.shape[1]
    n_blk = t // Q_BLOCK
    qb = jnp.moveaxis(q.reshape(nb, n_blk, Q_BLOCK, N_HEADS, 2, HEAD_DIM), 1, 0)
    starts = jnp.arange(n_blk, dtype=jnp.int32) * Q_BLOCK
    k_pos = jnp.arange(t, dtype=jnp.int32)
    scale = HEAD_DIM ** -0.5

    def block(args):
        q_i, start = args
        q_pos = start + jnp.arange(Q_BLOCK, dtype=jnp.int32)
        logits = jnp.einsum('bqhcd,bkhcd->bhcqk', q_i, k).astype(jnp.float32) * scale
        logits = logits + _rel_bias(rel_bias, q_pos, k_pos)[None, :, None]
        logits = jnp.where(k_pos[None, :] <= q_pos[:, None], logits, NEG_INF)
        w = _diff_weights(logits, lam)
        return jnp.einsum('bhqk,bkhe->bqhe', w.astype(v.dtype), v)

    o = lax.map(block, (qb, starts))
    return jnp.moveaxis(o, 0, 1).reshape(nb, t, N_HEADS, V_DIM)


def _diff_attn_sample(q, k, v, k_past, v_past, rel_bias, lam):
    t = q.shape[1]
    past = k_past.shape[1]
    q_pos = past + jnp.arange(t, dtype=jnp.int32)
    k_pos = jnp.arange(past + t, dtype=jnp.int32)
    scale = HEAD_DIM ** -0.5
    lp = jnp.einsum('bqhcd,bkhcd->bhcqk', q, k_past)
    ln = jnp.einsum('bqhcd,bkhcd->bhcqk', q, k)
    logits = jnp.concatenate([lp, ln], axis=-1).astype(jnp.float32) * scale
    logits = logits + _rel_bias(rel_bias, q_pos, k_pos)[None, :, None]
    logits = jnp.where(k_pos[None, :] <= q_pos[:, None], logits, NEG_INF)
    w = _diff_weights(logits, lam).astype(v.dtype)
    return (jnp.einsum('bhqk,bkhe->bqhe', w[..., :past], v_past)
            + jnp.einsum('bhqk,bkhe->bqhe', w[..., past:], v))


def _trunk(x, c, h0_re, h0_im, attend, p):
    nb, t = x.shape[0], x.shape[1]
    sc = jax.nn.silu(c)
    ssm_re, ssm_im = [], []
    k = None
    v = None
    for layer in range(DEPTH):
        mod = (sc @ p['w_ada'][layer] + p['b_ada'][layer]).reshape(nb, 1, N_MOD, D_MODEL)
        shift_m, scale_m, gate_m = mod[:, :, 0], mod[:, :, 1], mod[:, :, 2]
        shift_f, scale_f, gate_f = mod[:, :, 3], mod[:, :, 4], mod[:, :, 5]
        h = x * (1.0 + scale_m) + shift_m
        if layer < N_A_LAYERS:
            out, fr, fi = _ssm_mixer(h, h0_re[layer], h0_im[layer],
                                     p['ssm_lam_re'][layer], p['ssm_lam_im'][layer], p['ssm_log_dt'][layer],
                                     p['ssm_b_re'][layer], p['ssm_b_im'][layer],
                                     p['ssm_c_re'][layer], p['ssm_c_im'][layer], p['ssm_d'][layer],
                                     p['ssm_w_glu_a'][layer], p['ssm_w_glu_b'][layer])
            ssm_re.append(fr)
            ssm_im.append(fi)
        else:
            j = layer - N_A_LAYERS
            lam_init = 0.8 - 0.6 * math.exp(-0.3 * layer)
            lp = p['attn_lam'][j].astype(jnp.float32)
            lam = jnp.exp(jnp.sum(lp[0] * lp[1])) - jnp.exp(jnp.sum(lp[2] * lp[3])) + lam_init
            q = (h @ p['attn_w_q'][j]).reshape(nb, t, N_HEADS, 2, HEAD_DIM)
            o = attend(q, k, v, lam)
            o = _rms_norm(o, p['attn_subln_g'][j]) * (1.0 - lam_init)
            out = o.reshape(nb, t, D_MODEL) @ p['attn_w_o'][j]
        x = _layer_norm(ALPHA * x + (1.0 + gate_m) * out, p['ln_g'][layer, 0], p['ln_b'][layer, 0])
        h = x * (1.0 + scale_f) + shift_f
        hid = jnp.square(jax.nn.relu(h @ p['mlp_w1'][layer]))
        x = _layer_norm(ALPHA * x + (1.0 + gate_f) * (hid @ p['mlp_w2'][layer]),
                        p['ln_g'][layer, 1], p['ln_b'][layer, 1])
        if layer == N_A_LAYERS - 1:
            mkv = (sc @ p['w_ada_kv'] + p['b_ada_kv']).reshape(nb, 1, 2, D_MODEL)
            hkv = x * (1.0 + mkv[:, :, 1]) + mkv[:, :, 0]
            kv = hkv @ p['w_kv']
            k = kv[..., :Q_DIM].reshape(nb, t, N_HEADS, 2, HEAD_DIM)
            v = kv[..., Q_DIM:].reshape(nb, t, N_HEADS, V_DIM)
    return x, jnp.stack(ssm_re), jnp.stack(ssm_im), k, v


def setup_inputs(seed: int = 0) -> dict:
    key = jax.random.key(seed)
    ks = jax.random.split(key, 40)
    f32 = jnp.float32
    n_pages = PAST_LEN // PAGE_SIZE
    n_phys = (5 * DEC_BATCH * n_pages + 3) // 4
    nrm = lambda k, shape, s: jax.random.normal(k, shape, f32) * s
    page_table = jax.random.permutation(ks[0], n_phys)[:DEC_BATCH * n_pages].reshape(DEC_BATCH, n_pages).astype(jnp.int32)
    lam_im_base = (math.pi * jnp.arange(SSM_STATE, dtype=f32))[None, None, :]
    return {
        'x_prompt': nrm(ks[1], (BATCH, SEQ, D_MODEL), 1.0),
        'x_sample': nrm(ks[2], (DEC_BATCH, DEC_SEQ, D_MODEL), 1.0),
        'state_ssm_re': nrm(ks[3], (N_A_LAYERS, DEC_BATCH, SSM_GROUPS, SSM_STATE), 0.1),
        'state_ssm_im': nrm(ks[4], (N_A_LAYERS, DEC_BATCH, SSM_GROUPS, SSM_STATE), 0.1),
        'cache_k': nrm(ks[5], (n_phys, PAGE_SIZE, N_HEADS, 2, HEAD_DIM), 1.0),
        'cache_v': nrm(ks[6], (n_phys, PAGE_SIZE, N_HEADS, V_DIM), 1.0),
        'page_table': page_table,
        'c_prompt': nrm(ks[7], (BATCH, D_MODEL), 1.0),
        'c_sample': nrm(ks[8], (DEC_BATCH, D_MODEL), 1.0),
        'rel_bias': nrm(ks[9], (NUM_BUCKETS, N_HEADS), 0.5),
        'w_ada': nrm(ks[10], (DEPTH, D_MODEL, N_MOD * D_MODEL), 0.1 * D_MODEL ** -0.5),
        'b_ada': nrm(ks[11], (DEPTH, N_MOD * D_MODEL), 0.02),
        'ln_g': 1.0 + nrm(ks[12], (DEPTH, 2, D_MODEL), 0.02),
        'ln_b': nrm(ks[13], (DEPTH, 2, D_MODEL), 0.02),
        'ssm_lam_re': -0.5 + nrm(ks[14], (N_A_LAYERS, SSM_GROUPS, SSM_STATE), 0.01),
        'ssm_lam_im': lam_im_base + nrm(ks[15], (N_A_LAYERS, SSM_GROUPS, SSM_STATE), 0.01),
        'ssm_log_dt': jax.random.uniform(ks[16], (N_A_LAYERS, SSM_GROUPS), f32, math.log(DT_MIN), math.log(DT_MAX)),
        'ssm_b_re': nrm(ks[17], (N_A_LAYERS, SSM_GROUPS, SSM_STATE, SSM_GROUP), (2.0 * SSM_GROUP) ** -0.5),
        'ssm_b_im': nrm(ks[18], (N_A_LAYERS, SSM_GROUPS, SSM_STATE, SSM_GROUP), (2.0 * SSM_GROUP) ** -0.5),
        'ssm_c_re': nrm(ks[19], (N_A_LAYERS, SSM_GROUPS, SSM_GROUP, SSM_STATE), (2.0 / SSM_STATE) ** 0.5),
        'ssm_c_im': nrm(ks[20], (N_A_LAYERS, SSM_GROUPS, SSM_GROUP, SSM_STATE), (2.0 / SSM_STATE) ** 0.5),
        'ssm_d': nrm(ks[21], (N_A_LAYERS, D_MODEL), 1.0),
        'ssm_w_glu_a': nrm(ks[22], (N_A_LAYERS, D_MODEL, D_MODEL), BETA * D_MODEL ** -0.5),
        'ssm_w_glu_b': nrm(ks[23], (N_A_LAYERS, D_MODEL, D_MODEL), D_MODEL ** -0.5),
        'w_ada_kv': nrm(ks[24], (D_MODEL, 2 * D_MODEL), 0.1 * D_MODEL ** -0.5),
        'b_ada_kv': nrm(ks[25], (2 * D_MODEL,), 0.02),
        'w_kv': nrm(ks[26], (D_MODEL, KV_DIM), D_MODEL ** -0.5),
        'attn_w_q': nrm(ks[27], (N_B_LAYERS, D_MODEL, Q_DIM), D_MODEL ** -0.5),
        'attn_lam': nrm(ks[28], (N_B_LAYERS, 4, HEAD_DIM), 0.1),
        'attn_subln_g': 1.0 + nrm(ks[29], (N_B_LAYERS, V_DIM), 0.02),
        'attn_w_o': nrm(ks[30], (N_B_LAYERS, D_MODEL, D_MODEL), BETA * D_MODEL ** -0.5),
        'mlp_w1': nrm(ks[31], (DEPTH, D_MODEL, D_FF), D_MODEL ** -0.5),
        'mlp_w2': nrm(ks[32], (DEPTH, D_FF, D_MODEL), BETA * D_FF ** -0.5),
    }


def reference(x_prompt, x_sample, state_ssm_re, state_ssm_im, cache_k, cache_v, page_table,
              c_prompt, c_sample, rel_bias, w_ada, b_ada, ln_g, ln_b,
              ssm_lam_re, ssm_lam_im, ssm_log_dt, ssm_b_re, ssm_b_im, ssm_c_re, ssm_c_im, ssm_d,
              ssm_w_glu_a, ssm_w_glu_b, w_ada_kv, b_ada_kv, w_kv,
              attn_w_q, attn_lam, attn_subln_g, attn_w_o, mlp_w1, mlp_w2):
    params = dict(w_ada=w_ada, b_ada=b_ada, ln_g=ln_g, ln_b=ln_b,
                  ssm_lam_re=ssm_lam_re, ssm_lam_im=ssm_lam_im, ssm_log_dt=ssm_log_dt,
                  ssm_b_re=ssm_b_re, ssm_b_im=ssm_b_im, ssm_c_re=ssm_c_re, ssm_c_im=ssm_c_im, ssm_d=ssm_d,
                  ssm_w_glu_a=ssm_w_glu_a, ssm_w_glu_b=ssm_w_glu_b,
                  w_ada_kv=w_ada_kv, b_ada_kv=b_ada_kv, w_kv=w_kv,
                  attn_w_q=attn_w_q, attn_lam=attn_lam, attn_subln_g=attn_subln_g, attn_w_o=attn_w_o,
                  mlp_w1=mlp_w1, mlp_w2=mlp_w2)
    h0_p = jnp.zeros((N_A_LAYERS, x_prompt.shape[0], SSM_GROUPS, SSM_STATE), state_ssm_re.dtype)

    def attend_prompt(q, k, v, lam):
        return _diff_attn_prompt(q, k, v, rel_bias, lam)

    y_prompt, ssm_re_p, ssm_im_p, k_p, v_p = _trunk(x_prompt, c_prompt, h0_p, h0_p, attend_prompt, params)

    n_seq = page_table.shape[0]
    k_past = cache_k[page_table].reshape(n_seq, -1, N_HEADS, 2, HEAD_DIM)
    v_past = cache_v[page_table].reshape(n_seq, -1, N_HEADS, V_DIM)

    def attend_sample(q, k, v, lam):
        return _diff_attn_sample(q, k, v, k_past, v_past, rel_bias, lam)

    y_sample, ssm_re_s, ssm_im_s, k_s, v_s = _trunk(x_sample, c_sample, state_ssm_re, state_ssm_im,
                                                    attend_sample, params)
    return (y_prompt, y_sample, ssm_re_p, ssm_im_p, k_p, v_p, ssm_re_s, ssm_im_s, k_s, v_s)
```

```python
import functools
import math

import jax
import jax.numpy as jnp
from jax import lax
from jax.experimental import pallas as pl
from jax.experimental.pallas import tpu as pltpu

F32 = jnp.float32
BF16 = jnp.bfloat16

D_MODEL = 1024
BATCH = 4
SEQ = 4096
DEPTH = 4
DEC_BATCH = 128
DEC_SEQ = 4
PAST_LEN = 2048
PAGE_SIZE = 128
N_PAGES = PAST_LEN // PAGE_SIZE
N_A_LAYERS = DEPTH // 2
SSM_GROUP = 16
SSM_GROUPS = D_MODEL // SSM_GROUP
SSM_STATE = 64
N_STATE = SSM_GROUPS * SSM_STATE
N_HEADS = 8
HEAD_DIM = D_MODEL // (2 * N_HEADS)
V_DIM = 2 * HEAD_DIM
D_FF = 4 * D_MODEL
NUM_BUCKETS = 32
MAX_DISTANCE = 128
N_MOD = 6
ALPHA = (2.0 * DEPTH) ** 0.25
LN_EPS = 1e-5
NEG_INF = -1e30

SUBLANES = 8
LANES = 128
VMEM_LIMIT = 48 * 1024 * 1024

MOD_ROWS = DEC_BATCH + 2 * BATCH
CHUNK = LANES
N_CHUNK = D_MODEL // CHUNK
CHUNK_STATE = N_STATE // N_CHUNK
TM_PROMPT = 512
T_SSM = 512
TQ = 256
PAGES_PER_STEP = 8
FF_CHUNK = 1024


def _params(sem, vmem=VMEM_LIMIT):
    return pltpu.CompilerParams(dimension_semantics=sem, vmem_limit_bytes=vmem)


def _layer_norm(z, g, b):
    mu = jnp.mean(z, axis=-1, keepdims=True)
    zc = z - mu
    var = jnp.mean(zc * zc, axis=-1, keepdims=True)
    return zc * lax.rsqrt(var + LN_EPS) * g + b


def _ada_kernel(c_ref, w_ref, b_ref, o_ref):
    c = c_ref[...]
    sc = (c * jax.nn.sigmoid(c)).astype(BF16)
    w = w_ref[...].astype(BF16)
    o_ref[...] = jnp.dot(sc, w, preferred_element_type=F32) + b_ref[...]


def _ada(c_all, w, b):
    n_l, _, width = w.shape
    tn = 1024
    return pl.pallas_call(
        _ada_kernel,
        grid=(n_l, width // tn),
        in_specs=[
            pl.BlockSpec((MOD_ROWS, D_MODEL), lambda l, j: (0, 0)),
            pl.BlockSpec((None, D_MODEL, tn), lambda l, j: (l, 0, j)),
            pl.BlockSpec((None, 1, tn), lambda l, j: (l, 0, j)),
        ],
        out_specs=pl.BlockSpec((None, MOD_ROWS, tn), lambda l, j: (l, 0, j)),
        out_shape=jax.ShapeDtypeStruct((n_l, MOD_ROWS, width), F32),
        compiler_params=_params(("arbitrary", "arbitrary")),
        name="ada_mod",
    )(c_all, w, b.reshape(n_l, 1, width))


class _Mods:
    def __init__(self, arr, mode):
        self.mode = mode
        self.arr = arr if mode == "s" else arr.reshape(arr.shape[0], MOD_ROWS, 1, arr.shape[-1])

    def spec(self, layer, j, batch_of):
        if self.mode == "s":
            return pl.BlockSpec((None, DEC_BATCH, D_MODEL), lambda *g: (layer, 0, j))
        return pl.BlockSpec((None, None, 1, D_MODEL),
                            lambda *g: (layer, DEC_BATCH + batch_of(*g), 0, j))


def _row_tiling(mode):
    if mode == "s":
        return DEC_BATCH, lambda i: 0
    tiles_per_batch = SEQ // TM_PROMPT
    return TM_PROMPT, lambda i: i // tiles_per_batch


def _const_spec(shape):
    zeros = (0,) * len(shape)
    return pl.BlockSpec(shape, lambda *g: zeros, pipeline_mode=pl.Buffered(1))


def _mod_linear_kernel(x_ref, sc_ref, sh_ref, w_ref, *o_refs, scale):
    h = (x_ref[...] * (1.0 + sc_ref[...]) + sh_ref[...]).astype(BF16)
    y = jnp.dot(h, w_ref[...], preferred_element_type=F32)
    if scale != 1.0:
        y = y * scale
    for n, o_ref in enumerate(o_refs):
        o_ref[...] = y[:, n * D_MODEL:(n + 1) * D_MODEL].astype(o_ref.dtype)


def _mod_linear(x, mods, layer, j_scale, j_shift, w, out_dtypes, scale=1.0):
    n = x.shape[0]
    tm, batch_of = _row_tiling(mods.mode)
    row = pl.BlockSpec((tm, D_MODEL), lambda i: (i, 0))
    return pl.pallas_call(
        functools.partial(_mod_linear_kernel, scale=scale),
        grid=(n // tm,),
        in_specs=[row, mods.spec(layer, j_scale, batch_of), mods.spec(layer, j_shift, batch_of),
                  _const_spec(w.shape)],
        out_specs=[row] * len(out_dtypes),
        out_shape=[jax.ShapeDtypeStruct((n, D_MODEL), dt) for dt in out_dtypes],
        compiler_params=_params(("arbitrary",)),
        name="mod_linear",
    )(x, mods.arr, mods.arr, w)


def _linear_ln_kernel(a_ref, x_ref, gt_ref, w_ref, g_ref, b_ref, o_ref, *, glu):
    y = jnp.dot(a_ref[...], w_ref[...], preferred_element_type=F32)
    if glu:
        y = y[:, :D_MODEL] * jax.nn.sigmoid(y[:, D_MODEL:])
    z = ALPHA * x_ref[...] + (1.0 + gt_ref[...]) * y
    o_ref[...] = _layer_norm(z, g_ref[...], b_ref[...])


def _linear_ln(a, x, mods, layer, j_gate, w, ln_g, ln_b, glu):
    n = x.shape[0]
    tm, batch_of = _row_tiling(mods.mode)
    row = pl.BlockSpec((tm, D_MODEL), lambda i: (i, 0))
    return pl.pallas_call(
        functools.partial(_linear_ln_kernel, glu=glu),
        grid=(n // tm,),
        in_specs=[row, row, mods.spec(layer, j_gate, batch_of), _const_spec(w.shape),
                  _const_spec((1, D_MODEL)), _const_spec((1, D_MODEL))],
        out_specs=row,
        out_shape=jax.ShapeDtypeStruct((n, D_MODEL), F32),
        compiler_params=_params(("arbitrary",)),
        name="linear_ln",
    )(a, x, mods.arr, w, ln_g.reshape(1, D_MODEL), ln_b.reshape(1, D_MODEL))


def _mlp_kernel(x_ref, sc_ref, sh_ref, gt_ref, w1_ref, w2_ref, g_ref, b_ref, o_ref):
    x = x_ref[...]
    h = (x * (1.0 + sc_ref[...]) + sh_ref[...]).astype(BF16)
    acc = jnp.zeros(x.shape, F32)
    for c in range(D_FF // FF_CHUNK):
        cols = slice(c * FF_CHUNK, (c + 1) * FF_CHUNK)
        hid = jnp.dot(h, w1_ref[:, cols], preferred_element_type=F32)
        hid = jnp.square(jnp.maximum(hid, 0.0)).astype(BF16)
        acc = acc + jnp.dot(hid, w2_ref[cols, :], preferred_element_type=F32)
    z = ALPHA * x + (1.0 + gt_ref[...]) * acc
    o_ref[...] = _layer_norm(z, g_ref[...], b_ref[...])


def _mlp(x, mods, layer, w1, w2, ln_g, ln_b):
    n = x.shape[0]
    tm, batch_of = _row_tiling(mods.mode)
    row = pl.BlockSpec((tm, D_MODEL), lambda i: (i, 0))
    return pl.pallas_call(
        _mlp_kernel,
        grid=(n // tm,),
        in_specs=[row, mods.spec(layer, 4, batch_of), mods.spec(layer, 3, batch_of),
                  mods.spec(layer, 5, batch_of), _const_spec(w1.shape), _const_spec(w2.shape),
                  _const_spec((1, D_MODEL)), _const_spec((1, D_MODEL))],
        out_specs=row,
        out_shape=jax.ShapeDtypeStruct((n, D_MODEL), F32),
        compiler_params=_params(("arbitrary",)),
        name="mlp_ln",
    )(x, mods.arr, mods.arr, mods.arr, w1, w2, ln_g.reshape(1, D_MODEL), ln_b.reshape(1, D_MODEL))


def _cmul(ar, ai, br, bi):
    return ar * br - ai * bi, ar * bi + ai * br


def _ssm_prep(lam_re, lam_im, log_dt, b_re, b_im, c_re, c_im):
    dt = jnp.exp(log_dt)[:, None]
    mag = jnp.exp(lam_re * dt)
    ar = mag * jnp.cos(lam_im * dt)
    ai = mag * jnp.sin(lam_im * dt)
    den = lam_re * lam_re + lam_im * lam_im
    er = ((ar - 1.0) * lam_re + ai * lam_im) / den
    ei = (ai * lam_re - (ar - 1.0) * lam_im) / den
    bbr = er[..., None] * b_re - ei[..., None] * b_im
    bbi = er[..., None] * b_im + ei[..., None] * b_re
    gpc = CHUNK // SSM_GROUP
    eye = jnp.eye(gpc, dtype=F32)

    def b_blocks(m):
        m = m.reshape(N_CHUNK, gpc, SSM_STATE, SSM_GROUP)
        return jnp.einsum("kgpi,gf->kgifp", m, eye).reshape(N_CHUNK, CHUNK, CHUNK_STATE)

    def c_blocks(m):
        m = m.reshape(N_CHUNK, gpc, SSM_GROUP, SSM_STATE)
        return jnp.einsum("kghp,gf->kgpfh", m, eye).reshape(N_CHUNK, CHUNK_STATE, CHUNK)

    bcat = jnp.concatenate([b_blocks(bbr), b_blocks(bbi)], axis=-1).astype(BF16)
    ccat = jnp.concatenate([c_blocks(c_re), -c_blocks(c_im)], axis=1).astype(BF16)

    a1 = (ar.reshape(1, N_STATE), ai.reshape(1, N_STATE))
    a2 = _cmul(*a1, *a1)
    a3 = _cmul(*a2, *a1)
    a4 = _cmul(*a2, *a2)
    a5 = _cmul(*a4, *a1)
    a6 = _cmul(*a4, *a2)
    a7 = _cmul(*a4, *a3)
    a8 = _cmul(*a4, *a4)
    rows = lax.broadcasted_iota(jnp.int32, (SUBLANES, N_STATE), 0)

    def masked(a, first):
        return [jnp.where(rows >= first, jnp.broadcast_to(p, rows.shape), 0.0) for p in a]

    powers = [jnp.concatenate(p, axis=0) for p in zip(a1, a2, a3, a4, a5, a6, a7, a8)]
    tab = jnp.stack(masked(a1, 1) + masked(a2, 2) + masked(a4, 4) + powers + masked(a8, 0))
    atab = jnp.stack(masked(a1, 0))
    return bcat, ccat, tab, atab


def _ssm_prompt_kernel(x_ref, sc_ref, sh_ref, bcat_ref, ccat_ref, d_ref, tab_ref,
                       g_ref, hre_ref, him_ref, buf_ref, car_ref):
    @pl.when(pl.program_id(1) == 0)
    def _():
        car_ref[...] = jnp.zeros_like(car_ref)

    h = x_ref[...] * (1.0 + sc_ref[...]) + sh_ref[...]
    hb = h.astype(BF16)
    n_tiles = x_ref.shape[0] // SUBLANES
    half = CHUNK_STATE
    for k in range(N_CHUNK):
        ch = slice(k * CHUNK, (k + 1) * CHUNK)
        st = slice(k * half, (k + 1) * half)
        buf_ref[...] = jnp.dot(hb[:, ch], bcat_ref[k], preferred_element_type=F32)

        def tile_scan(i, carry, st=st):
            cr, ci = carry
            r0 = pl.multiple_of(i * SUBLANES, SUBLANES)
            vr = buf_ref[pl.ds(r0, SUBLANES), 0:half]
            vi = buf_ref[pl.ds(r0, SUBLANES), half:2 * half]
            for lvl, shift in enumerate((1, 2, 4)):
                pr, pi = _cmul(tab_ref[2 * lvl, :, st], tab_ref[2 * lvl + 1, :, st],
                               pltpu.roll(vr, shift, 0), pltpu.roll(vi, shift, 0))
                vr, vi = vr + pr, vi + pi
            pr, pi = _cmul(tab_ref[6, :, st], tab_ref[7, :, st], cr, ci)
            buf_ref[pl.ds(r0, SUBLANES), 0:half] = vr + pr
            buf_ref[pl.ds(r0, SUBLANES), half:2 * half] = vi + pi
            pr, pi = _cmul(tab_ref[8, :, st], tab_ref[9, :, st], cr, ci)
            last = SUBLANES - 1
            return (jnp.broadcast_to(vr[last:, :], vr.shape) + pr,
                    jnp.broadcast_to(vi[last:, :], vi.shape) + pi)

        cr, ci = lax.fori_loop(0, n_tiles, tile_scan, (car_ref[0, :, st], car_ref[1, :, st]))
        car_ref[0, :, st] = cr
        car_ref[1, :, st] = ci
        y = jnp.dot(buf_ref[...].astype(BF16), ccat_ref[k], preferred_element_type=F32)
        y = y + d_ref[:, ch] * h[:, ch]
        g_ref[:, ch] = jax.nn.gelu(y).astype(g_ref.dtype)
    hre_ref[...] = car_ref[0, 0:1, :]
    him_ref[...] = car_ref[1, 0:1, :]


def _ssm_prompt(x, mods, layer, bcat, ccat, d_skip, tab):
    n_t = SEQ // T_SSM
    row = pl.BlockSpec((T_SSM, D_MODEL), lambda b, c: (b * n_t + c, 0))
    fin = pl.BlockSpec((None, 1, N_STATE), lambda b, c: (b, 0, 0))
    batch_of = lambda b, c: b
    g, hre, him = pl.pallas_call(
        _ssm_prompt_kernel,
        grid=(BATCH, n_t),
        in_specs=[row, mods.spec(layer, 1, batch_of), mods.spec(layer, 0, batch_of),
                  _const_spec(bcat.shape), _const_spec(ccat.shape), _const_spec((1, D_MODEL)),
                  _const_spec(tab.shape)],
        out_specs=[row, fin, fin],
        out_shape=[jax.ShapeDtypeStruct((BATCH * SEQ, D_MODEL), BF16),
                   jax.ShapeDtypeStruct((BATCH, 1, N_STATE), F32),
                   jax.ShapeDtypeStruct((BATCH, 1, N_STATE), F32)],
        scratch_shapes=[pltpu.VMEM((T_SSM, 2 * CHUNK_STATE), F32),
                        pltpu.VMEM((2, SUBLANES, N_STATE), F32)],
        compiler_params=_params(("arbitrary", "arbitrary")),
        name="ssm_prompt",
    )(x, mods.arr, mods.arr, bcat, ccat, d_skip.reshape(1, D_MODEL), tab)
    shape = (BATCH, SSM_GROUPS, SSM_STATE)
    return g, hre.reshape(shape), him.reshape(shape)


def _ssm_sample_kernel(x_ref, sc_ref, sh_ref, bcat_ref, ccat_ref, d_ref, a_ref, h0r_ref, h0i_ref,
                       g_ref, hre_ref, him_ref):
    tile = lambda m: jnp.concatenate([m] * DEC_SEQ, axis=0)
    h = x_ref[...] * (1.0 + tile(sc_ref[...])) + tile(sh_ref[...])
    bu = jnp.dot(h.astype(BF16), bcat_ref[...], preferred_element_type=F32)
    ar = a_ref[0, 0:1, :]
    ai = a_ref[1, 0:1, :]
    sr = h0r_ref[...]
    si = h0i_ref[...]
    states = []
    for t in range(DEC_SEQ):
        rows = slice(t * DEC_BATCH, (t + 1) * DEC_BATCH)
        pr, pi = _cmul(ar, ai, sr, si)
        sr = pr + bu[rows, :CHUNK_STATE]
        si = pi + bu[rows, CHUNK_STATE:]
        states.append(jnp.concatenate([sr, si], axis=1))
    xs = jnp.concatenate(states, axis=0).astype(BF16)
    y = jnp.dot(xs, ccat_ref[...], preferred_element_type=F32) + d_ref[...] * h
    g_ref[...] = jax.nn.gelu(y).astype(g_ref.dtype)
    hre_ref[...] = sr
    him_ref[...] = si


def _ssm_sample(x, mods, layer, bcat, ccat, d_skip, atab, h0_re, h0_im):
    n = DEC_BATCH * DEC_SEQ
    col = pl.BlockSpec((n, CHUNK), lambda k: (0, k))
    st = pl.BlockSpec((DEC_BATCH, CHUNK_STATE), lambda k: (0, k))
    mod = lambda j: pl.BlockSpec((None, DEC_BATCH, CHUNK), lambda k: (layer, 0, j * N_CHUNK + k))
    g, hre, him = pl.pallas_call(
        _ssm_sample_kernel,
        grid=(N_CHUNK,),
        in_specs=[col, mod(1), mod(0),
                  pl.BlockSpec((None, CHUNK, 2 * CHUNK_STATE), lambda k: (k, 0, 0)),
                  pl.BlockSpec((None, 2 * CHUNK_STATE, CHUNK), lambda k: (k, 0, 0)),
                  pl.BlockSpec((1, CHUNK), lambda k: (0, k)),
                  pl.BlockSpec((2, SUBLANES, CHUNK_STATE), lambda k: (0, 0, k)),
                  st, st],
        out_specs=[col, st, st],
        out_shape=[jax.ShapeDtypeStruct((n, D_MODEL), BF16),
                   jax.ShapeDtypeStruct((DEC_BATCH, N_STATE), F32),
                   jax.ShapeDtypeStruct((DEC_BATCH, N_STATE), F32)],
        compiler_params=_params(("arbitrary",)),
        name="ssm_sample",
    )(x, mods.arr, mods.arr, bcat, ccat, d_skip.reshape(1, D_MODEL), atab,
      h0_re.reshape(DEC_BATCH, N_STATE), h0_im.reshape(DEC_BATCH, N_STATE))
    shape = (DEC_BATCH, SSM_GROUPS, SSM_STATE)
    return g, hre.reshape(shape), him.reshape(shape)


def _rel_bucket(rel):
    n = jnp.maximum(rel, 0)
    max_exact = NUM_BUCKETS // 2
    large = max_exact + (jnp.log(jnp.maximum(n, 1).astype(F32) / max_exact)
                         / math.log(MAX_DISTANCE / max_exact) * (NUM_BUCKETS - max_exact)).astype(jnp.int32)
    large = jnp.minimum(large, NUM_BUCKETS - 1)
    return jnp.where(n < max_exact, n, large)


def _attn_scalars(attn_lam, subln_g, layer):
    lam_init = 0.8 - 0.6 * math.exp(-0.3 * layer)
    lam = (jnp.exp(jnp.sum(attn_lam[0] * attn_lam[1])) - jnp.exp(jnp.sum(attn_lam[2] * attn_lam[3]))
           + lam_init)
    lam_tile = jnp.full((SUBLANES, LANES), lam, F32)
    gain = jnp.tile(subln_g * (1.0 - lam_init), N_HEADS).reshape(1, D_MODEL)
    return lam_tile, gain


def _softmax_step(s, v, m_ref, l_ref, acc_ref):
    m_prev = m_ref[...]
    m_new = jnp.maximum(m_prev, jnp.max(s, axis=-1, keepdims=True))
    p = jnp.exp(s - m_new)
    alpha = jnp.exp(m_prev - m_new)
    l_ref[...] = alpha * l_ref[...] + jnp.sum(p, axis=-1, keepdims=True)
    acc_ref[...] = alpha * acc_ref[...] + jnp.dot(p.astype(BF16), v, preferred_element_type=F32)
    m_ref[...] = m_new


_NT = (((1,), (1,)), ((), ()))


def _attn_prompt_kernel(q_ref, k_ref, v_ref, tab_ref, lam_ref, gain_ref, o_ref,
                        q2_ref, m_ref, l_ref, acc_ref):
    i = pl.program_id(2)
    q = q_ref[...]
    lane = lax.broadcasted_iota(jnp.int32, q.shape, 1)
    zero = jnp.zeros_like(q)
    q2_ref[0:TQ, :] = jnp.where(lane < HEAD_DIM, q, zero)
    q2_ref[TQ:2 * TQ, :] = jnp.where(lane >= HEAD_DIM, q, zero)
    m_ref[...] = jnp.full(m_ref.shape, NEG_INF, F32)
    l_ref[...] = jnp.zeros_like(l_ref)
    acc_ref[...] = jnp.zeros_like(acc_ref)

    def step(j, bias):
        rows = pl.ds(pl.multiple_of(j * TQ, TQ), TQ)
        s = lax.dot_general(q2_ref[...], k_ref[rows, :].astype(BF16), _NT,
                            preferred_element_type=F32)
        if bias is not None:
            s = s + bias
        _softmax_step(s, v_ref[rows, :].astype(BF16), m_ref, l_ref, acc_ref)

    def far(j, c):
        step(j, None)
        return c

    lax.fori_loop(0, jnp.maximum(i - 1, 0), far, 0)

    @pl.when(i >= 1)
    def _():
        step(i - 1, tab_ref[1])

    step(i, tab_ref[0])

    o = acc_ref[...] / l_ref[...]
    o = o[0:TQ, :] - lam_ref[0:1, 0:1] * o[TQ:2 * TQ, :]
    o = o * lax.rsqrt(jnp.mean(o * o, axis=-1, keepdims=True) + LN_EPS) * gain_ref[...]
    o_ref[...] = o.astype(o_ref.dtype)


def _attn_prompt_tables(rel_bias):
    qi = jnp.arange(TQ, dtype=jnp.int32)[:, None]
    kj = jnp.arange(TQ, dtype=jnp.int32)[None, :]
    far = rel_bias[NUM_BUCKETS - 1]
    diag = jnp.where((qi >= kj)[..., None], rel_bias[_rel_bucket(qi - kj)] - far, NEG_INF)
    prev = rel_bias[_rel_bucket(TQ + qi - kj)] - far
    tab = jnp.stack([diag, prev]).astype(F32)
    tab = jnp.transpose(tab, (3, 0, 1, 2))
    return jnp.concatenate([tab, tab], axis=2)


def _attn_prompt(q, k, v, tab, lam_tile, gain):
    n_q = SEQ // TQ
    shape3 = (BATCH, SEQ, D_MODEL)
    qspec = pl.BlockSpec((None, TQ, V_DIM), lambda b, h, i: (b, i, h))
    kvspec = pl.BlockSpec((None, SEQ, V_DIM), lambda b, h, i: (b, 0, h))
    o = pl.pallas_call(
        _attn_prompt_kernel,
        grid=(BATCH, N_HEADS, n_q),
        in_specs=[qspec, kvspec, kvspec,
                  pl.BlockSpec((None, 2, 2 * TQ, TQ), lambda b, h, i: (h, 0, 0, 0)),
                  pl.BlockSpec((SUBLANES, LANES), lambda b, h, i: (0, 0)),
                  pl.BlockSpec((1, V_DIM), lambda b, h, i: (0, 0))],
        out_specs=qspec,
        out_shape=jax.ShapeDtypeStruct(shape3, BF16),
        scratch_shapes=[pltpu.VMEM((2 * TQ, V_DIM), BF16), pltpu.VMEM((2 * TQ, 1), F32),
                        pltpu.VMEM((2 * TQ, 1), F32), pltpu.VMEM((2 * TQ, V_DIM), F32)],
        compiler_params=_params(("arbitrary", "arbitrary", "arbitrary")),
        name="attn_prompt",
    )(q.reshape(shape3), k.reshape(shape3), v.reshape(shape3), tab, lam_tile, gain[:, :V_DIM])
    return o.reshape(BATCH * SEQ, D_MODEL)


N_SROWS = 2 * DEC_SEQ * N_HEADS


def _attn_sample_kernel(pt_ref, q_ref, qmask_ref, knew_ref, vnew_ref, bias_ref, bnew_ref,
                        dmask_ref, lam_ref, gain_ref, *rest):
    del pt_ref
    k_refs = rest[:PAGES_PER_STEP]
    v_refs = rest[PAGES_PER_STEP:2 * PAGES_PER_STEP]
    o_ref, qx_ref, m_ref, l_ref, acc_ref = rest[2 * PAGES_PER_STEP:]
    step = pl.program_id(1)

    @pl.when(step == 0)
    def _():
        q = q_ref[...].astype(F32)
        qx_ref[...] = jnp.concatenate(
            [jnp.broadcast_to(q[t:t + 1, :], (N_HEADS, D_MODEL)) * qmask_ref[c]
             for c in range(2) for t in range(DEC_SEQ)], axis=0).astype(BF16)
        m_ref[...] = jnp.full(m_ref.shape, NEG_INF, F32)
        l_ref[...] = jnp.zeros_like(l_ref)
        acc_ref[...] = jnp.zeros_like(acc_ref)

    qx = qx_ref[...]
    kb = jnp.concatenate([r[...].astype(BF16) for r in k_refs], axis=0)
    vb = jnp.concatenate([r[...].astype(BF16) for r in v_refs], axis=0)
    s = lax.dot_general(qx, kb, _NT, preferred_element_type=F32) + bias_ref[...]
    _softmax_step(s, vb, m_ref, l_ref, acc_ref)

    @pl.when(step == pl.num_programs(1) - 1)
    def _():
        pad = jnp.zeros((LANES - SUBLANES, D_MODEL), BF16)
        kn = jnp.concatenate([knew_ref[...], pad], axis=0)
        vn = jnp.concatenate([vnew_ref[...], pad], axis=0)
        sn = lax.dot_general(qx, kn, _NT, preferred_element_type=F32) + bnew_ref[...]
        _softmax_step(sn, vn, m_ref, l_ref, acc_ref)
        o = acc_ref[...] / l_ref[...]
        half = N_SROWS // 2
        o = o[0:half, :] - lam_ref[0:1, 0:1] * o[half:N_SROWS, :]
        o = o * jnp.concatenate([dmask_ref[...]] * DEC_SEQ, axis=0)
        ms = jnp.sum(o * o, axis=-1, keepdims=True) * (1.0 / V_DIM)
        o = o * lax.rsqrt(ms + LN_EPS) * gain_ref[...]
        for t in range(DEC_SEQ):
            o_ref[t:t + 1, :] = jnp.sum(o[t * N_HEADS:(t + 1) * N_HEADS, :], axis=0, keepdims=True)


def _attn_sample_tables(rel_bias):
    far = rel_bias[NUM_BUCKETS - 1]
    q_pos = PAST_LEN + jnp.arange(DEC_SEQ, dtype=jnp.int32)[:, None]
    past = rel_bias[_rel_bucket(q_pos - jnp.arange(PAST_LEN, dtype=jnp.int32)[None, :])] - far
    past = jnp.transpose(past, (0, 2, 1)).reshape(DEC_SEQ * N_HEADS, PAST_LEN)
    rel = q_pos - (PAST_LEN + jnp.arange(LANES, dtype=jnp.int32)[None, :])
    new = jnp.where((rel >= 0)[..., None], rel_bias[_rel_bucket(rel)] - far, NEG_INF)
    new = jnp.transpose(new, (0, 2, 1)).reshape(DEC_SEQ * N_HEADS, LANES)
    head_of_col = jnp.arange(D_MODEL, dtype=jnp.int32)[None, :] // V_DIM
    comp_of_col = (jnp.arange(D_MODEL, dtype=jnp.int32)[None, :] % V_DIM) // HEAD_DIM
    heads = jnp.arange(N_HEADS, dtype=jnp.int32)[:, None]
    dmask = (head_of_col == heads).astype(F32)
    qmask = jnp.stack([(head_of_col == heads) & (comp_of_col == c) for c in range(2)]).astype(F32)
    both = lambda m: jnp.concatenate([m, m], axis=0).astype(F32)
    return both(past), both(new), dmask, qmask


def _attn_sample(q, k_new, v_new, cache_k, cache_v, page_table, tables, lam_tile, gain):
    past, new, dmask, qmask = tables
    n_phys = cache_k.shape[0]
    ck = cache_k.reshape(n_phys, PAGE_SIZE, D_MODEL)
    cv = cache_v.reshape(n_phys, PAGE_SIZE, D_MODEL)
    n_steps = N_PAGES // PAGES_PER_STEP
    keys_per_step = PAGES_PER_STEP * PAGE_SIZE

    def page_spec(r):
        return pl.BlockSpec((None, PAGE_SIZE, D_MODEL),
                            lambda b, s, pt: (pt[b * N_PAGES + s * PAGES_PER_STEP + r], 0, 0))

    per_seq = lambda rows: pl.BlockSpec((None, rows, D_MODEL), lambda b, s, pt: (b, 0, 0))
    const = lambda shape: pl.BlockSpec(shape, lambda b, s, pt: (0,) * len(shape))
    grid_spec = pltpu.PrefetchScalarGridSpec(
        num_scalar_prefetch=1,
        grid=(DEC_BATCH, n_steps),
        in_specs=[per_seq(DEC_SEQ), const(qmask.shape), per_seq(SUBLANES), per_seq(SUBLANES),
                  pl.BlockSpec((N_SROWS, keys_per_step), lambda b, s, pt: (0, s)),
                  const(new.shape), const(dmask.shape), const(lam_tile.shape), const(gain.shape)]
                 + [page_spec(r) for r in range(PAGES_PER_STEP)] * 2,
        out_specs=per_seq(DEC_SEQ),
        scratch_shapes=[pltpu.VMEM((N_SROWS, D_MODEL), BF16), pltpu.VMEM((N_SROWS, 1), F32),
                        pltpu.VMEM((N_SROWS, 1), F32), pltpu.VMEM((N_SROWS, D_MODEL), F32)],
    )
    return pl.pallas_call(
        _attn_sample_kernel,
        grid_spec=grid_spec,
        out_shape=jax.ShapeDtypeStruct((DEC_BATCH, DEC_SEQ, D_MODEL), F32),
        compiler_params=_params(("arbitrary", "arbitrary")),
        name="attn_sample",
    )(page_table.reshape(-1), q, qmask, k_new, v_new, past, new, dmask, lam_tile, gain,
      *([ck] * PAGES_PER_STEP), *([cv] * PAGES_PER_STEP))


def _to_batch_major(a):
    return jnp.transpose(a.reshape(DEC_SEQ, DEC_BATCH, D_MODEL), (1, 0, 2))


def _to_time_major(a):
    return jnp.transpose(a, (1, 0, 2)).reshape(DEC_SEQ * DEC_BATCH, D_MODEL)


def _trunk(x, mods, mods_kv, h0_re, h0_im, attend, w):
    ssm_re, ssm_im = [], []
    k = v = None
    for layer in range(DEPTH):
        if layer < N_A_LAYERS:
            bcat, ccat, tab, atab = w["ssm"][layer]
            if mods.mode == "p":
                g, fr, fi = _ssm_prompt(x, mods, layer, bcat, ccat, w["ssm_d"][layer], tab)
            else:
                g, fr, fi = _ssm_sample(x, mods, layer, bcat, ccat, w["ssm_d"][layer], atab,
                                        h0_re[layer], h0_im[layer])
            ssm_re.append(fr)
            ssm_im.append(fi)
            x = _linear_ln(g, x, mods, layer, 2, w["glu"][layer],
                           w["ln_g"][layer, 0], w["ln_b"][layer, 0], glu=True)
        else:
            j = layer - N_A_LAYERS
            (q,) = _mod_linear(x, mods, layer, 1, 0, w["wq"][j], [BF16], scale=HEAD_DIM ** -0.5)
            o = attend(q, k, v, j)
            x = _linear_ln(o, x, mods, layer, 2, w["wo"][j],
                           w["ln_g"][layer, 0], w["ln_b"][layer, 0], glu=False)
        x = _mlp(x, mods, layer, w["w1"][layer], w["w2"][layer],
                 w["ln_g"][layer, 1], w["ln_b"][layer, 1])
        if layer == N_A_LAYERS - 1:
            k, v = _mod_linear(x, mods_kv, 0, 1, 0, w["wkv"], [F32, F32])
    return x, jnp.stack(ssm_re), jnp.stack(ssm_im), k, v


def kernel(x_prompt, x_sample, state_ssm_re, state_ssm_im, cache_k, cache_v, page_table, c_prompt, c_sample, rel_bias, w_ada, b_ada, ln_g, ln_b, ssm_lam_re, ssm_lam_im, ssm_log_dt, ssm_b_re, ssm_b_im, ssm_c_re, ssm_c_im, ssm_d, ssm_w_glu_a, ssm_w_glu_b, w_ada_kv, b_ada_kv, w_kv, attn_w_q, attn_lam, attn_subln_g, attn_w_o, mlp_w1, mlp_w2):
    w = {
        "ssm": [_ssm_prep(ssm_lam_re[l], ssm_lam_im[l], ssm_log_dt[l], ssm_b_re[l], ssm_b_im[l],
                          ssm_c_re[l], ssm_c_im[l]) for l in range(N_A_LAYERS)],
        "ssm_d": ssm_d,
        "glu": jnp.concatenate([ssm_w_glu_a, ssm_w_glu_b], axis=-1).astype(BF16),
        "wq": attn_w_q.astype(BF16),
        "wo": attn_w_o.astype(BF16),
        "wkv": w_kv.astype(BF16),
        "w1": mlp_w1.astype(BF16),
        "w2": mlp_w2.astype(BF16),
        "ln_g": ln_g,
        "ln_b": ln_b,
    }
    c_all = jnp.concatenate([c_sample, c_prompt, jnp.zeros((BATCH, D_MODEL), F32)], axis=0)
    ada = _ada(c_all, w_ada, b_ada)
    ada_kv = _ada(c_all, w_ada_kv[None], b_ada_kv[None])
    scalars = [_attn_scalars(attn_lam[j], attn_subln_g[j], N_A_LAYERS + j)
               for j in range(DEPTH - N_A_LAYERS)]

    prompt_tab = _attn_prompt_tables(rel_bias)

    def attend_prompt(q, k, v, j):
        return _attn_prompt(q, k, v, prompt_tab, *scalars[j])

    y_p, re_p, im_p, k_p, v_p = _trunk(x_prompt.reshape(BATCH * SEQ, D_MODEL), _Mods(ada, "p"),
                                       _Mods(ada_kv, "p"), None, None, attend_prompt, w)

    sample_tab = _attn_sample_tables(rel_bias)

    def attend_sample(q, k, v, j):
        pad = lambda a: jnp.pad(_to_batch_major(a).astype(BF16),
                                ((0, 0), (0, SUBLANES - DEC_SEQ), (0, 0)))
        o = _attn_sample(_to_batch_major(q), pad(k), pad(v), cache_k, cache_v, page_table,
                         sample_tab, *scalars[j])
        return _to_time_major(o).astype(BF16)

    y_s, re_s, im_s, k_s, v_s = _trunk(_to_time_major(x_sample), _Mods(ada, "s"), _Mods(ada_kv, "s"),
                                       state_ssm_re, state_ssm_im, attend_sample, w)

    return (y_p.reshape(BATCH, SEQ, D_MODEL), _to_batch_major(y_s), re_p, im_p,
            k_p.reshape(BATCH, SEQ, N_HEADS, 2, HEAD_DIM), v_p.reshape(BATCH, SEQ, N_HEADS, V_DIM),
            re_s, im_s,
            _to_batch_major(k_s).reshape(DEC_BATCH, DEC_SEQ, N_HEADS, 2, HEAD_DIM),
            _to_batch_major(v_s).reshape(DEC_BATCH, DEC_SEQ, N_HEADS, V_DIM))
```

```python
import functools
import math

import jax
import jax.numpy as jnp
from jax import lax
from jax.experimental import pallas as pl
from jax.experimental.pallas import tpu as pltpu

F32 = jnp.float32
BF16 = jnp.bfloat16

D_MODEL = 1024
BATCH = 4
SEQ = 4096
DEPTH = 4
DEC_BATCH = 128
DEC_SEQ = 4
PAST_LEN = 2048
PAGE_SIZE = 128
N_PAGES = PAST_LEN // PAGE_SIZE
N_A_LAYERS = DEPTH // 2
SSM_GROUP = 16
SSM_GROUPS = D_MODEL // SSM_GROUP
SSM_STATE = 64
N_STATE = SSM_GROUPS * SSM_STATE
N_HEADS = 8
HEAD_DIM = D_MODEL // (2 * N_HEADS)
V_DIM = 2 * HEAD_DIM
D_FF = 4 * D_MODEL
NUM_BUCKETS = 32
MAX_DISTANCE = 128
N_MOD = 6
ALPHA = (2.0 * DEPTH) ** 0.25
LN_EPS = 1e-5
NEG_INF = -1e30

SUBLANES = 8
LANES = 128
VMEM_LIMIT = 48 * 1024 * 1024

MOD_ROWS = DEC_BATCH + 2 * BATCH
CHUNK = LANES
N_CHUNK = D_MODEL // CHUNK
CHUNK_STATE = N_STATE // N_CHUNK
TM_PROMPT = 512
T_SSM = 512
TQ = 512
HEADS_PER_STEP = 2
PAGES_PER_STEP = 8
LOG2E = math.log2(math.e)
FF_CHUNK = 1024


def _params(sem, vmem=VMEM_LIMIT):
    return pltpu.CompilerParams(dimension_semantics=sem, vmem_limit_bytes=vmem)


def _layer_norm(z, g, b):
    mu = jnp.mean(z, axis=-1, keepdims=True)
    zc = z - mu
    var = jnp.mean(zc * zc, axis=-1, keepdims=True)
    return zc * lax.rsqrt(var + LN_EPS) * g + b


def _ada_kernel(c_ref, w_ref, b_ref, o_ref):
    c = c_ref[...]
    sc = (c * jax.nn.sigmoid(c)).astype(BF16)
    w = w_ref[...].astype(BF16)
    o_ref[...] = jnp.dot(sc, w, preferred_element_type=F32) + b_ref[...]


def _ada(c_all, w, b):
    n_l, _, width = w.shape
    tn = 1024
    return pl.pallas_call(
        _ada_kernel,
        grid=(n_l, width // tn),
        in_specs=[
            pl.BlockSpec((MOD_ROWS, D_MODEL), lambda l, j: (0, 0)),
            pl.BlockSpec((None, D_MODEL, tn), lambda l, j: (l, 0, j)),
            pl.BlockSpec((None, 1, tn), lambda l, j: (l, 0, j)),
        ],
        out_specs=pl.BlockSpec((None, MOD_ROWS, tn), lambda l, j: (l, 0, j)),
        out_shape=jax.ShapeDtypeStruct((n_l, MOD_ROWS, width), F32),
        compiler_params=_params(("arbitrary", "arbitrary")),
        name="ada_mod",
    )(c_all, w, b.reshape(n_l, 1, width))


class _Mods:
    def __init__(self, arr, mode):
        self.mode = mode
        self.arr = arr if mode == "s" else arr.reshape(arr.shape[0], MOD_ROWS, 1, arr.shape[-1])

    def spec(self, layer, j, batch_of):
        if self.mode == "s":
            return pl.BlockSpec((None, DEC_BATCH, D_MODEL), lambda *g: (layer, 0, j))
        return pl.BlockSpec((None, None, 1, D_MODEL),
                            lambda *g: (layer, DEC_BATCH + batch_of(*g), 0, j))


def _row_tiling(mode):
    if mode == "s":
        return DEC_BATCH, lambda i: 0
    tiles_per_batch = SEQ // TM_PROMPT
    return TM_PROMPT, lambda i: i // tiles_per_batch


def _const_spec(shape):
    zeros = (0,) * len(shape)
    return pl.BlockSpec(shape, lambda *g: zeros, pipeline_mode=pl.Buffered(1))


def _mod_linear_kernel(x_ref, sc_ref, sh_ref, w_ref, *o_refs, scale):
    h = (x_ref[...] * (1.0 + sc_ref[...]) + sh_ref[...]).astype(BF16)
    y = jnp.dot(h, w_ref[...], preferred_element_type=F32)
    if scale != 1.0:
        y = y * scale
    for n, o_ref in enumerate(o_refs):
        o_ref[...] = y[:, n * D_MODEL:(n + 1) * D_MODEL].astype(o_ref.dtype)


def _mod_linear(x, mods, layer, j_scale, j_shift, w, out_dtypes, scale=1.0):
    n = x.shape[0]
    tm, batch_of = _row_tiling(mods.mode)
    row = pl.BlockSpec((tm, D_MODEL), lambda i: (i, 0))
    return pl.pallas_call(
        functools.partial(_mod_linear_kernel, scale=scale),
        grid=(n // tm,),
        in_specs=[row, mods.spec(layer, j_scale, batch_of), mods.spec(layer, j_shift, batch_of),
                  _const_spec(w.shape)],
        out_specs=[row] * len(out_dtypes),
        out_shape=[jax.ShapeDtypeStruct((n, D_MODEL), dt) for dt in out_dtypes],
        compiler_params=_params(("arbitrary",)),
        name="mod_linear",
    )(x, mods.arr, mods.arr, w)


def _linear_ln_kernel(a_ref, x_ref, gt_ref, w_ref, g_ref, b_ref, o_ref, *, glu):
    y = jnp.dot(a_ref[...], w_ref[...], preferred_element_type=F32)
    if glu:
        y = y[:, :D_MODEL] * jax.nn.sigmoid(y[:, D_MODEL:])
    z = ALPHA * x_ref[...] + (1.0 + gt_ref[...]) * y
    o_ref[...] = _layer_norm(z, g_ref[...], b_ref[...])


def _linear_ln(a, x, mods, layer, j_gate, w, ln_g, ln_b, glu):
    n = x.shape[0]
    tm, batch_of = _row_tiling(mods.mode)
    row = pl.BlockSpec((tm, D_MODEL), lambda i: (i, 0))
    return pl.pallas_call(
        functools.partial(_linear_ln_kernel, glu=glu),
        grid=(n // tm,),
        in_specs=[row, row, mods.spec(layer, j_gate, batch_of), _const_spec(w.shape),
                  _const_spec((1, D_MODEL)), _const_spec((1, D_MODEL))],
        out_specs=row,
        out_shape=jax.ShapeDtypeStruct((n, D_MODEL), F32),
        compiler_params=_params(("arbitrary",)),
        name="linear_ln",
    )(a, x, mods.arr, w, ln_g.reshape(1, D_MODEL), ln_b.reshape(1, D_MODEL))


def _mlp_kernel(x_ref, sc_ref, sh_ref, gt_ref, w1_ref, w2_ref, g_ref, b_ref, o_ref):
    x = x_ref[...]
    h = (x * (1.0 + sc_ref[...]) + sh_ref[...]).astype(BF16)
    acc = jnp.zeros(x.shape, F32)
    for c in range(D_FF // FF_CHUNK):
        cols = slice(c * FF_CHUNK, (c + 1) * FF_CHUNK)
        hid = jnp.dot(h, w1_ref[:, cols], preferred_element_type=F32)
        hid = jnp.square(jnp.maximum(hid, 0.0)).astype(BF16)
        acc = acc + jnp.dot(hid, w2_ref[cols, :], preferred_element_type=F32)
    z = ALPHA * x + (1.0 + gt_ref[...]) * acc
    o_ref[...] = _layer_norm(z, g_ref[...], b_ref[...])


def _mlp(x, mods, layer, w1, w2, ln_g, ln_b):
    n = x.shape[0]
    tm, batch_of = _row_tiling(mods.mode)
    row = pl.BlockSpec((tm, D_MODEL), lambda i: (i, 0))
    return pl.pallas_call(
        _mlp_kernel,
        grid=(n // tm,),
        in_specs=[row, mods.spec(layer, 4, batch_of), mods.spec(layer, 3, batch_of),
                  mods.spec(layer, 5, batch_of), _const_spec(w1.shape), _const_spec(w2.shape),
                  _const_spec((1, D_MODEL)), _const_spec((1, D_MODEL))],
        out_specs=row,
        out_shape=jax.ShapeDtypeStruct((n, D_MODEL), F32),
        compiler_params=_params(("arbitrary",)),
        name="mlp_ln",
    )(x, mods.arr, mods.arr, mods.arr, w1, w2, ln_g.reshape(1, D_MODEL), ln_b.reshape(1, D_MODEL))


def _cmul(ar, ai, br, bi):
    return ar * br - ai * bi, ar * bi + ai * br


def _ssm_prep(lam_re, lam_im, log_dt, b_re, b_im, c_re, c_im):
    dt = jnp.exp(log_dt)[:, None]
    mag = jnp.exp(lam_re * dt)
    ar = mag * jnp.cos(lam_im * dt)
    ai = mag * jnp.sin(lam_im * dt)
    den = lam_re * lam_re + lam_im * lam_im
    er = ((ar - 1.0) * lam_re + ai * lam_im) / den
    ei = (ai * lam_re - (ar - 1.0) * lam_im) / den
    bbr = er[..., None] * b_re - ei[..., None] * b_im
    bbi = er[..., None] * b_im + ei[..., None] * b_re
    gpc = CHUNK // SSM_GROUP
    eye = jnp.eye(gpc, dtype=F32)

    def b_blocks(m):
        m = m.reshape(N_CHUNK, gpc, SSM_STATE, SSM_GROUP)
        return jnp.einsum("kgpi,gf->kgifp", m, eye).reshape(N_CHUNK, CHUNK, CHUNK_STATE)

    def c_blocks(m):
        m = m.reshape(N_CHUNK, gpc, SSM_GROUP, SSM_STATE)
        return jnp.einsum("kghp,gf->kgpfh", m, eye).reshape(N_CHUNK, CHUNK_STATE, CHUNK)

    bcat = jnp.concatenate([b_blocks(bbr), b_blocks(bbi)], axis=-1).astype(BF16)
    ccat = jnp.concatenate([c_blocks(c_re), -c_blocks(c_im)], axis=1).astype(BF16)

    a1 = (ar.reshape(1, N_STATE), ai.reshape(1, N_STATE))
    a2 = _cmul(*a1, *a1)
    a3 = _cmul(*a2, *a1)
    a4 = _cmul(*a2, *a2)
    a5 = _cmul(*a4, *a1)
    a6 = _cmul(*a4, *a2)
    a7 = _cmul(*a4, *a3)
    a8 = _cmul(*a4, *a4)
    rows = lax.broadcasted_iota(jnp.int32, (SUBLANES, N_STATE), 0)

    def masked(a, first):
        return [jnp.where(rows >= first, jnp.broadcast_to(p, rows.shape), 0.0) for p in a]

    powers = [jnp.concatenate(p, axis=0) for p in zip(a1, a2, a3, a4, a5, a6, a7, a8)]
    tab = jnp.stack(masked(a1, 1) + masked(a2, 2) + masked(a4, 4) + powers + masked(a8, 0))
    atab = jnp.stack(masked(a1, 0))
    return bcat, ccat, tab, atab


def _ssm_prompt_kernel(x_ref, sc_ref, sh_ref, bcat_ref, ccat_ref, d_ref, tab_ref,
                       g_ref, hre_ref, him_ref, buf_ref, car_ref):
    @pl.when(pl.program_id(1) == 0)
    def _():
        car_ref[...] = jnp.zeros_like(car_ref)

    h = x_ref[...] * (1.0 + sc_ref[...]) + sh_ref[...]
    hb = h.astype(BF16)
    n_tiles = x_ref.shape[0] // SUBLANES
    half = CHUNK_STATE
    for k in range(N_CHUNK):
        ch = slice(k * CHUNK, (k + 1) * CHUNK)
        st = slice(k * half, (k + 1) * half)
        buf_ref[...] = jnp.dot(hb[:, ch], bcat_ref[k], preferred_element_type=F32)

        def tile_scan(i, carry, st=st):
            cr, ci = carry
            r0 = pl.multiple_of(i * SUBLANES, SUBLANES)
            vr = buf_ref[pl.ds(r0, SUBLANES), 0:half]
            vi = buf_ref[pl.ds(r0, SUBLANES), half:2 * half]
            for lvl, shift in enumerate((1, 2, 4)):
                pr, pi = _cmul(tab_ref[2 * lvl, :, st], tab_ref[2 * lvl + 1, :, st],
                               pltpu.roll(vr, shift, 0), pltpu.roll(vi, shift, 0))
                vr, vi = vr + pr, vi + pi
            pr, pi = _cmul(tab_ref[6, :, st], tab_ref[7, :, st], cr, ci)
            buf_ref[pl.ds(r0, SUBLANES), 0:half] = vr + pr
            buf_ref[pl.ds(r0, SUBLANES), half:2 * half] = vi + pi
            pr, pi = _cmul(tab_ref[8, :, st], tab_ref[9, :, st], cr, ci)
            last = SUBLANES - 1
            return (jnp.broadcast_to(vr[last:, :], vr.shape) + pr,
                    jnp.broadcast_to(vi[last:, :], vi.shape) + pi)

        cr, ci = lax.fori_loop(0, n_tiles, tile_scan, (car_ref[0, :, st], car_ref[1, :, st]))
        car_ref[0, :, st] = cr
        car_ref[1, :, st] = ci
        y = jnp.dot(buf_ref[...].astype(BF16), ccat_ref[k], preferred_element_type=F32)
        y = y + d_ref[:, ch] * h[:, ch]
        g_ref[:, ch] = jax.nn.gelu(y).astype(g_ref.dtype)
    hre_ref[...] = car_ref[0, 0:1, :]
    him_ref[...] = car_ref[1, 0:1, :]


def _ssm_prompt(x, mods, layer, bcat, ccat, d_skip, tab):
    n_t = SEQ // T_SSM
    row = pl.BlockSpec((T_SSM, D_MODEL), lambda b, c: (b * n_t + c, 0))
    fin = pl.BlockSpec((None, 1, N_STATE), lambda b, c: (b, 0, 0))
    batch_of = lambda b, c: b
    g, hre, him = pl.pallas_call(
        _ssm_prompt_kernel,
        grid=(BATCH, n_t),
        in_specs=[row, mods.spec(layer, 1, batch_of), mods.spec(layer, 0, batch_of),
                  _const_spec(bcat.shape), _const_spec(ccat.shape), _const_spec((1, D_MODEL)),
                  _const_spec(tab.shape)],
        out_specs=[row, fin, fin],
        out_shape=[jax.ShapeDtypeStruct((BATCH * SEQ, D_MODEL), BF16),
                   jax.ShapeDtypeStruct((BATCH, 1, N_STATE), F32),
                   jax.ShapeDtypeStruct((BATCH, 1, N_STATE), F32)],
        scratch_shapes=[pltpu.VMEM((T_SSM, 2 * CHUNK_STATE), F32),
                        pltpu.VMEM((2, SUBLANES, N_STATE), F32)],
        compiler_params=_params(("arbitrary", "arbitrary")),
        name="ssm_prompt",
    )(x, mods.arr, mods.arr, bcat, ccat, d_skip.reshape(1, D_MODEL), tab)
    shape = (BATCH, SSM_GROUPS, SSM_STATE)
    return g, hre.reshape(shape), him.reshape(shape)


def _ssm_sample_kernel(x_ref, sc_ref, sh_ref, bcat_ref, ccat_ref, d_ref, a_ref, h0r_ref, h0i_ref,
                       g_ref, hre_ref, him_ref):
    tile = lambda m: jnp.concatenate([m] * DEC_SEQ, axis=0)
    h = x_ref[...] * (1.0 + tile(sc_ref[...])) + tile(sh_ref[...])
    bu = jnp.dot(h.astype(BF16), bcat_ref[...], preferred_element_type=F32)
    ar = a_ref[0, 0:1, :]
    ai = a_ref[1, 0:1, :]
    sr = h0r_ref[...]
    si = h0i_ref[...]
    states = []
    for t in range(DEC_SEQ):
        rows = slice(t * DEC_BATCH, (t + 1) * DEC_BATCH)
        pr, pi = _cmul(ar, ai, sr, si)
        sr = pr + bu[rows, :CHUNK_STATE]
        si = pi + bu[rows, CHUNK_STATE:]
        states.append(jnp.concatenate([sr, si], axis=1))
    xs = jnp.concatenate(states, axis=0).astype(BF16)
    y = jnp.dot(xs, ccat_ref[...], preferred_element_type=F32) + d_ref[...] * h
    g_ref[...] = jax.nn.gelu(y).astype(g_ref.dtype)
    hre_ref[...] = sr
    him_ref[...] = si


def _ssm_sample(x, mods, layer, bcat, ccat, d_skip, atab, h0_re, h0_im):
    n = DEC_BATCH * DEC_SEQ
    col = pl.BlockSpec((n, CHUNK), lambda k: (0, k))
    st = pl.BlockSpec((DEC_BATCH, CHUNK_STATE), lambda k: (0, k))
    mod = lambda j: pl.BlockSpec((None, DEC_BATCH, CHUNK), lambda k: (layer, 0, j * N_CHUNK + k))
    g, hre, him = pl.pallas_call(
        _ssm_sample_kernel,
        grid=(N_CHUNK,),
        in_specs=[col, mod(1), mod(0),
                  pl.BlockSpec((None, CHUNK, 2 * CHUNK_STATE), lambda k: (k, 0, 0)),
                  pl.BlockSpec((None, 2 * CHUNK_STATE, CHUNK), lambda k: (k, 0, 0)),
                  pl.BlockSpec((1, CHUNK), lambda k: (0, k)),
                  pl.BlockSpec((2, SUBLANES, CHUNK_STATE), lambda k: (0, 0, k)),
                  st, st],
        out_specs=[col, st, st],
        out_shape=[jax.ShapeDtypeStruct((n, D_MODEL), BF16),
                   jax.ShapeDtypeStruct((DEC_BATCH, N_STATE), F32),
                   jax.ShapeDtypeStruct((DEC_BATCH, N_STATE), F32)],
        compiler_params=_params(("arbitrary",)),
        name="ssm_sample",
    )(x, mods.arr, mods.arr, bcat, ccat, d_skip.reshape(1, D_MODEL), atab,
      h0_re.reshape(DEC_BATCH, N_STATE), h0_im.reshape(DEC_BATCH, N_STATE))
    shape = (DEC_BATCH, SSM_GROUPS, SSM_STATE)
    return g, hre.reshape(shape), him.reshape(shape)


def _rel_bucket(rel):
    n = jnp.maximum(rel, 0)
    max_exact = NUM_BUCKETS // 2
    large = max_exact + (jnp.log(jnp.maximum(n, 1).astype(F32) / max_exact)
                         / math.log(MAX_DISTANCE / max_exact) * (NUM_BUCKETS - max_exact)).astype(jnp.int32)
    large = jnp.minimum(large, NUM_BUCKETS - 1)
    return jnp.where(n < max_exact, n, large)


def _attn_scalars(attn_lam, subln_g, layer):
    lam_init = 0.8 - 0.6 * math.exp(-0.3 * layer)
    lam = (jnp.exp(jnp.sum(attn_lam[0] * attn_lam[1])) - jnp.exp(jnp.sum(attn_lam[2] * attn_lam[3]))
           + lam_init)
    lam_tile = jnp.full((SUBLANES, LANES), lam, F32)
    gain = (subln_g * (1.0 - lam_init)).reshape(1, V_DIM)
    return lam_tile, gain


def _bias_by_distance(rel_bias, n):
    bv = rel_bias[_rel_bucket(jnp.arange(n, dtype=jnp.int32))]
    return ((bv - rel_bias[NUM_BUCKETS - 1]) * LOG2E).T


def _toeplitz(w, n):
    u = jnp.concatenate([w[::-1], w[:1]])
    r = jnp.tile(u, n)[:n * (2 * n - 1)].reshape(n, 2 * n - 1)
    return r[:, n - 1:]


def _softmax_update(s, m_prev):
    m_new = jnp.maximum(m_prev, jnp.max(s, axis=1, keepdims=True))
    p = jnp.exp2(s - jnp.concatenate([m_new] * (s.shape[1] // LANES), axis=1))
    return m_new, p.astype(BF16), jnp.exp2(m_prev - m_new)


def _with_ones(v):
    return jnp.concatenate([v, jnp.ones((v.shape[0], V_DIM), BF16)], axis=1)


def _acc_update(acc, alpha, p, vx):
    return (jnp.concatenate([alpha, alpha], axis=1) * acc
            + jnp.dot(p, vx, preferred_element_type=F32))


_NT = (((1,), (1,)), ((), ()))


def _attn_prompt_kernel(q_ref, k_ref, v_ref, tab_ref, lam_ref, gain_ref, o_ref,
                        q2_ref, m_ref, acc_ref):
    i = pl.program_id(2)
    for hh in range(HEADS_PER_STEP):
        q = q_ref[:, hh * V_DIM:(hh + 1) * V_DIM]
        lane = lax.broadcasted_iota(jnp.int32, q.shape, 1)
        zero = jnp.zeros_like(q)
        q2_ref[hh, 0:TQ, :] = jnp.where(lane < HEAD_DIM, q, zero)
        q2_ref[hh, TQ:2 * TQ, :] = jnp.where(lane >= HEAD_DIM, q, zero)
    m_ref[...] = jnp.full(m_ref.shape, NEG_INF, F32)
    acc_ref[...] = jnp.zeros_like(acc_ref)

    def step(j, table):
        rows = pl.ds(pl.multiple_of(j * TQ, TQ), TQ)
        for hh in range(HEADS_PER_STEP):
            cols = slice(hh * V_DIM, (hh + 1) * V_DIM)
            s = lax.dot_general(q2_ref[hh], k_ref[rows, cols].astype(BF16), _NT,
                                preferred_element_type=F32)
            if table is not None:
                bias = tab_ref[hh, table]
                s = s + jnp.concatenate([bias, bias], axis=0)
            m_new, p, alpha = _softmax_update(s, m_ref[hh])
            acc_ref[hh] = _acc_update(acc_ref[hh], alpha, p,
                                      _with_ones(v_ref[rows, cols].astype(BF16)))
            m_ref[hh] = m_new

    def far(j, c):
        step(j, None)
        return c

    lax.fori_loop(0, jnp.maximum(i - 1, 0), far, 0)

    @pl.when(i >= 1)
    def _():
        step(i - 1, 1)

    step(i, 0)

    for hh in range(HEADS_PER_STEP):
        acc = acc_ref[hh]
        o = acc[:, :V_DIM] / acc[:, V_DIM:]
        o = o[0:TQ, :] - lam_ref[0:1, 0:1] * o[TQ:2 * TQ, :]
        o = o * lax.rsqrt(jnp.mean(o * o, axis=-1, keepdims=True) + LN_EPS) * gain_ref[...]
        o_ref[:, hh * V_DIM:(hh + 1) * V_DIM] = o.astype(o_ref.dtype)


def _attn_prompt_tables(rel_bias):
    bv = _bias_by_distance(rel_bias, 2 * TQ)
    masked = jnp.concatenate([jnp.full((N_HEADS, TQ - 1), NEG_INF, F32), bv[:, :TQ]], axis=1)
    toeplitz = jax.vmap(lambda w: _toeplitz(w, TQ))
    return jnp.stack([toeplitz(masked), toeplitz(bv[:, 1:])], axis=1)


def _attn_prompt(q, k, v, tab, lam_tile, gain):
    n_q = SEQ // TQ
    width = HEADS_PER_STEP * V_DIM
    shape3 = (BATCH, SEQ, D_MODEL)
    qspec = pl.BlockSpec((None, TQ, width), lambda b, h, i: (b, i, h))
    kvspec = pl.BlockSpec((None, SEQ, width), lambda b, h, i: (b, 0, h))
    o = pl.pallas_call(
        _attn_prompt_kernel,
        grid=(BATCH, N_HEADS // HEADS_PER_STEP, n_q),
        in_specs=[qspec, kvspec, kvspec,
                  pl.BlockSpec((HEADS_PER_STEP, 2, TQ, TQ), lambda b, h, i: (h, 0, 0, 0)),
                  pl.BlockSpec((SUBLANES, LANES), lambda b, h, i: (0, 0)),
                  pl.BlockSpec((1, V_DIM), lambda b, h, i: (0, 0))],
        out_specs=qspec,
        out_shape=jax.ShapeDtypeStruct(shape3, BF16),
        scratch_shapes=[pltpu.VMEM((HEADS_PER_STEP, 2 * TQ, V_DIM), BF16),
                        pltpu.VMEM((HEADS_PER_STEP, 2 * TQ, LANES), F32),
                        pltpu.VMEM((HEADS_PER_STEP, 2 * TQ, 2 * V_DIM), F32)],
        compiler_params=_params(("arbitrary", "arbitrary", "arbitrary")),
        name="attn_prompt",
    )(q.reshape(shape3), k.reshape(shape3), v.reshape(shape3), tab, lam_tile, gain)
    return o.reshape(BATCH * SEQ, D_MODEL)


N_SROWS = N_HEADS * 2 * DEC_SEQ


def _attn_sample_kernel(pt_ref, q_ref, qmask_ref, knew_ref, vnew_ref, bias_ref, bnew_ref,
                        lam_ref, gain_ref, *rest):
    del pt_ref
    k_refs = rest[:PAGES_PER_STEP]
    v_refs = rest[PAGES_PER_STEP:2 * PAGES_PER_STEP]
    o_ref, qx_ref, m_ref, acc_ref = rest[2 * PAGES_PER_STEP:]
    step = pl.program_id(1)
    rows_per_head = 2 * DEC_SEQ

    @pl.when(step == 0)
    def _():
        q = q_ref[...].astype(F32)
        q = jnp.concatenate([q, q], axis=0)
        qx_ref[...] = (jnp.concatenate([q] * N_HEADS, axis=0) * qmask_ref[...]).astype(BF16)
        m_ref[...] = jnp.full(m_ref.shape, NEG_INF, F32)
        acc_ref[...] = jnp.zeros_like(acc_ref)

    def update(s, values_of_head):
        m_new, p, alpha = _softmax_update(s, m_ref[...])
        for h in range(N_HEADS):
            r = slice(h * rows_per_head, (h + 1) * rows_per_head)
            acc_ref[r, :] = _acc_update(acc_ref[r, :], alpha[r, :], p[r, :],
                                        _with_ones(values_of_head(h)))
        m_ref[...] = m_new

    qx = qx_ref[...]
    kt = jnp.concatenate([r[...].astype(BF16) for r in k_refs], axis=1)
    s = jnp.dot(qx, kt, preferred_element_type=F32) + bias_ref[...]
    update(s, lambda h: jnp.concatenate(
        [r[pl.ds(h, PAGE_SIZE, stride=N_HEADS), :].astype(BF16) for r in v_refs], axis=0))

    @pl.when(step == pl.num_programs(1) - 1)
    def _():
        pad = jnp.zeros((LANES - SUBLANES, D_MODEL), BF16)
        kn = jnp.concatenate([knew_ref[...], pad], axis=0)
        vn = jnp.concatenate([vnew_ref[...], pad], axis=0)
        sn = lax.dot_general(qx, kn, _NT, preferred_element_type=F32) + bnew_ref[...]
        update(sn, lambda h: vn[:, h * V_DIM:(h + 1) * V_DIM])
        acc = acc_ref[...]
        o = acc[:, :V_DIM] / acc[:, V_DIM:]
        heads = []
        for h in range(N_HEADS):
            r0 = h * rows_per_head
            oh = o[r0:r0 + DEC_SEQ, :] - lam_ref[0:1, 0:1] * o[r0 + DEC_SEQ:r0 + rows_per_head, :]
            oh = oh * lax.rsqrt(jnp.mean(oh * oh, axis=-1, keepdims=True) + LN_EPS) * gain_ref[...]
            heads.append(oh)
        o_ref[...] = jnp.concatenate(heads, axis=1)


def _attn_sample_tables(rel_bias):
    bv = _bias_by_distance(rel_bias, PAST_LEN + DEC_SEQ)
    past = jnp.stack([bv[:, t + 1:t + 1 + PAST_LEN][:, ::-1] for t in range(DEC_SEQ)], axis=1)
    rel = (jnp.arange(DEC_SEQ, dtype=jnp.int32)[:, None]
           - jnp.arange(LANES, dtype=jnp.int32)[None, :])
    new = jnp.where(rel >= 0, bv[:, jnp.clip(rel, 0, DEC_SEQ - 1)], NEG_INF)
    rows = lambda a: jnp.broadcast_to(a[:, None], (N_HEADS, 2) + a.shape[1:]).reshape(
        N_SROWS, a.shape[-1])
    col = jnp.arange(D_MODEL, dtype=jnp.int32)[None, :]
    row = jnp.arange(N_SROWS, dtype=jnp.int32)[:, None]
    qmask = ((col // V_DIM == row // (2 * DEC_SEQ))
             & ((col % V_DIM) // HEAD_DIM == (row // DEC_SEQ) % 2)).astype(F32)
    return rows(past), rows(new), qmask


def _attn_sample(q, k_new, v_new, cache_k, cache_v, page_table, tables, lam_tile, gain):
    past, new, qmask = tables
    n_phys = cache_k.shape[0]
    ck = jnp.transpose(cache_k, (0, 2, 3, 4, 1)).reshape(n_phys, D_MODEL, PAGE_SIZE)
    cv = cache_v.reshape(n_phys, PAGE_SIZE * N_HEADS, V_DIM)
    n_steps = N_PAGES // PAGES_PER_STEP
    keys_per_step = PAGES_PER_STEP * PAGE_SIZE

    def page_spec(r):
        return pl.BlockSpec((None, D_MODEL, PAGE_SIZE),
                            lambda b, s, pt: (pt[b * N_PAGES + s * PAGES_PER_STEP + r], 0, 0))

    per_seq = lambda rows: pl.BlockSpec((None, rows, D_MODEL), lambda b, s, pt: (b, 0, 0))
    const = lambda shape: pl.BlockSpec(shape, lambda b, s, pt: (0,) * len(shape))
    grid_spec = pltpu.PrefetchScalarGridSpec(
        num_scalar_prefetch=1,
        grid=(DEC_BATCH, n_steps),
        in_specs=[per_seq(DEC_SEQ), const(qmask.shape), per_seq(SUBLANES), per_seq(SUBLANES),
                  pl.BlockSpec((N_SROWS, keys_per_step), lambda b, s, pt: (0, s)),
                  const(new.shape), const(lam_tile.shape), const(gain.shape)]
                 + [page_spec(r) for r in range(PAGES_PER_STEP)] * 2,
        out_specs=per_seq(DEC_SEQ),
        scratch_shapes=[pltpu.VMEM((N_SROWS, D_MODEL), BF16), pltpu.VMEM((N_SROWS, LANES), F32),
                        pltpu.VMEM((N_SROWS, 2 * V_DIM), F32)],
    )
    return pl.pallas_call(
        _attn_sample_kernel,
        grid_spec=grid_spec,
        out_shape=jax.ShapeDtypeStruct((DEC_BATCH, DEC_SEQ, D_MODEL), F32),
        compiler_params=_params(("arbitrary", "arbitrary")),
        name="attn_sample",
    )(page_table.reshape(-1), q, qmask, k_new, v_new, past, new, lam_tile, gain,
      *([ck] * PAGES_PER_STEP), *([cv] * PAGES_PER_STEP))


def _to_batch_major(a):
    return jnp.transpose(a.reshape(DEC_SEQ, DEC_BATCH, D_MODEL), (1, 0, 2))


def _to_time_major(a):
    return jnp.transpose(a, (1, 0, 2)).reshape(DEC_SEQ * DEC_BATCH, D_MODEL)


def _trunk(x, mods, mods_kv, h0_re, h0_im, attend, w):
    ssm_re, ssm_im = [], []
    k = v = None
    for layer in range(DEPTH):
        if layer < N_A_LAYERS:
            bcat, ccat, tab, atab = w["ssm"][layer]
            if mods.mode == "p":
                g, fr, fi = _ssm_prompt(x, mods, layer, bcat, ccat, w["ssm_d"][layer], tab)
            else:
                g, fr, fi = _ssm_sample(x, mods, layer, bcat, ccat, w["ssm_d"][layer], atab,
                                        h0_re[layer], h0_im[layer])
            ssm_re.append(fr)
            ssm_im.append(fi)
            x = _linear_ln(g, x, mods, layer, 2, w["glu"][layer],
                           w["ln_g"][layer, 0], w["ln_b"][layer, 0], glu=True)
        else:
            j = layer - N_A_LAYERS
            (q,) = _mod_linear(x, mods, layer, 1, 0, w["wq"][j], [BF16],
                               scale=HEAD_DIM ** -0.5 * LOG2E)
            o = attend(q, k, v, j)
            x = _linear_ln(o, x, mods, layer, 2, w["wo"][j],
                           w["ln_g"][layer, 0], w["ln_b"][layer, 0], glu=False)
        x = _mlp(x, mods, layer, w["w1"][layer], w["w2"][layer],
                 w["ln_g"][layer, 1], w["ln_b"][layer, 1])
        if layer == N_A_LAYERS - 1:
            k, v = _mod_linear(x, mods_kv, 0, 1, 0, w["wkv"], [F32, F32])
    return x, jnp.stack(ssm_re), jnp.stack(ssm_im), k, v


def kernel(x_prompt, x_sample, state_ssm_re, state_ssm_im, cache_k, cache_v, page_table, c_prompt, c_sample, rel_bias, w_ada, b_ada, ln_g, ln_b, ssm_lam_re, ssm_lam_im, ssm_log_dt, ssm_b_re, ssm_b_im, ssm_c_re, ssm_c_im, ssm_d, ssm_w_glu_a, ssm_w_glu_b, w_ada_kv, b_ada_kv, w_kv, attn_w_q, attn_lam, attn_subln_g, attn_w_o, mlp_w1, mlp_w2):
    w = {
        "ssm": [_ssm_prep(ssm_lam_re[l], ssm_lam_im[l], ssm_log_dt[l], ssm_b_re[l], ssm_b_im[l],
                          ssm_c_re[l], ssm_c_im[l]) for l in range(N_A_LAYERS)],
        "ssm_d": ssm_d,
        "glu": jnp.concatenate([ssm_w_glu_a, ssm_w_glu_b], axis=-1).astype(BF16),
        "wq": attn_w_q.astype(BF16),
        "wo": attn_w_o.astype(BF16),
        "wkv": w_kv.astype(BF16),
        "w1": mlp_w1.astype(BF16),
        "w2": mlp_w2.astype(BF16),
        "ln_g": ln_g,
        "ln_b": ln_b,
    }
    c_all = jnp.concatenate([c_sample, c_prompt, jnp.zeros((BATCH, D_MODEL), F32)], axis=0)
    ada = _ada(c_all, w_ada, b_ada)
    ada_kv = _ada(c_all, w_ada_kv[None], b_ada_kv[None])
    scalars = [_attn_scalars(attn_lam[j], attn_subln_g[j], N_A_LAYERS + j)
               for j in range(DEPTH - N_A_LAYERS)]

    prompt_tab = _attn_prompt_tables(rel_bias)

    def attend_prompt(q, k, v, j):
        return _attn_prompt(q, k, v, prompt_tab, *scalars[j])

    y_p, re_p, im_p, k_p, v_p = _trunk(x_prompt.reshape(BATCH * SEQ, D_MODEL), _Mods(ada, "p"),
                                       _Mods(ada_kv, "p"), None, None, attend_prompt, w)

    sample_tab = _attn_sample_tables(rel_bias)

    def attend_sample(q, k, v, j):
        pad = lambda a: jnp.pad(_to_batch_major(a).astype(BF16),
                                ((0, 0), (0, SUBLANES - DEC_SEQ), (0, 0)))
        o = _attn_sample(_to_batch_major(q), pad(k), pad(v), cache_k, cache_v, page_table,
                         sample_tab, *scalars[j])
        return _to_time_major(o).astype(BF16)

    y_s, re_s, im_s, k_s, v_s = _trunk(_to_time_major(x_sample), _Mods(ada, "s"), _Mods(ada_kv, "s"),
                                       state_ssm_re, state_ssm_im, attend_sample, w)

    return (y_p.reshape(BATCH, SEQ, D_MODEL), _to_batch_major(y_s), re_p, im_p,
            k_p.reshape(BATCH, SEQ, N_HEADS, 2, HEAD_DIM), v_p.reshape(BATCH, SEQ, N_HEADS, V_DIM),
            re_s, im_s,
            _to_batch_major(k_s).reshape(DEC_BATCH, DEC_SEQ, N_HEADS, 2, HEAD_DIM),
            _to_batch_major(v_s).reshape(DEC_BATCH, DEC_SEQ, N_HEADS, V_DIM))
```

```python
import functools
import math

import jax
import jax.numpy as jnp
from jax import lax
from jax.experimental import pallas as pl
from jax.experimental.pallas import tpu as pltpu

F32 = jnp.float32
BF16 = jnp.bfloat16

D_MODEL = 1024
BATCH = 4
SEQ = 4096
DEPTH = 4
DEC_BATCH = 128
DEC_SEQ = 4
PAST_LEN = 2048
PAGE_SIZE = 128
N_PAGES = PAST_LEN // PAGE_SIZE
N_A_LAYERS = DEPTH // 2
SSM_GROUP = 16
SSM_GROUPS = D_MODEL // SSM_GROUP
SSM_STATE = 64
N_STATE = SSM_GROUPS * SSM_STATE
N_HEADS = 8
HEAD_DIM = D_MODEL // (2 * N_HEADS)
V_DIM = 2 * HEAD_DIM
D_FF = 4 * D_MODEL
NUM_BUCKETS = 32
MAX_DISTANCE = 128
N_MOD = 6
ALPHA = (2.0 * DEPTH) ** 0.25
LN_EPS = 1e-5
NEG_INF = -1e30

SUBLANES = 8
LANES = 128
VMEM_LIMIT = 48 * 1024 * 1024

MOD_ROWS = DEC_BATCH + 2 * BATCH
CHUNK = LANES
N_CHUNK = D_MODEL // CHUNK
CHUNK_STATE = N_STATE // N_CHUNK
TM_PROMPT = 512
TILE = SUBLANES
T_SSM = 512
TQ = 512
HEADS_PER_STEP = 2
PAGES_PER_STEP = 8
LOG2E = math.log2(math.e)
FF_CHUNK = 1024


def _params(sem, vmem=VMEM_LIMIT):
    return pltpu.CompilerParams(dimension_semantics=sem, vmem_limit_bytes=vmem)


def _layer_norm(z, g, b):
    mu = jnp.mean(z, axis=-1, keepdims=True)
    zc = z - mu
    var = jnp.mean(zc * zc, axis=-1, keepdims=True)
    return zc * lax.rsqrt(var + LN_EPS) * g + b


def _ada_kernel(c_ref, w_ref, b_ref, o_ref):
    c = c_ref[...]
    sc = (c * jax.nn.sigmoid(c)).astype(BF16)
    w = w_ref[...].astype(BF16)
    o_ref[...] = jnp.dot(sc, w, preferred_element_type=F32) + b_ref[...]


def _ada(c_all, w, b):
    n_l, _, width = w.shape
    tn = 1024
    return pl.pallas_call(
        _ada_kernel,
        grid=(n_l, width // tn),
        in_specs=[
            pl.BlockSpec((MOD_ROWS, D_MODEL), lambda l, j: (0, 0)),
            pl.BlockSpec((None, D_MODEL, tn), lambda l, j: (l, 0, j)),
            pl.BlockSpec((None, 1, tn), lambda l, j: (l, 0, j)),
        ],
        out_specs=pl.BlockSpec((None, MOD_ROWS, tn), lambda l, j: (l, 0, j)),
        out_shape=jax.ShapeDtypeStruct((n_l, MOD_ROWS, width), F32),
        compiler_params=_params(("arbitrary", "arbitrary")),
        name="ada_mod",
    )(c_all, w, b.reshape(n_l, 1, width))


class _Mods:
    def __init__(self, arr, mode):
        self.mode = mode
        self.arr = arr if mode == "s" else arr.reshape(arr.shape[0], MOD_ROWS, 1, arr.shape[-1])

    def spec(self, layer, j, batch_of):
        if self.mode == "s":
            return pl.BlockSpec((None, DEC_BATCH, D_MODEL), lambda *g: (layer, 0, j))
        return pl.BlockSpec((None, None, 1, D_MODEL),
                            lambda *g: (layer, DEC_BATCH + batch_of(*g), 0, j))


def _row_tiling(mode):
    if mode == "s":
        return DEC_BATCH, lambda i: 0
    tiles_per_batch = SEQ // TM_PROMPT
    return TM_PROMPT, lambda i: i // tiles_per_batch


def _const_spec(shape):
    zeros = (0,) * len(shape)
    return pl.BlockSpec(shape, lambda *g: zeros, pipeline_mode=pl.Buffered(1))


def _mod_linear_kernel(x_ref, sc_ref, sh_ref, w_ref, *o_refs, scale):
    h = (x_ref[...] * (1.0 + sc_ref[...]) + sh_ref[...]).astype(BF16)
    y = jnp.dot(h, w_ref[...], preferred_element_type=F32)
    if scale != 1.0:
        y = y * scale
    for n, o_ref in enumerate(o_refs):
        o_ref[...] = y[:, n * D_MODEL:(n + 1) * D_MODEL].astype(o_ref.dtype)


def _mod_linear(x, mods, layer, j_scale, j_shift, w, out_dtypes, scale=1.0):
    n = x.shape[0]
    tm, batch_of = _row_tiling(mods.mode)
    row = pl.BlockSpec((tm, D_MODEL), lambda i: (i, 0))
    return pl.pallas_call(
        functools.partial(_mod_linear_kernel, scale=scale),
        grid=(n // tm,),
        in_specs=[row, mods.spec(layer, j_scale, batch_of), mods.spec(layer, j_shift, batch_of),
                  _const_spec(w.shape)],
        out_specs=[row] * len(out_dtypes),
        out_shape=[jax.ShapeDtypeStruct((n, D_MODEL), dt) for dt in out_dtypes],
        compiler_params=_params(("arbitrary",)),
        name="mod_linear",
    )(x, mods.arr, mods.arr, w)


def _linear_ln_kernel(a_ref, x_ref, gt_ref, w_ref, g_ref, b_ref, o_ref, *, glu):
    y = jnp.dot(a_ref[...], w_ref[...], preferred_element_type=F32)
    if glu:
        y = y[:, :D_MODEL] * jax.nn.sigmoid(y[:, D_MODEL:])
    z = ALPHA * x_ref[...] + (1.0 + gt_ref[...]) * y
    o_ref[...] = _layer_norm(z, g_ref[...], b_ref[...])


def _linear_ln(a, x, mods, layer, j_gate, w, ln_g, ln_b, glu):
    n = x.shape[0]
    tm, batch_of = _row_tiling(mods.mode)
    row = pl.BlockSpec((tm, D_MODEL), lambda i: (i, 0))
    return pl.pallas_call(
        functools.partial(_linear_ln_kernel, glu=glu),
        grid=(n // tm,),
        in_specs=[row, row, mods.spec(layer, j_gate, batch_of), _const_spec(w.shape),
                  _const_spec((1, D_MODEL)), _const_spec((1, D_MODEL))],
        out_specs=row,
        out_shape=jax.ShapeDtypeStruct((n, D_MODEL), F32),
        compiler_params=_params(("arbitrary",)),
        name="linear_ln",
    )(a, x, mods.arr, w, ln_g.reshape(1, D_MODEL), ln_b.reshape(1, D_MODEL))


def _mlp_kernel(x_ref, sc_ref, sh_ref, gt_ref, w1_ref, w2_ref, g_ref, b_ref, o_ref):
    x = x_ref[...]
    h = (x * (1.0 + sc_ref[...]) + sh_ref[...]).astype(BF16)
    acc = jnp.zeros(x.shape, F32)
    for c in range(D_FF // FF_CHUNK):
        cols = slice(c * FF_CHUNK, (c + 1) * FF_CHUNK)
        hid = jnp.dot(h, w1_ref[:, cols], preferred_element_type=F32)
        hid = jnp.square(jnp.maximum(hid, 0.0)).astype(BF16)
        acc = acc + jnp.dot(hid, w2_ref[cols, :], preferred_element_type=F32)
    z = ALPHA * x + (1.0 + gt_ref[...]) * acc
    o_ref[...] = _layer_norm(z, g_ref[...], b_ref[...])


def _mlp(x, mods, layer, w1, w2, ln_g, ln_b):
    n = x.shape[0]
    tm, batch_of = _row_tiling(mods.mode)
    row = pl.BlockSpec((tm, D_MODEL), lambda i: (i, 0))
    return pl.pallas_call(
        _mlp_kernel,
        grid=(n // tm,),
        in_specs=[row, mods.spec(layer, 4, batch_of), mods.spec(layer, 3, batch_of),
                  mods.spec(layer, 5, batch_of), _const_spec(w1.shape), _const_spec(w2.shape),
                  _const_spec((1, D_MODEL)), _const_spec((1, D_MODEL))],
        out_specs=row,
        out_shape=jax.ShapeDtypeStruct((n, D_MODEL), F32),
        compiler_params=_params(("arbitrary",)),
        name="mlp_ln",
    )(x, mods.arr, mods.arr, mods.arr, w1, w2, ln_g.reshape(1, D_MODEL), ln_b.reshape(1, D_MODEL))


def _cmul(ar, ai, br, bi):
    return ar * br - ai * bi, ar * bi + ai * br


def _ssm_prep(lam_re, lam_im, log_dt, b_re, b_im, c_re, c_im):
    dt = jnp.exp(log_dt)[:, None]
    mag = jnp.exp(lam_re * dt)
    ar = mag * jnp.cos(lam_im * dt)
    ai = mag * jnp.sin(lam_im * dt)
    den = lam_re * lam_re + lam_im * lam_im
    er = ((ar - 1.0) * lam_re + ai * lam_im) / den
    ei = (ai * lam_re - (ar - 1.0) * lam_im) / den
    bbr = er[..., None] * b_re - ei[..., None] * b_im
    bbi = er[..., None] * b_im + ei[..., None] * b_re
    gpc = CHUNK // SSM_GROUP
    eye = jnp.eye(gpc, dtype=BF16)

    def blocks(m):
        x, y = m.shape[1:]
        m = m.astype(BF16).reshape(N_CHUNK, gpc, x, 1, y) * eye[None, :, None, :, None]
        return m.reshape(N_CHUNK, gpc * x, gpc * y)

    swap = lambda m: jnp.swapaxes(m, 1, 2)
    bcat = jnp.concatenate([blocks(swap(bbr)), blocks(swap(bbi))], axis=-1).astype(BF16)
    ccat = jnp.concatenate([blocks(swap(c_re)), -blocks(swap(c_im))], axis=1).astype(BF16)

    pw = [(jnp.ones_like(ar), jnp.zeros_like(ai))]
    for _ in range(TILE):
        pw.append(_cmul(*pw[-1], ar, ai))
    col = lambda p: (p[0][..., None], p[1][..., None])
    row = lambda p: (p[0][:, None, :], p[1][:, None, :])
    dot_p = functools.partial(jnp.einsum, "ghp,gpi->gih", precision=lax.Precision.HIGHEST)

    taps = []
    for j in range(TILE):
        abr, abi = _cmul(*col(pw[j]), bbr, bbi)
        taps.append(blocks(dot_p(c_re, abr) - dot_p(c_im, abi)))
    kcat = jnp.concatenate(taps, axis=1).astype(BF16)

    ends = []
    for r in range(TILE):
        wr, wi = _cmul(*col(pw[TILE - 1 - r]), bbr, bbi)
        ends.append(jnp.concatenate([blocks(swap(wr)), blocks(swap(wi))], axis=-1))
    wst = jnp.concatenate(ends, axis=1).astype(BF16)

    outs = []
    for s in range(TILE):
        mr, mi = _cmul(*row(pw[s + 1]), c_re, c_im)
        outs.append(jnp.concatenate([blocks(swap(mr)), -blocks(swap(mi))], axis=1))
    call = jnp.concatenate(outs, axis=-1).astype(BF16)

    flat = lambda p: (p[0].reshape(1, N_STATE), p[1].reshape(1, N_STATE))
    rows = lax.broadcasted_iota(jnp.int32, (SUBLANES, N_STATE), 0)

    def masked(a, first):
        return [jnp.where(rows >= first, jnp.broadcast_to(p, rows.shape), 0.0) for p in a]

    b = [flat(pw[TILE])]
    for _ in range(SUBLANES - 1):
        b.append(_cmul(*b[-1], *b[0]))
    powers = [jnp.concatenate(p, axis=0) for p in zip(*b)]
    tab = jnp.stack(masked(b[0], 1) + masked(b[1], 2) + masked(b[3], 4) + powers
                    + masked(b[SUBLANES - 1], 0))
    atab = jnp.stack(masked(flat(pw[1]), 0))
    return dict(bcat=bcat, ccat=ccat, kcat=kcat, wst=wst, call=call, tab=tab, atab=atab)


def _ssm_prompt_kernel(x_ref, sc_ref, sh_ref, kcat_ref, wst_ref, call_ref, d_ref, tab_ref,
                       g_ref, hre_ref, him_ref, h_ref, buf_ref, y_ref):
    n_tiles = SEQ // TILE
    half = CHUNK_STATE
    h_ref[...] = x_ref[...] * (1.0 + sc_ref[...]) + sh_ref[...]

    steps = [h_ref[pl.ds(r, n_tiles, stride=TILE), :].astype(BF16) for r in range(TILE)]
    buf_ref[...] = jnp.dot(jnp.concatenate(steps, axis=1), wst_ref[...],
                           preferred_element_type=F32)

    row = lax.broadcasted_iota(jnp.int32, (SUBLANES, half), 0)

    def tile_scan(i, carry):
        cr, ci = carry
        r0 = pl.multiple_of(i * SUBLANES, SUBLANES)
        vr = buf_ref[pl.ds(r0, SUBLANES), 0:half]
        vi = buf_ref[pl.ds(r0, SUBLANES), half:2 * half]
        for lvl, shift in enumerate((1, 2, 4)):
            pr, pi = _cmul(tab_ref[2 * lvl], tab_ref[2 * lvl + 1],
                           pltpu.roll(vr, shift, 0), pltpu.roll(vi, shift, 0))
            vr, vi = vr + pr, vi + pi
        pr, pi = _cmul(tab_ref[6], tab_ref[7], cr, ci)
        buf_ref[pl.ds(r0, SUBLANES), 0:half] = jnp.where(row == 0, cr, pltpu.roll(vr + pr, 1, 0))
        buf_ref[pl.ds(r0, SUBLANES), half:2 * half] = jnp.where(row == 0, ci,
                                                                 pltpu.roll(vi + pi, 1, 0))
        pr, pi = _cmul(tab_ref[8], tab_ref[9], cr, ci)
        last = SUBLANES - 1
        return (jnp.broadcast_to(vr[last:, :], vr.shape) + pr,
                jnp.broadcast_to(vi[last:, :], vi.shape) + pi)

    zero = jnp.zeros((SUBLANES, half), F32)
    cr, ci = lax.fori_loop(0, n_tiles // SUBLANES, tile_scan, (zero, zero))
    hre_ref[...] = cr[0:1, :]
    him_ref[...] = ci[0:1, :]

    y_state = jnp.dot(buf_ref[...].astype(BF16), call_ref[...], preferred_element_type=F32)
    for s in range(TILE):
        y_ref[pl.ds(s, n_tiles, stride=TILE), :] = y_state[:, s * CHUNK:(s + 1) * CHUNK]

    step_in_tile = lax.broadcasted_iota(jnp.int32, (T_SSM, CHUNK), 0) % TILE
    for blk in range(SEQ // T_SSM):
        rows = slice(blk * T_SSM, (blk + 1) * T_SSM)
        h = h_ref[rows, :]
        lags = [h.astype(BF16)] + [
            jnp.where(step_in_tile >= j, pltpu.roll(h, j, 0), 0.0).astype(BF16)
            for j in range(1, TILE)]
        y = jnp.dot(jnp.concatenate(lags, axis=1), kcat_ref[...], preferred_element_type=F32)
        y = y + y_ref[rows, :] + d_ref[...] * h
        g_ref[rows, :] = jax.nn.gelu(y).astype(g_ref.dtype)


def _ssm_prompt(x, mods, layer, p, d_skip):
    col = pl.BlockSpec((SEQ, CHUNK), lambda k, b: (b, k))
    fin = pl.BlockSpec((None, 1, CHUNK_STATE), lambda k, b: (b, 0, k))
    mod = lambda j: pl.BlockSpec((None, None, 1, CHUNK),
                                 lambda k, b: (layer, DEC_BATCH + b, 0, j * N_CHUNK + k))
    weight = lambda a: pl.BlockSpec((None,) + a.shape[1:], lambda k, b: (k, 0, 0))
    g, hre, him = pl.pallas_call(
        _ssm_prompt_kernel,
        grid=(N_CHUNK, BATCH),
        in_specs=[col, mod(1), mod(0), weight(p["kcat"]), weight(p["wst"]), weight(p["call"]),
                  pl.BlockSpec((1, CHUNK), lambda k, b: (0, k)),
                  pl.BlockSpec((p["tab"].shape[0], SUBLANES, CHUNK_STATE), lambda k, b: (0, 0, k))],
        out_specs=[col, fin, fin],
        out_shape=[jax.ShapeDtypeStruct((BATCH * SEQ, D_MODEL), BF16),
                   jax.ShapeDtypeStruct((BATCH, 1, N_STATE), F32),
                   jax.ShapeDtypeStruct((BATCH, 1, N_STATE), F32)],
        scratch_shapes=[pltpu.VMEM((SEQ, CHUNK), F32),
                        pltpu.VMEM((SEQ // TILE, 2 * CHUNK_STATE), F32),
                        pltpu.VMEM((SEQ, CHUNK), F32)],
        compiler_params=_params(("arbitrary", "arbitrary")),
        name="ssm_prompt",
    )(x, mods.arr, mods.arr, p["kcat"], p["wst"], p["call"], d_skip.reshape(1, D_MODEL), p["tab"])
    shape = (BATCH, SSM_GROUPS, SSM_STATE)
    return g, hre.reshape(shape), him.reshape(shape)


def _ssm_sample_kernel(x_ref, sc_ref, sh_ref, bcat_ref, ccat_ref, d_ref, a_ref, h0r_ref, h0i_ref,
                       g_ref, hre_ref, him_ref):
    tile = lambda m: jnp.concatenate([m] * DEC_SEQ, axis=0)
    h = x_ref[...] * (1.0 + tile(sc_ref[...])) + tile(sh_ref[...])
    bu = jnp.dot(h.astype(BF16), bcat_ref[...], preferred_element_type=F32)
    ar = a_ref[0, 0:1, :]
    ai = a_ref[1, 0:1, :]
    sr = h0r_ref[...]
    si = h0i_ref[...]
    states = []
    for t in range(DEC_SEQ):
        rows = slice(t * DEC_BATCH, (t + 1) * DEC_BATCH)
        pr, pi = _cmul(ar, ai, sr, si)
        sr = pr + bu[rows, :CHUNK_STATE]
        si = pi + bu[rows, CHUNK_STATE:]
        states.append(jnp.concatenate([sr, si], axis=1))
    xs = jnp.concatenate(states, axis=0).astype(BF16)
    y = jnp.dot(xs, ccat_ref[...], preferred_element_type=F32) + d_ref[...] * h
    g_ref[...] = jax.nn.gelu(y).astype(g_ref.dtype)
    hre_ref[...] = sr
    him_ref[...] = si


def _ssm_sample(x, mods, layer, bcat, ccat, d_skip, atab, h0_re, h0_im):
    n = DEC_BATCH * DEC_SEQ
    col = pl.BlockSpec((n, CHUNK), lambda k: (0, k))
    st = pl.BlockSpec((DEC_BATCH, CHUNK_STATE), lambda k: (0, k))
    mod = lambda j: pl.BlockSpec((None, DEC_BATCH, CHUNK), lambda k: (layer, 0, j * N_CHUNK + k))
    g, hre, him = pl.pallas_call(
        _ssm_sample_kernel,
        grid=(N_CHUNK,),
        in_specs=[col, mod(1), mod(0),
                  pl.BlockSpec((None, CHUNK, 2 * CHUNK_STATE), lambda k: (k, 0, 0)),
                  pl.BlockSpec((None, 2 * CHUNK_STATE, CHUNK), lambda k: (k, 0, 0)),
                  pl.BlockSpec((1, CHUNK), lambda k: (0, k)),
                  pl.BlockSpec((2, SUBLANES, CHUNK_STATE), lambda k: (0, 0, k)),
                  st, st],
        out_specs=[col, st, st],
        out_shape=[jax.ShapeDtypeStruct((n, D_MODEL), BF16),
                   jax.ShapeDtypeStruct((DEC_BATCH, N_STATE), F32),
                   jax.ShapeDtypeStruct((DEC_BATCH, N_STATE), F32)],
        compiler_params=_params(("arbitrary",)),
        name="ssm_sample",
    )(x, mods.arr, mods.arr, bcat, ccat, d_skip.reshape(1, D_MODEL), atab,
      h0_re.reshape(DEC_BATCH, N_STATE), h0_im.reshape(DEC_BATCH, N_STATE))
    shape = (DEC_BATCH, SSM_GROUPS, SSM_STATE)
    return g, hre.reshape(shape), him.reshape(shape)


def _rel_bucket(rel):
    n = jnp.maximum(rel, 0)
    max_exact = NUM_BUCKETS // 2
    large = max_exact + (jnp.log(jnp.maximum(n, 1).astype(F32) / max_exact)
                         / math.log(MAX_DISTANCE / max_exact) * (NUM_BUCKETS - max_exact)).astype(jnp.int32)
    large = jnp.minimum(large, NUM_BUCKETS - 1)
    return jnp.where(n < max_exact, n, large)


def _attn_scalars(attn_lam, subln_g, layer):
    lam_init = 0.8 - 0.6 * math.exp(-0.3 * layer)
    lam = (jnp.exp(jnp.sum(attn_lam[0] * attn_lam[1])) - jnp.exp(jnp.sum(attn_lam[2] * attn_lam[3]))
           + lam_init)
    lam_tile = jnp.full((SUBLANES, LANES), lam, F32)
    gain = (subln_g * (1.0 - lam_init)).reshape(1, V_DIM)
    return lam_tile, gain


def _bias_by_distance(rel_bias, n):
    bv = rel_bias[_rel_bucket(jnp.arange(n, dtype=jnp.int32))]
    return ((bv - rel_bias[NUM_BUCKETS - 1]) * LOG2E).T


def _toeplitz(w, n):
    u = jnp.concatenate([w[::-1], w[:1]])
    r = jnp.tile(u, n)[:n * (2 * n - 1)].reshape(n, 2 * n - 1)
    return r[:, n - 1:]


def _softmax_update(s, m_prev):
    m_new = jnp.maximum(m_prev, jnp.max(s, axis=1, keepdims=True))
    p = jnp.exp2(s - jnp.concatenate([m_new] * (s.shape[1] // LANES), axis=1))
    return m_new, p.astype(BF16), jnp.exp2(m_prev - m_new)


def _with_ones(v):
    return jnp.concatenate([v, jnp.ones((v.shape[0], V_DIM), BF16)], axis=1)


def _acc_update(acc, alpha, p, vx):
    return (jnp.concatenate([alpha, alpha], axis=1) * acc
            + jnp.dot(p, vx, preferred_element_type=F32))


_NT = (((1,), (1,)), ((), ()))


def _attn_prompt_kernel(q_ref, k_ref, v_ref, tab_ref, lam_ref, gain_ref, o_ref,
                        q2_ref, m_ref, acc_ref):
    i = pl.program_id(2)
    for hh in range(HEADS_PER_STEP):
        q = q_ref[:, hh * V_DIM:(hh + 1) * V_DIM]
        lane = lax.broadcasted_iota(jnp.int32, q.shape, 1)
        zero = jnp.zeros_like(q)
        q2_ref[hh, 0:TQ, :] = jnp.where(lane < HEAD_DIM, q, zero)
        q2_ref[hh, TQ:2 * TQ, :] = jnp.where(lane >= HEAD_DIM, q, zero)
    m_ref[...] = jnp.full(m_ref.shape, NEG_INF, F32)
    acc_ref[...] = jnp.zeros_like(acc_ref)

    def step(j, table):
        rows = pl.ds(pl.multiple_of(j * TQ, TQ), TQ)
        for hh in range(HEADS_PER_STEP):
            cols = slice(hh * V_DIM, (hh + 1) * V_DIM)
            s = lax.dot_general(q2_ref[hh], k_ref[rows, cols].astype(BF16), _NT,
                                preferred_element_type=F32)
            if table is not None:
                bias = tab_ref[hh, table]
                s = s + jnp.concatenate([bias, bias], axis=0)
            m_new, p, alpha = _softmax_update(s, m_ref[hh])
            acc_ref[hh] = _acc_update(acc_ref[hh], alpha, p,
                                      _with_ones(v_ref[rows, cols].astype(BF16)))
            m_ref[hh] = m_new

    def far(j, c):
        step(j, None)
        return c

    lax.fori_loop(0, jnp.maximum(i - 1, 0), far, 0)

    @pl.when(i >= 1)
    def _():
        step(i - 1, 1)

    step(i, 0)

    for hh in range(HEADS_PER_STEP):
        acc = acc_ref[hh]
        o = acc[:, :V_DIM] / acc[:, V_DIM:]
        o = o[0:TQ, :] - lam_ref[0:1, 0:1] * o[TQ:2 * TQ, :]
        o = o * lax.rsqrt(jnp.mean(o * o, axis=-1, keepdims=True) + LN_EPS) * gain_ref[...]
        o_ref[:, hh * V_DIM:(hh + 1) * V_DIM] = o.astype(o_ref.dtype)


def _attn_prompt_tables(rel_bias):
    bv = _bias_by_distance(rel_bias, 2 * TQ)
    masked = jnp.concatenate([jnp.full((N_HEADS, TQ - 1), NEG_INF, F32), bv[:, :TQ]], axis=1)
    toeplitz = jax.vmap(lambda w: _toeplitz(w, TQ))
    return jnp.stack([toeplitz(masked), toeplitz(bv[:, 1:])], axis=1)


def _attn_prompt(q, k, v, tab, lam_tile, gain):
    n_q = SEQ // TQ
    width = HEADS_PER_STEP * V_DIM
    shape3 = (BATCH, SEQ, D_MODEL)
    qspec = pl.BlockSpec((None, TQ, width), lambda b, h, i: (b, i, h))
    kvspec = pl.BlockSpec((None, SEQ, width), lambda b, h, i: (b, 0, h))
    o = pl.pallas_call(
        _attn_prompt_kernel,
        grid=(BATCH, N_HEADS // HEADS_PER_STEP, n_q),
        in_specs=[qspec, kvspec, kvspec,
                  pl.BlockSpec((HEADS_PER_STEP, 2, TQ, TQ), lambda b, h, i: (h, 0, 0, 0)),
                  pl.BlockSpec((SUBLANES, LANES), lambda b, h, i: (0, 0)),
                  pl.BlockSpec((1, V_DIM), lambda b, h, i: (0, 0))],
        out_specs=qspec,
        out_shape=jax.ShapeDtypeStruct(shape3, BF16),
        scratch_shapes=[pltpu.VMEM((HEADS_PER_STEP, 2 * TQ, V_DIM), BF16),
                        pltpu.VMEM((HEADS_PER_STEP, 2 * TQ, LANES), F32),
                        pltpu.VMEM((HEADS_PER_STEP, 2 * TQ, 2 * V_DIM), F32)],
        compiler_params=_params(("arbitrary", "arbitrary", "arbitrary")),
        name="attn_prompt",
    )(q.reshape(shape3), k.reshape(shape3), v.reshape(shape3), tab, lam_tile, gain)
    return o.reshape(BATCH * SEQ, D_MODEL)


N_SROWS = N_HEADS * 2 * DEC_SEQ


def _attn_sample_kernel(pt_ref, q_ref, qmask_ref, knew_ref, vnew_ref, bias_ref, bnew_ref,
                        lam_ref, gain_ref, *rest):
    del pt_ref
    k_refs = rest[:PAGES_PER_STEP]
    v_refs = rest[PAGES_PER_STEP:2 * PAGES_PER_STEP]
    o_ref, qx_ref, m_ref, acc_ref = rest[2 * PAGES_PER_STEP:]
    step = pl.program_id(1)
    rows_per_head = 2 * DEC_SEQ

    @pl.when(step == 0)
    def _():
        q = q_ref[...].astype(F32)
        q = jnp.concatenate([q, q], axis=0)
        qx_ref[...] = (jnp.concatenate([q] * N_HEADS, axis=0) * qmask_ref[...]).astype(BF16)
        m_ref[...] = jnp.full(m_ref.shape, NEG_INF, F32)
        acc_ref[...] = jnp.zeros_like(acc_ref)

    def update(s, values_of_head):
        m_new, p, alpha = _softmax_update(s, m_ref[...])
        for h in range(N_HEADS):
            r = slice(h * rows_per_head, (h + 1) * rows_per_head)
            acc_ref[r, :] = _acc_update(acc_ref[r, :], alpha[r, :], p[r, :],
                                        _with_ones(values_of_head(h)))
        m_ref[...] = m_new

    qx = qx_ref[...]
    kt = jnp.concatenate([r[...].astype(BF16) for r in k_refs], axis=1)
    s = jnp.dot(qx, kt, preferred_element_type=F32) + bias_ref[...]
    update(s, lambda h: jnp.concatenate(
        [r[pl.ds(h, PAGE_SIZE, stride=N_HEADS), :].astype(BF16) for r in v_refs], axis=0))

    @pl.when(step == pl.num_programs(1) - 1)
    def _():
        pad = jnp.zeros((LANES - SUBLANES, D_MODEL), BF16)
        kn = jnp.concatenate([knew_ref[...], pad], axis=0)
        vn = jnp.concatenate([vnew_ref[...], pad], axis=0)
        sn = lax.dot_general(qx, kn, _NT, preferred_element_type=F32) + bnew_ref[...]
        update(sn, lambda h: vn[:, h * V_DIM:(h + 1) * V_DIM])
        acc = acc_ref[...]
        o = acc[:, :V_DIM] / acc[:, V_DIM:]
        heads = []
        for h in range(N_HEADS):
            r0 = h * rows_per_head
            oh = o[r0:r0 + DEC_SEQ, :] - lam_ref[0:1, 0:1] * o[r0 + DEC_SEQ:r0 + rows_per_head, :]
            oh = oh * lax.rsqrt(jnp.mean(oh * oh, axis=-1, keepdims=True) + LN_EPS) * gain_ref[...]
            heads.append(oh)
        o_ref[...] = jnp.concatenate(heads, axis=1)


def _attn_sample_tables(rel_bias):
    bv = _bias_by_distance(rel_bias, PAST_LEN + DEC_SEQ)
    past = jnp.stack([bv[:, t + 1:t + 1 + PAST_LEN][:, ::-1] for t in range(DEC_SEQ)], axis=1)
    rel = (jnp.arange(DEC_SEQ, dtype=jnp.int32)[:, None]
           - jnp.arange(LANES, dtype=jnp.int32)[None, :])
    new = jnp.where(rel >= 0, bv[:, jnp.clip(rel, 0, DEC_SEQ - 1)], NEG_INF)
    rows = lambda a: jnp.broadcast_to(a[:, None], (N_HEADS, 2) + a.shape[1:]).reshape(
        N_SROWS, a.shape[-1])
    col = jnp.arange(D_MODEL, dtype=jnp.int32)[None, :]
    row = jnp.arange(N_SROWS, dtype=jnp.int32)[:, None]
    qmask = ((col // V_DIM == row // (2 * DEC_SEQ))
             & ((col % V_DIM) // HEAD_DIM == (row // DEC_SEQ) % 2)).astype(F32)
    return rows(past), rows(new), qmask


def _attn_sample(q, k_new, v_new, cache_k, cache_v, page_table, tables, lam_tile, gain):
    past, new, qmask = tables
    n_phys = cache_k.shape[0]
    ck = jnp.transpose(cache_k, (0, 2, 3, 4, 1)).reshape(n_phys, D_MODEL, PAGE_SIZE)
    cv = cache_v.reshape(n_phys, PAGE_SIZE * N_HEADS, V_DIM)
    n_steps = N_PAGES // PAGES_PER_STEP
    keys_per_step = PAGES_PER_STEP * PAGE_SIZE

    def page_spec(r):
        return pl.BlockSpec((None, D_MODEL, PAGE_SIZE),
                            lambda b, s, pt: (pt[b * N_PAGES + s * PAGES_PER_STEP + r], 0, 0))

    per_seq = lambda rows: pl.BlockSpec((None, rows, D_MODEL), lambda b, s, pt: (b, 0, 0))
    const = lambda shape: pl.BlockSpec(shape, lambda b, s, pt: (0,) * len(shape))
    grid_spec = pltpu.PrefetchScalarGridSpec(
        num_scalar_prefetch=1,
        grid=(DEC_BATCH, n_steps),
        in_specs=[per_seq(DEC_SEQ), const(qmask.shape), per_seq(SUBLANES), per_seq(SUBLANES),
                  pl.BlockSpec((N_SROWS, keys_per_step), lambda b, s, pt: (0, s)),
                  const(new.shape), const(lam_tile.shape), const(gain.shape)]
                 + [page_spec(r) for r in range(PAGES_PER_STEP)] * 2,
        out_specs=per_seq(DEC_SEQ),
        scratch_shapes=[pltpu.VMEM((N_SROWS, D_MODEL), BF16), pltpu.VMEM((N_SROWS, LANES), F32),
                        pltpu.VMEM((N_SROWS, 2 * V_DIM), F32)],
    )
    return pl.pallas_call(
        _attn_sample_kernel,
        grid_spec=grid_spec,
        out_shape=jax.ShapeDtypeStruct((DEC_BATCH, DEC_SEQ, D_MODEL), F32),
        compiler_params=_params(("arbitrary", "arbitrary")),
        name="attn_sample",
    )(page_table.reshape(-1), q, qmask, k_new, v_new, past, new, lam_tile, gain,
      *([ck] * PAGES_PER_STEP), *([cv] * PAGES_PER_STEP))


def _to_batch_major(a):
    return jnp.transpose(a.reshape(DEC_SEQ, DEC_BATCH, D_MODEL), (1, 0, 2))


def _to_time_major(a):
    return jnp.transpose(a, (1, 0, 2)).reshape(DEC_SEQ * DEC_BATCH, D_MODEL)


def _trunk(x, mods, mods_kv, h0_re, h0_im, attend, w):
    ssm_re, ssm_im = [], []
    k = v = None
    for layer in range(DEPTH):
        if layer < N_A_LAYERS:
            p = w["ssm"][layer]
            if mods.mode == "p":
                g, fr, fi = _ssm_prompt(x, mods, layer, p, w["ssm_d"][layer])
            else:
                g, fr, fi = _ssm_sample(x, mods, layer, p["bcat"], p["ccat"], w["ssm_d"][layer],
                                        p["atab"], h0_re[layer], h0_im[layer])
            ssm_re.append(fr)
            ssm_im.append(fi)
            x = _linear_ln(g, x, mods, layer, 2, w["glu"][layer],
                           w["ln_g"][layer, 0], w["ln_b"][layer, 0], glu=True)
        else:
            j = layer - N_A_LAYERS
            (q,) = _mod_linear(x, mods, layer, 1, 0, w["wq"][j], [BF16],
                               scale=HEAD_DIM ** -0.5 * LOG2E)
            o = attend(q, k, v, j)
            x = _linear_ln(o, x, mods, layer, 2, w["wo"][j],
                           w["ln_g"][layer, 0], w["ln_b"][layer, 0], glu=False)
        x = _mlp(x, mods, layer, w["w1"][layer], w["w2"][layer],
                 w["ln_g"][layer, 1], w["ln_b"][layer, 1])
        if layer == N_A_LAYERS - 1:
            k, v = _mod_linear(x, mods_kv, 0, 1, 0, w["wkv"], [F32, F32])
    return x, jnp.stack(ssm_re), jnp.stack(ssm_im), k, v


def kernel(x_prompt, x_sample, state_ssm_re, state_ssm_im, cache_k, cache_v, page_table, c_prompt, c_sample, rel_bias, w_ada, b_ada, ln_g, ln_b, ssm_lam_re, ssm_lam_im, ssm_log_dt, ssm_b_re, ssm_b_im, ssm_c_re, ssm_c_im, ssm_d, ssm_w_glu_a, ssm_w_glu_b, w_ada_kv, b_ada_kv, w_kv, attn_w_q, attn_lam, attn_subln_g, attn_w_o, mlp_w1, mlp_w2):
    w = {
        "ssm": [_ssm_prep(ssm_lam_re[l], ssm_lam_im[l], ssm_log_dt[l], ssm_b_re[l], ssm_b_im[l],
                          ssm_c_re[l], ssm_c_im[l]) for l in range(N_A_LAYERS)],
        "ssm_d": ssm_d,
        "glu": jnp.concatenate([ssm_w_glu_a, ssm_w_glu_b], axis=-1).astype(BF16),
        "wq": attn_w_q.astype(BF16),
        "wo": attn_w_o.astype(BF16),
        "wkv": w_kv.astype(BF16),
        "w1": mlp_w1.astype(BF16),
        "w2": mlp_w2.astype(BF16),
        "ln_g": ln_g,
        "ln_b": ln_b,
    }
    c_all = jnp.concatenate([c_sample, c_prompt, jnp.zeros((BATCH, D_MODEL), F32)], axis=0)
    ada = _ada(c_all, w_ada, b_ada)
    ada_kv = _ada(c_all, w_ada_kv[None], b_ada_kv[None])
    scalars = [_attn_scalars(attn_lam[j], attn_subln_g[j], N_A_LAYERS + j)
               for j in range(DEPTH - N_A_LAYERS)]

    prompt_tab = _attn_prompt_tables(rel_bias)

    def attend_prompt(q, k, v, j):
        return _attn_prompt(q, k, v, prompt_tab, *scalars[j])

    y_p, re_p, im_p, k_p, v_p = _trunk(x_prompt.reshape(BATCH * SEQ, D_MODEL), _Mods(ada, "p"),
                                       _Mods(ada_kv, "p"), None, None, attend_prompt, w)

    sample_tab = _attn_sample_tables(rel_bias)

    def attend_sample(q, k, v, j):
        pad = lambda a: jnp.pad(_to_batch_major(a).astype(BF16),
                                ((0, 0), (0, SUBLANES - DEC_SEQ), (0, 0)))
        o = _attn_sample(_to_batch_major(q), pad(k), pad(v), cache_k, cache_v, page_table,
                         sample_tab, *scalars[j])
        return _to_time_major(o).astype(BF16)

    y_s, re_s, im_s, k_s, v_s = _trunk(_to_time_major(x_sample), _Mods(ada, "s"), _Mods(ada_kv, "s"),
                                       state_ssm_re, state_ssm_im, attend_sample, w)

    return (y_p.reshape(BATCH, SEQ, D_MODEL), _to_batch_major(y_s), re_p, im_p,
            k_p.reshape(BATCH, SEQ, N_HEADS, 2, HEAD_DIM), v_p.reshape(BATCH, SEQ, N_HEADS, V_DIM),
            re_s, im_s,
            _to_batch_major(k_s).reshape(DEC_BATCH, DEC_SEQ, N_HEADS, 2, HEAD_DIM),
            _to_batch_major(v_s).reshape(DEC_BATCH, DEC_SEQ, N_HEADS, V_DIM))
```

```python
import functools
import math

import jax
import jax.numpy as jnp
from jax import lax
from jax.experimental import pallas as pl
from jax.experimental.pallas import tpu as pltpu

F32 = jnp.float32
BF16 = jnp.bfloat16

D_MODEL = 1024
BATCH = 4
SEQ = 4096
DEPTH = 4
DEC_BATCH = 128
DEC_SEQ = 4
PAST_LEN = 2048
PAGE_SIZE = 128
N_PAGES = PAST_LEN // PAGE_SIZE
N_A_LAYERS = DEPTH // 2
SSM_GROUP = 16
SSM_GROUPS = D_MODEL // SSM_GROUP
SSM_STATE = 64
N_STATE = SSM_GROUPS * SSM_STATE
N_HEADS = 8
HEAD_DIM = D_MODEL // (2 * N_HEADS)
V_DIM = 2 * HEAD_DIM
D_FF = 4 * D_MODEL
NUM_BUCKETS = 32
MAX_DISTANCE = 128
N_MOD = 6
ALPHA = (2.0 * DEPTH) ** 0.25
LN_EPS = 1e-5
NEG_INF = -1e30

SUBLANES = 8
LANES = 128
VMEM_LIMIT = 48 * 1024 * 1024

MOD_ROWS = DEC_BATCH + 2 * BATCH
CHUNK = LANES
N_CHUNK = D_MODEL // CHUNK
CHUNK_STATE = N_STATE // N_CHUNK
TM_PROMPT = 512
TILE = SUBLANES
T_SSM = 512
TQ = 512
HEADS_PER_STEP = 2
PAGES_PER_STEP = 8
LOG2E = math.log2(math.e)
FF_CHUNK = 1024


def _params(sem, vmem=VMEM_LIMIT):
    return pltpu.CompilerParams(dimension_semantics=sem, vmem_limit_bytes=vmem)


def _layer_norm(z, g, b):
    mu = jnp.mean(z, axis=-1, keepdims=True)
    zc = z - mu
    var = jnp.mean(zc * zc, axis=-1, keepdims=True)
    return zc * lax.rsqrt(var + LN_EPS) * g + b


def _ada_kernel(c_ref, w_ref, b_ref, o_ref):
    c = c_ref[...]
    sc = (c * jax.nn.sigmoid(c)).astype(BF16)
    w = w_ref[...].astype(BF16)
    o_ref[...] = jnp.dot(sc, w, preferred_element_type=F32) + b_ref[...]


def _ada(c_all, w, b):
    n_l, _, width = w.shape
    tn = 1024
    return pl.pallas_call(
        _ada_kernel,
        grid=(n_l, width // tn),
        in_specs=[
            pl.BlockSpec((MOD_ROWS, D_MODEL), lambda l, j: (0, 0)),
            pl.BlockSpec((None, D_MODEL, tn), lambda l, j: (l, 0, j)),
            pl.BlockSpec((None, 1, tn), lambda l, j: (l, 0, j)),
        ],
        out_specs=pl.BlockSpec((None, MOD_ROWS, tn), lambda l, j: (l, 0, j)),
        out_shape=jax.ShapeDtypeStruct((n_l, MOD_ROWS, width), F32),
        compiler_params=_params(("arbitrary", "arbitrary")),
        name="ada_mod",
    )(c_all, w, b.reshape(n_l, 1, width))


class _Mods:
    def __init__(self, arr, mode):
        self.mode = mode
        self.arr = arr if mode == "s" else arr.reshape(arr.shape[0], MOD_ROWS, 1, arr.shape[-1])

    def spec(self, layer, j, batch_of):
        if self.mode == "s":
            return pl.BlockSpec((None, DEC_BATCH, D_MODEL), lambda *g: (layer, 0, j))
        return pl.BlockSpec((None, None, 1, D_MODEL),
                            lambda *g: (layer, DEC_BATCH + batch_of(*g), 0, j))


def _row_tiling(mode):
    if mode == "s":
        return DEC_BATCH, lambda i: 0
    tiles_per_batch = SEQ // TM_PROMPT
    return TM_PROMPT, lambda i: i // tiles_per_batch


def _const_spec(shape):
    zeros = (0,) * len(shape)
    return pl.BlockSpec(shape, lambda *g: zeros, pipeline_mode=pl.Buffered(1))


def _mod_linear_kernel(x_ref, sc_ref, sh_ref, w_ref, *o_refs, scale, blocks):
    h = (x_ref[...] * (1.0 + sc_ref[...]) + sh_ref[...]).astype(BF16)
    y = jnp.dot(h, w_ref[...], preferred_element_type=F32)
    if scale != 1.0:
        y = y * scale
    for (n, transposed), o_ref in zip(blocks, o_refs):
        block = y[:, n * D_MODEL:(n + 1) * D_MODEL].astype(o_ref.dtype)
        o_ref[...] = block.T if transposed else block


def _mod_linear(x, mods, layer, j_scale, j_shift, w, outs, scale=1.0):
    n = x.shape[0]
    tm, batch_of = _row_tiling(mods.mode)
    row = pl.BlockSpec((tm, D_MODEL), lambda i: (i, 0))
    tiles_per_batch = SEQ // tm
    col = pl.BlockSpec((None, D_MODEL, tm), lambda i: (i // tiles_per_batch, 0, i % tiles_per_batch))
    return pl.pallas_call(
        functools.partial(_mod_linear_kernel, scale=scale,
                          blocks=tuple((o[0], o[2]) for o in outs)),
        grid=(n // tm,),
        in_specs=[row, mods.spec(layer, j_scale, batch_of), mods.spec(layer, j_shift, batch_of),
                  _const_spec(w.shape)],
        out_specs=[col if o[2] else row for o in outs],
        out_shape=[jax.ShapeDtypeStruct((BATCH, D_MODEL, SEQ) if o[2] else (n, D_MODEL), o[1])
                   for o in outs],
        compiler_params=_params(("arbitrary",)),
        name="mod_linear",
    )(x, mods.arr, mods.arr, w)


def _linear_ln_kernel(a_ref, x_ref, gt_ref, w_ref, g_ref, b_ref, o_ref, *, glu):
    y = jnp.dot(a_ref[...], w_ref[...], preferred_element_type=F32)
    if glu:
        y = y[:, :D_MODEL] * jax.nn.sigmoid(y[:, D_MODEL:])
    z = ALPHA * x_ref[...] + (1.0 + gt_ref[...]) * y
    o_ref[...] = _layer_norm(z, g_ref[...], b_ref[...])


def _linear_ln(a, x, mods, layer, j_gate, w, ln_g, ln_b, glu):
    n = x.shape[0]
    tm, batch_of = _row_tiling(mods.mode)
    row = pl.BlockSpec((tm, D_MODEL), lambda i: (i, 0))
    return pl.pallas_call(
        functools.partial(_linear_ln_kernel, glu=glu),
        grid=(n // tm,),
        in_specs=[row, row, mods.spec(layer, j_gate, batch_of), _const_spec(w.shape),
                  _const_spec((1, D_MODEL)), _const_spec((1, D_MODEL))],
        out_specs=row,
        out_shape=jax.ShapeDtypeStruct((n, D_MODEL), F32),
        compiler_params=_params(("arbitrary",)),
        name="linear_ln",
    )(a, x, mods.arr, w, ln_g.reshape(1, D_MODEL), ln_b.reshape(1, D_MODEL))


def _mlp_kernel(x_ref, sc_ref, sh_ref, gt_ref, w1_ref, w2_ref, g_ref, b_ref, o_ref):
    x = x_ref[...]
    h = (x * (1.0 + sc_ref[...]) + sh_ref[...]).astype(BF16)
    acc = jnp.zeros(x.shape, F32)
    for c in range(D_FF // FF_CHUNK):
        cols = slice(c * FF_CHUNK, (c + 1) * FF_CHUNK)
        hid = jnp.dot(h, w1_ref[:, cols], preferred_element_type=F32)
        hid = jnp.square(jnp.maximum(hid, 0.0)).astype(BF16)
        acc = acc + jnp.dot(hid, w2_ref[cols, :], preferred_element_type=F32)
    z = ALPHA * x + (1.0 + gt_ref[...]) * acc
    o_ref[...] = _layer_norm(z, g_ref[...], b_ref[...])


def _mlp(x, mods, layer, w1, w2, ln_g, ln_b):
    n = x.shape[0]
    tm, batch_of = _row_tiling(mods.mode)
    row = pl.BlockSpec((tm, D_MODEL), lambda i: (i, 0))
    return pl.pallas_call(
        _mlp_kernel,
        grid=(n // tm,),
        in_specs=[row, mods.spec(layer, 4, batch_of), mods.spec(layer, 3, batch_of),
                  mods.spec(layer, 5, batch_of), _const_spec(w1.shape), _const_spec(w2.shape),
                  _const_spec((1, D_MODEL)), _const_spec((1, D_MODEL))],
        out_specs=row,
        out_shape=jax.ShapeDtypeStruct((n, D_MODEL), F32),
        compiler_params=_params(("arbitrary",)),
        name="mlp_ln",
    )(x, mods.arr, mods.arr, mods.arr, w1, w2, ln_g.reshape(1, D_MODEL), ln_b.reshape(1, D_MODEL))


def _cmul(ar, ai, br, bi):
    return ar * br - ai * bi, ar * bi + ai * br


def _ssm_prep(lam_re, lam_im, log_dt, b_re, b_im, c_re, c_im):
    n_l = lam_re.shape[0]
    dt = jnp.exp(log_dt)[..., None]
    ns = list(range(TILE + 1)) + [TILE * m for m in range(2, SUBLANES + 1)]
    n = jnp.asarray(ns, F32).reshape(-1, 1, 1, 1)
    mag = jnp.exp(n * (lam_re * dt))
    pr = mag * jnp.cos(n * (lam_im * dt))
    pi = mag * jnp.sin(n * (lam_im * dt))
    ar, ai = pr[1], pi[1]
    den = lam_re * lam_re + lam_im * lam_im
    er = ((ar - 1.0) * lam_re + ai * lam_im) / den
    ei = (ai * lam_re - (ar - 1.0) * lam_im) / den
    bbr = er[..., None] * b_re - ei[..., None] * b_im
    bbi = er[..., None] * b_im + ei[..., None] * b_re
    gpc = CHUNK // SSM_GROUP
    eye = jnp.eye(gpc, dtype=BF16)

    def blocks(m):
        lead, (x, y) = m.shape[:-3], m.shape[-2:]
        m = m.astype(BF16).reshape(lead + (N_CHUNK, gpc, x, 1, y)) * eye[:, None, :, None]
        return m.reshape(lead + (N_CHUNK, gpc * x, gpc * y))

    swap = lambda m: jnp.swapaxes(m, -1, -2)
    bcat = jnp.concatenate([blocks(swap(bbr)), blocks(swap(bbi))], axis=-1)
    ccat = jnp.concatenate([blocks(swap(c_re)), -blocks(swap(c_im))], axis=-2)

    def lag_rows(m):
        m = jnp.moveaxis(m, 0, 2)
        return m.reshape(m.shape[:2] + (-1, m.shape[-1]))

    def lag_cols(m):
        m = jnp.moveaxis(m, 0, 3)
        return m.reshape(m.shape[:3] + (-1,))

    abr, abi = _cmul(pr[:TILE, ..., None], pi[:TILE, ..., None], bbr, bbi)
    dot_p = functools.partial(jnp.einsum, "lghp,jlgpi->jlgih", precision=lax.Precision.HIGHEST)
    kcat = lag_rows(blocks(dot_p(c_re, abr) - dot_p(c_im, abi)))

    wr, wi = swap(abr[::-1]), swap(abi[::-1])
    wst = lag_rows(jnp.concatenate([blocks(wr), blocks(wi)], axis=-1))

    mr, mi = _cmul(pr[1:TILE + 1, :, :, None, :], pi[1:TILE + 1, :, :, None, :], c_re, c_im)
    call = lag_cols(jnp.concatenate([blocks(swap(mr)), -blocks(swap(mi))], axis=-2))

    flat = lambda a: jnp.moveaxis(a, 0, 1).reshape(n_l, -1, N_STATE)
    rows = lax.broadcasted_iota(jnp.int32, (SUBLANES, N_STATE), 0)
    parts = []
    for a in (flat(pr[TILE:]), flat(pi[TILE:])):
        masked = lambda m, first: jnp.where(rows >= first, a[:, m - 1:m, :], 0.0)
        parts.append([masked(1, 1), masked(2, 2), masked(4, 4), a, masked(SUBLANES, 0)])
    tab = jnp.stack([p for pair in zip(*parts) for p in pair], axis=1)
    atab = jnp.stack([jnp.broadcast_to(flat(a[1:2]), (n_l, SUBLANES, N_STATE)) for a in (pr, pi)],
                     axis=1)
    return dict(bcat=bcat, ccat=ccat, kcat=kcat, wst=wst, call=call, tab=tab, atab=atab)


def _ssm_prompt_kernel(x_ref, sc_ref, sh_ref, kcat_ref, wst_ref, call_ref, d_ref, tab_ref,
                       g_ref, hre_ref, him_ref, h_ref, buf_ref, y_ref):
    n_tiles = SEQ // TILE
    half = CHUNK_STATE
    h_ref[...] = x_ref[...] * (1.0 + sc_ref[...]) + sh_ref[...]

    steps = [h_ref[pl.ds(r, n_tiles, stride=TILE), :].astype(BF16) for r in range(TILE)]
    buf_ref[...] = jnp.dot(jnp.concatenate(steps, axis=1), wst_ref[...],
                           preferred_element_type=F32)

    row = lax.broadcasted_iota(jnp.int32, (SUBLANES, half), 0)

    def tile_scan(i, carry):
        cr, ci = carry
        r0 = pl.multiple_of(i * SUBLANES, SUBLANES)
        vr = buf_ref[pl.ds(r0, SUBLANES), 0:half]
        vi = buf_ref[pl.ds(r0, SUBLANES), half:2 * half]
        for lvl, shift in enumerate((1, 2, 4)):
            pr, pi = _cmul(tab_ref[2 * lvl], tab_ref[2 * lvl + 1],
                           pltpu.roll(vr, shift, 0), pltpu.roll(vi, shift, 0))
            vr, vi = vr + pr, vi + pi
        pr, pi = _cmul(tab_ref[6], tab_ref[7], cr, ci)
        buf_ref[pl.ds(r0, SUBLANES), 0:half] = jnp.where(row == 0, cr, pltpu.roll(vr + pr, 1, 0))
        buf_ref[pl.ds(r0, SUBLANES), half:2 * half] = jnp.where(row == 0, ci,
                                                                 pltpu.roll(vi + pi, 1, 0))
        pr, pi = _cmul(tab_ref[8], tab_ref[9], cr, ci)
        last = SUBLANES - 1
        return (jnp.broadcast_to(vr[last:, :], vr.shape) + pr,
                jnp.broadcast_to(vi[last:, :], vi.shape) + pi)

    zero = jnp.zeros((SUBLANES, half), F32)
    cr, ci = lax.fori_loop(0, n_tiles // SUBLANES, tile_scan, (zero, zero))
    hre_ref[...] = cr[0:1, :]
    him_ref[...] = ci[0:1, :]

    y_state = jnp.dot(buf_ref[...].astype(BF16), call_ref[...], preferred_element_type=F32)
    for s in range(TILE):
        y_ref[pl.ds(s, n_tiles, stride=TILE), :] = y_state[:, s * CHUNK:(s + 1) * CHUNK]

    step_in_tile = lax.broadcasted_iota(jnp.int32, (T_SSM, CHUNK), 0) % TILE
    for blk in range(SEQ // T_SSM):
        rows = slice(blk * T_SSM, (blk + 1) * T_SSM)
        h = h_ref[rows, :]
        lags = [h.astype(BF16)] + [
            jnp.where(step_in_tile >= j, pltpu.roll(h, j, 0), 0.0).astype(BF16)
            for j in range(1, TILE)]
        y = jnp.dot(jnp.concatenate(lags, axis=1), kcat_ref[...], preferred_element_type=F32)
        y = y + y_ref[rows, :] + d_ref[...] * h
        g_ref[rows, :] = jax.nn.gelu(y).astype(g_ref.dtype)


def _ssm_prompt(x, mods, layer, p, d_skip):
    col = pl.BlockSpec((SEQ, CHUNK), lambda k, b: (b, k))
    fin = pl.BlockSpec((None, 1, CHUNK_STATE), lambda k, b: (b, 0, k))
    mod = lambda j: pl.BlockSpec((None, None, 1, CHUNK),
                                 lambda k, b: (layer, DEC_BATCH + b, 0, j * N_CHUNK + k))
    weight = lambda a: pl.BlockSpec((None, None) + a.shape[2:], lambda k, b: (layer, k, 0, 0))
    g, hre, him = pl.pallas_call(
        _ssm_prompt_kernel,
        grid=(N_CHUNK, BATCH),
        in_specs=[col, mod(1), mod(0), weight(p["kcat"]), weight(p["wst"]), weight(p["call"]),
                  pl.BlockSpec((None, 1, CHUNK), lambda k, b: (layer, 0, k)),
                  pl.BlockSpec((None, p["tab"].shape[1], SUBLANES, CHUNK_STATE),
                               lambda k, b: (layer, 0, 0, k))],
        out_specs=[col, fin, fin],
        out_shape=[jax.ShapeDtypeStruct((BATCH * SEQ, D_MODEL), BF16),
                   jax.ShapeDtypeStruct((BATCH, 1, N_STATE), F32),
                   jax.ShapeDtypeStruct((BATCH, 1, N_STATE), F32)],
        scratch_shapes=[pltpu.VMEM((SEQ, CHUNK), F32),
                        pltpu.VMEM((SEQ // TILE, 2 * CHUNK_STATE), F32),
                        pltpu.VMEM((SEQ, CHUNK), F32)],
        compiler_params=_params(("arbitrary", "arbitrary")),
        name="ssm_prompt",
    )(x, mods.arr, mods.arr, p["kcat"], p["wst"], p["call"], d_skip[:, None, :], p["tab"])
    shape = (BATCH, SSM_GROUPS, SSM_STATE)
    return g, hre.reshape(shape), him.reshape(shape)


def _ssm_sample_kernel(x_ref, sc_ref, sh_ref, bcat_ref, ccat_ref, d_ref, a_ref, h0r_ref, h0i_ref,
                       g_ref, hre_ref, him_ref):
    tile = lambda m: jnp.concatenate([m] * DEC_SEQ, axis=0)
    h = x_ref[...] * (1.0 + tile(sc_ref[...])) + tile(sh_ref[...])
    bu = jnp.dot(h.astype(BF16), bcat_ref[...], preferred_element_type=F32)
    ar = a_ref[0, 0:1, :]
    ai = a_ref[1, 0:1, :]
    sr = h0r_ref[...]
    si = h0i_ref[...]
    states = []
    for t in range(DEC_SEQ):
        rows = slice(t * DEC_BATCH, (t + 1) * DEC_BATCH)
        pr, pi = _cmul(ar, ai, sr, si)
        sr = pr + bu[rows, :CHUNK_STATE]
        si = pi + bu[rows, CHUNK_STATE:]
        states.append(jnp.concatenate([sr, si], axis=1))
    xs = jnp.concatenate(states, axis=0).astype(BF16)
    y = jnp.dot(xs, ccat_ref[...], preferred_element_type=F32) + d_ref[...] * h
    g_ref[...] = jax.nn.gelu(y).astype(g_ref.dtype)
    hre_ref[...] = sr
    him_ref[...] = si


def _ssm_sample(x, mods, layer, p, d_skip, h0_re, h0_im):
    n = DEC_BATCH * DEC_SEQ
    col = pl.BlockSpec((n, CHUNK), lambda k: (0, k))
    st = pl.BlockSpec((None, DEC_BATCH, CHUNK_STATE), lambda k: (layer, 0, k))
    mod = lambda j: pl.BlockSpec((None, DEC_BATCH, CHUNK), lambda k: (layer, 0, j * N_CHUNK + k))
    g, hre, him = pl.pallas_call(
        _ssm_sample_kernel,
        grid=(N_CHUNK,),
        in_specs=[col, mod(1), mod(0),
                  pl.BlockSpec((None, None, CHUNK, 2 * CHUNK_STATE), lambda k: (layer, k, 0, 0)),
                  pl.BlockSpec((None, None, 2 * CHUNK_STATE, CHUNK), lambda k: (layer, k, 0, 0)),
                  pl.BlockSpec((None, 1, CHUNK), lambda k: (layer, 0, k)),
                  pl.BlockSpec((None, 2, SUBLANES, CHUNK_STATE), lambda k: (layer, 0, 0, k)),
                  st, st],
        out_specs=[col] + [pl.BlockSpec((DEC_BATCH, CHUNK_STATE), lambda k: (0, k))] * 2,
        out_shape=[jax.ShapeDtypeStruct((n, D_MODEL), BF16),
                   jax.ShapeDtypeStruct((DEC_BATCH, N_STATE), F32),
                   jax.ShapeDtypeStruct((DEC_BATCH, N_STATE), F32)],
        compiler_params=_params(("arbitrary",)),
        name="ssm_sample",
    )(x, mods.arr, mods.arr, p["bcat"], p["ccat"], d_skip[:, None, :], p["atab"],
      h0_re.reshape(-1, DEC_BATCH, N_STATE), h0_im.reshape(-1, DEC_BATCH, N_STATE))
    shape = (DEC_BATCH, SSM_GROUPS, SSM_STATE)
    return g, hre.reshape(shape), him.reshape(shape)


def _rel_bucket(rel):
    n = jnp.maximum(rel, 0)
    max_exact = NUM_BUCKETS // 2
    large = max_exact + (jnp.log(jnp.maximum(n, 1).astype(F32) / max_exact)
                         / math.log(MAX_DISTANCE / max_exact) * (NUM_BUCKETS - max_exact)).astype(jnp.int32)
    large = jnp.minimum(large, NUM_BUCKETS - 1)
    return jnp.where(n < max_exact, n, large)


def _attn_scalars(attn_lam, subln_g, layer):
    lam_init = 0.8 - 0.6 * math.exp(-0.3 * layer)
    lam = (jnp.exp(jnp.sum(attn_lam[0] * attn_lam[1])) - jnp.exp(jnp.sum(attn_lam[2] * attn_lam[3]))
           + lam_init)
    lam_tile = jnp.full((SUBLANES, LANES), lam, F32)
    gain = (subln_g * (1.0 - lam_init)).reshape(1, V_DIM)
    return lam_tile, gain


def _bias_by_distance(rel_bias, n):
    bv = rel_bias[_rel_bucket(jnp.arange(n, dtype=jnp.int32))]
    return ((bv - rel_bias[NUM_BUCKETS - 1]) * LOG2E).T


def _toeplitz(w, n):
    u = jnp.concatenate([w[::-1], w[:1]])
    r = jnp.tile(u, n)[:n * (2 * n - 1)].reshape(n, 2 * n - 1)
    return r[:, n - 1:]


def _softmax_update(s, m_prev):
    m_new = jnp.maximum(m_prev, jnp.max(s, axis=1, keepdims=True))
    p = jnp.exp2(s - jnp.concatenate([m_new] * (s.shape[1] // LANES), axis=1))
    return m_new, p.astype(BF16), jnp.exp2(m_prev - m_new)


def _with_ones(v):
    return jnp.concatenate([v, jnp.ones((v.shape[0], V_DIM), BF16)], axis=1)


def _acc_update(acc, alpha, p, vx):
    return (jnp.concatenate([alpha, alpha], axis=1) * acc
            + jnp.dot(p, vx, preferred_element_type=F32))


_NT = (((1,), (1,)), ((), ()))


def _attn_prompt_kernel(q_ref, k_ref, v_ref, tab_ref, lam_ref, gain_ref, o_ref,
                        q2_ref, m_ref, acc_ref, s_ref):
    i = pl.program_id(2)
    for hh in range(HEADS_PER_STEP):
        q = q_ref[:, hh * V_DIM:(hh + 1) * V_DIM]
        lane = lax.broadcasted_iota(jnp.int32, q.shape, 1)
        zero = jnp.zeros_like(q)
        q2_ref[hh, 0:TQ, :] = jnp.where(lane < HEAD_DIM, q, zero)
        q2_ref[hh, TQ:2 * TQ, :] = jnp.where(lane >= HEAD_DIM, q, zero)
    m_ref[...] = jnp.full(m_ref.shape, NEG_INF, F32)
    acc_ref[...] = jnp.zeros_like(acc_ref)

    def key_rows(j):
        return pl.ds(pl.multiple_of(j * TQ, TQ), TQ)

    def logits(hh, j):
        cols = slice(hh * V_DIM, (hh + 1) * V_DIM)
        return lax.dot_general(q2_ref[hh], k_ref[key_rows(j), cols].astype(BF16), _NT,
                               preferred_element_type=F32)

    def consume(hh, j, s, table):
        cols = slice(hh * V_DIM, (hh + 1) * V_DIM)
        if table is not None:
            bias = tab_ref[hh, table]
            s = s + jnp.concatenate([bias, bias], axis=0)
        m_new, p, alpha = _softmax_update(s, m_ref[hh])
        acc_ref[hh] = _acc_update(acc_ref[hh], alpha, p,
                                  _with_ones(v_ref[key_rows(j), cols].astype(BF16)))
        m_ref[hh] = m_new

    def step(j, table, more):
        consume(0, j, s_ref[...], table)
        for hh in range(1, HEADS_PER_STEP):
            consume(hh, j, logits(hh, j), table)
        if more:
            s_ref[...] = logits(0, j + 1)

    s_ref[...] = logits(0, 0)

    def far(j, c):
        step(j, None, True)
        return c

    lax.fori_loop(0, jnp.maximum(i - 1, 0), far, 0)

    @pl.when(i >= 1)
    def _():
        step(i - 1, 1, True)

    step(i, 0, False)

    for hh in range(HEADS_PER_STEP):
        acc = acc_ref[hh]
        o = acc[:, :V_DIM] / acc[:, V_DIM:]
        o = o[0:TQ, :] - lam_ref[0:1, 0:1] * o[TQ:2 * TQ, :]
        o = o * lax.rsqrt(jnp.mean(o * o, axis=-1, keepdims=True) + LN_EPS) * gain_ref[...]
        o_ref[:, hh * V_DIM:(hh + 1) * V_DIM] = o.astype(o_ref.dtype)


def _attn_prompt_tables(rel_bias):
    bv = _bias_by_distance(rel_bias, 2 * TQ)
    masked = jnp.concatenate([jnp.full((N_HEADS, TQ - 1), NEG_INF, F32), bv[:, :TQ]], axis=1)
    toeplitz = jax.vmap(lambda w: _toeplitz(w, TQ))
    return jnp.stack([toeplitz(masked), toeplitz(bv[:, 1:])], axis=1)


def _attn_prompt(q, k, v, tab, lam_tile, gain):
    n_q = SEQ // TQ
    width = HEADS_PER_STEP * V_DIM
    shape3 = (BATCH, SEQ, D_MODEL)
    qspec = pl.BlockSpec((None, TQ, width), lambda b, h, i: (b, i, h))
    kvspec = pl.BlockSpec((None, SEQ, width), lambda b, h, i: (b, 0, h))
    o = pl.pallas_call(
        _attn_prompt_kernel,
        grid=(BATCH, N_HEADS // HEADS_PER_STEP, n_q),
        in_specs=[qspec, kvspec, kvspec,
                  pl.BlockSpec((HEADS_PER_STEP, 2, TQ, TQ), lambda b, h, i: (h, 0, 0, 0)),
                  pl.BlockSpec((SUBLANES, LANES), lambda b, h, i: (0, 0)),
                  pl.BlockSpec((1, V_DIM), lambda b, h, i: (0, 0))],
        out_specs=qspec,
        out_shape=jax.ShapeDtypeStruct(shape3, BF16),
        scratch_shapes=[pltpu.VMEM((HEADS_PER_STEP, 2 * TQ, V_DIM), BF16),
                        pltpu.VMEM((HEADS_PER_STEP, 2 * TQ, LANES), F32),
                        pltpu.VMEM((HEADS_PER_STEP, 2 * TQ, 2 * V_DIM), F32),
                        pltpu.VMEM((2 * TQ, TQ), F32)],
        compiler_params=_params(("arbitrary", "arbitrary", "arbitrary")),
        name="attn_prompt",
    )(q.reshape(shape3), k.reshape(shape3), v.reshape(shape3), tab, lam_tile, gain)
    return o.reshape(BATCH * SEQ, D_MODEL)


N_SROWS = N_HEADS * 2 * DEC_SEQ


def _attn_sample_kernel(pt_ref, q_ref, qmask_ref, knew_ref, vnew_ref, bias_ref, bnew_ref,
                        lam_ref, gain_ref, *rest):
    del pt_ref
    k_refs = rest[:PAGES_PER_STEP]
    v_refs = rest[PAGES_PER_STEP:2 * PAGES_PER_STEP]
    o_ref, qx_ref, m_ref, acc_ref = rest[2 * PAGES_PER_STEP:]
    step = pl.program_id(1)
    rows_per_head = 2 * DEC_SEQ

    @pl.when(step == 0)
    def _():
        q = q_ref[...].astype(F32)
        q = jnp.concatenate([q, q], axis=0)
        qx_ref[...] = (jnp.concatenate([q] * N_HEADS, axis=0) * qmask_ref[...]).astype(BF16)
        m_ref[...] = jnp.full(m_ref.shape, NEG_INF, F32)
        acc_ref[...] = jnp.zeros_like(acc_ref)

    def update(s, values_of_head):
        m_new, p, alpha = _softmax_update(s, m_ref[...])
        for h in range(N_HEADS):
            r = slice(h * rows_per_head, (h + 1) * rows_per_head)
            acc_ref[r, :] = _acc_update(acc_ref[r, :], alpha[r, :], p[r, :],
                                        _with_ones(values_of_head(h)))
        m_ref[...] = m_new

    qx = qx_ref[...]
    kt = jnp.concatenate([r[...].astype(BF16) for r in k_refs], axis=1)
    s = jnp.dot(qx, kt, preferred_element_type=F32) + bias_ref[...]
    update(s, lambda h: jnp.concatenate(
        [r[pl.ds(h, PAGE_SIZE, stride=N_HEADS), :].astype(BF16) for r in v_refs], axis=0))

    @pl.when(step == pl.num_programs(1) - 1)
    def _():
        pad = jnp.zeros((LANES - SUBLANES, D_MODEL), BF16)
        kn = jnp.concatenate([knew_ref[...], pad], axis=0)
        vn = jnp.concatenate([vnew_ref[...], pad], axis=0)
        sn = lax.dot_general(qx, kn, _NT, preferred_element_type=F32) + bnew_ref[...]
        update(sn, lambda h: vn[:, h * V_DIM:(h + 1) * V_DIM])
        acc = acc_ref[...]
        o = acc[:, :V_DIM] / acc[:, V_DIM:]
        heads = []
        for h in range(N_HEADS):
            r0 = h * rows_per_head
            oh = o[r0:r0 + DEC_SEQ, :] - lam_ref[0:1, 0:1] * o[r0 + DEC_SEQ:r0 + rows_per_head, :]
            oh = oh * lax.rsqrt(jnp.mean(oh * oh, axis=-1, keepdims=True) + LN_EPS) * gain_ref[...]
            heads.append(oh)
        o_ref[...] = jnp.concatenate(heads, axis=1)


def _attn_sample_tables(rel_bias):
    bv = _bias_by_distance(rel_bias, PAST_LEN + DEC_SEQ)
    past = jnp.stack([bv[:, t + 1:t + 1 + PAST_LEN][:, ::-1] for t in range(DEC_SEQ)], axis=1)
    rel = (jnp.arange(DEC_SEQ, dtype=jnp.int32)[:, None]
           - jnp.arange(LANES, dtype=jnp.int32)[None, :])
    new = jnp.where(rel >= 0, bv[:, jnp.clip(rel, 0, DEC_SEQ - 1)], NEG_INF)
    rows = lambda a: jnp.broadcast_to(a[:, None], (N_HEADS, 2) + a.shape[1:]).reshape(
        N_SROWS, a.shape[-1])
    col = jnp.arange(D_MODEL, dtype=jnp.int32)[None, :]
    row = jnp.arange(N_SROWS, dtype=jnp.int32)[:, None]
    qmask = ((col // V_DIM == row // (2 * DEC_SEQ))
             & ((col % V_DIM) // HEAD_DIM == (row // DEC_SEQ) % 2)).astype(F32)
    return rows(past), rows(new), qmask


def _attn_sample(q, k_new, v_new, cache_k, cache_v, page_table, tables, lam_tile, gain):
    past, new, qmask = tables
    n_phys = cache_k.shape[0]
    ck = jnp.transpose(cache_k, (0, 2, 3, 4, 1)).reshape(n_phys, D_MODEL, PAGE_SIZE)
    cv = cache_v.reshape(n_phys, PAGE_SIZE * N_HEADS, V_DIM)
    n_steps = N_PAGES // PAGES_PER_STEP
    keys_per_step = PAGES_PER_STEP * PAGE_SIZE

    def page_spec(r):
        return pl.BlockSpec((None, D_MODEL, PAGE_SIZE),
                            lambda b, s, pt: (pt[b * N_PAGES + s * PAGES_PER_STEP + r], 0, 0))

    per_seq = lambda rows: pl.BlockSpec((None, rows, D_MODEL), lambda b, s, pt: (b, 0, 0))
    const = lambda shape: pl.BlockSpec(shape, lambda b, s, pt: (0,) * len(shape))
    grid_spec = pltpu.PrefetchScalarGridSpec(
        num_scalar_prefetch=1,
        grid=(DEC_BATCH, n_steps),
        in_specs=[per_seq(DEC_SEQ), const(qmask.shape), per_seq(SUBLANES), per_seq(SUBLANES),
                  pl.BlockSpec((N_SROWS, keys_per_step), lambda b, s, pt: (0, s)),
                  const(new.shape), const(lam_tile.shape), const(gain.shape)]
                 + [page_spec(r) for r in range(PAGES_PER_STEP)] * 2,
        out_specs=per_seq(DEC_SEQ),
        scratch_shapes=[pltpu.VMEM((N_SROWS, D_MODEL), BF16), pltpu.VMEM((N_SROWS, LANES), F32),
                        pltpu.VMEM((N_SROWS, 2 * V_DIM), F32)],
    )
    return pl.pallas_call(
        _attn_sample_kernel,
        grid_spec=grid_spec,
        out_shape=jax.ShapeDtypeStruct((DEC_BATCH, DEC_SEQ, D_MODEL), F32),
        compiler_params=_params(("arbitrary", "arbitrary")),
        name="attn_sample",
    )(page_table.reshape(-1), q, qmask, k_new, v_new, past, new, lam_tile, gain,
      *([ck] * PAGES_PER_STEP), *([cv] * PAGES_PER_STEP))


def _to_batch_major(a):
    return jnp.transpose(a.reshape(DEC_SEQ, DEC_BATCH, D_MODEL), (1, 0, 2))


def _to_time_major(a):
    return jnp.transpose(a, (1, 0, 2)).reshape(DEC_SEQ * DEC_BATCH, D_MODEL)


def _trunk(x, mods, mods_kv, h0_re, h0_im, attend, w):
    ssm_re, ssm_im = [], []
    k = v = None
    for layer in range(DEPTH):
        if layer < N_A_LAYERS:
            if mods.mode == "p":
                g, fr, fi = _ssm_prompt(x, mods, layer, w["ssm"], w["ssm_d"])
            else:
                g, fr, fi = _ssm_sample(x, mods, layer, w["ssm"], w["ssm_d"], h0_re, h0_im)
            ssm_re.append(fr)
            ssm_im.append(fi)
            x = _linear_ln(g, x, mods, layer, 2, w["glu"][layer],
                           w["ln_g"][layer, 0], w["ln_b"][layer, 0], glu=True)
        else:
            j = layer - N_A_LAYERS
            (q,) = _mod_linear(x, mods, layer, 1, 0, w["wq"][j], [(0, BF16, False)],
                               scale=HEAD_DIM ** -0.5 * LOG2E)
            o = attend(q, k, v, j)
            x = _linear_ln(o, x, mods, layer, 2, w["wo"][j],
                           w["ln_g"][layer, 0], w["ln_b"][layer, 0], glu=False)
        x = _mlp(x, mods, layer, w["w1"][layer], w["w2"][layer],
                 w["ln_g"][layer, 1], w["ln_b"][layer, 1])
        if layer == N_A_LAYERS - 1:
            outs = [(0, F32, False), (1, F32, False)] + [(0, F32, True)] * (mods.mode == "p")
            k, v, *k_out = _mod_linear(x, mods_kv, 0, 1, 0, w["wkv"], outs)
    return x, jnp.stack(ssm_re), jnp.stack(ssm_im), (k_out or [k])[0], v


def kernel(x_prompt, x_sample, state_ssm_re, state_ssm_im, cache_k, cache_v, page_table, c_prompt, c_sample, rel_bias, w_ada, b_ada, ln_g, ln_b, ssm_lam_re, ssm_lam_im, ssm_log_dt, ssm_b_re, ssm_b_im, ssm_c_re, ssm_c_im, ssm_d, ssm_w_glu_a, ssm_w_glu_b, w_ada_kv, b_ada_kv, w_kv, attn_w_q, attn_lam, attn_subln_g, attn_w_o, mlp_w1, mlp_w2):
    w = {
        "ssm": _ssm_prep(ssm_lam_re, ssm_lam_im, ssm_log_dt, ssm_b_re, ssm_b_im, ssm_c_re, ssm_c_im),
        "ssm_d": ssm_d,
        "glu": jnp.concatenate([ssm_w_glu_a, ssm_w_glu_b], axis=-1).astype(BF16),
        "wq": attn_w_q.astype(BF16),
        "wo": attn_w_o.astype(BF16),
        "wkv": w_kv.astype(BF16),
        "w1": mlp_w1.astype(BF16),
        "w2": mlp_w2.astype(BF16),
        "ln_g": ln_g,
        "ln_b": ln_b,
    }
    c_all = jnp.concatenate([c_sample, c_prompt, jnp.zeros((BATCH, D_MODEL), F32)], axis=0)
    ada = _ada(c_all, w_ada, b_ada)
    ada_kv = _ada(c_all, w_ada_kv[None], b_ada_kv[None])
    scalars = [_attn_scalars(attn_lam[j], attn_subln_g[j], N_A_LAYERS + j)
               for j in range(DEPTH - N_A_LAYERS)]

    prompt_tab = _attn_prompt_tables(rel_bias)

    def attend_prompt(q, k, v, j):
        return _attn_prompt(q, k, v, prompt_tab, *scalars[j])

    y_p, re_p, im_p, k_p, v_p = _trunk(x_prompt.reshape(BATCH * SEQ, D_MODEL), _Mods(ada, "p"),
                                       _Mods(ada_kv, "p"), None, None, attend_prompt, w)

    sample_tab = _attn_sample_tables(rel_bias)

    def attend_sample(q, k, v, j):
        pad = lambda a: jnp.pad(_to_batch_major(a).astype(BF16),
                                ((0, 0), (0, SUBLANES - DEC_SEQ), (0, 0)))
        o = _attn_sample(_to_batch_major(q), pad(k), pad(v), cache_k, cache_v, page_table,
                         sample_tab, *scalars[j])
        return _to_time_major(o).astype(BF16)

    y_s, re_s, im_s, k_s, v_s = _trunk(_to_time_major(x_sample), _Mods(ada, "s"), _Mods(ada_kv, "s"),
                                       state_ssm_re, state_ssm_im, attend_sample, w)

    return (y_p.reshape(BATCH, SEQ, D_MODEL), _to_batch_major(y_s), re_p, im_p,
            jnp.transpose(k_p.reshape(BATCH, N_HEADS, 2, HEAD_DIM, SEQ), (0, 4, 1, 2, 3)),
            v_p.reshape(BATCH, SEQ, N_HEADS, V_DIM),
            re_s, im_s,
            _to_batch_major(k_s).reshape(DEC_BATCH, DEC_SEQ, N_HEADS, 2, HEAD_DIM),
            _to_batch_major(v_s).reshape(DEC_BATCH, DEC_SEQ, N_HEADS, V_DIM))
```

```python
import functools
import math

import jax
import jax.numpy as jnp
from jax import lax
from jax.experimental import pallas as pl
from jax.experimental.pallas import tpu as pltpu

F32 = jnp.float32
BF16 = jnp.bfloat16

D_MODEL = 1024
BATCH = 4
SEQ = 4096
DEPTH = 4
DEC_BATCH = 128
DEC_SEQ = 4
PAST_LEN = 2048
PAGE_SIZE = 128
N_PAGES = PAST_LEN // PAGE_SIZE
N_A_LAYERS = DEPTH // 2
SSM_GROUP = 16
SSM_GROUPS = D_MODEL // SSM_GROUP
SSM_STATE = 64
N_STATE = SSM_GROUPS * SSM_STATE
N_HEADS = 8
HEAD_DIM = D_MODEL // (2 * N_HEADS)
V_DIM = 2 * HEAD_DIM
D_FF = 4 * D_MODEL
NUM_BUCKETS = 32
MAX_DISTANCE = 128
N_MOD = 6
ALPHA = (2.0 * DEPTH) ** 0.25
LN_EPS = 1e-5
NEG_INF = -1e30

SUBLANES = 8
LANES = 128
VMEM_LIMIT = 48 * 1024 * 1024

MOD_ROWS = DEC_BATCH + 2 * BATCH
CHUNK = LANES
N_CHUNK = D_MODEL // CHUNK
CHUNK_STATE = N_STATE // N_CHUNK
TM_PROMPT = 512
TILE = SUBLANES
T_SSM = 512
TQ = 512
HEADS_PER_STEP = 2
PAGES_PER_STEP = 8
LOG2E = math.log2(math.e)
FF_CHUNK = 1024


def _params(sem, vmem=VMEM_LIMIT):
    return pltpu.CompilerParams(dimension_semantics=sem, vmem_limit_bytes=vmem)


def _layer_norm(z, g, b):
    mu = jnp.mean(z, axis=-1, keepdims=True)
    zc = z - mu
    var = jnp.mean(zc * zc, axis=-1, keepdims=True)
    return zc * lax.rsqrt(var + LN_EPS) * g + b


def _ada_kernel(c_ref, w_ref, b_ref, o_ref):
    c = c_ref[...]
    sc = (c * jax.nn.sigmoid(c)).astype(BF16)
    w = w_ref[...].astype(BF16)
    o_ref[...] = jnp.dot(sc, w, preferred_element_type=F32) + b_ref[...]


def _ada(c_all, w, b):
    n_l, _, width = w.shape
    tn = 1024
    return pl.pallas_call(
        _ada_kernel,
        grid=(n_l, width // tn),
        in_specs=[
            pl.BlockSpec((MOD_ROWS, D_MODEL), lambda l, j: (0, 0)),
            pl.BlockSpec((None, D_MODEL, tn), lambda l, j: (l, 0, j)),
            pl.BlockSpec((None, 1, tn), lambda l, j: (l, 0, j)),
        ],
        out_specs=pl.BlockSpec((None, MOD_ROWS, tn), lambda l, j: (l, 0, j)),
        out_shape=jax.ShapeDtypeStruct((n_l, MOD_ROWS, width), F32),
        compiler_params=_params(("arbitrary", "arbitrary")),
        name="ada_mod",
    )(c_all, w, b.reshape(n_l, 1, width))


class _Mods:
    def __init__(self, arr, mode):
        self.mode = mode
        self.arr = arr if mode == "s" else arr.reshape(arr.shape[0], MOD_ROWS, 1, arr.shape[-1])

    def spec(self, layer, j, batch_of):
        if self.mode == "s":
            return pl.BlockSpec((None, DEC_BATCH, D_MODEL), lambda *g: (layer, 0, j))
        return pl.BlockSpec((None, None, 1, D_MODEL),
                            lambda *g: (layer, DEC_BATCH + batch_of(*g), 0, j))


def _row_tiling(mode):
    if mode == "s":
        return DEC_BATCH, lambda i: 0
    tiles_per_batch = SEQ // TM_PROMPT
    return TM_PROMPT, lambda i: i // tiles_per_batch


def _const_spec(shape):
    zeros = (0,) * len(shape)
    return pl.BlockSpec(shape, lambda *g: zeros, pipeline_mode=pl.Buffered(1))


def _mod_linear_kernel(x_ref, sc_ref, sh_ref, w_ref, *o_refs, scale, blocks):
    h = (x_ref[...] * (1.0 + sc_ref[...]) + sh_ref[...]).astype(BF16)
    y = jnp.dot(h, w_ref[...], preferred_element_type=F32)
    if scale != 1.0:
        y = y * scale
    for (n, transposed), o_ref in zip(blocks, o_refs):
        block = y[:, n * D_MODEL:(n + 1) * D_MODEL].astype(o_ref.dtype)
        o_ref[...] = block.T if transposed else block


def _mod_linear(x, mods, layer, j_scale, j_shift, w, outs, scale=1.0):
    n = x.shape[0]
    tm, batch_of = _row_tiling(mods.mode)
    row = pl.BlockSpec((tm, D_MODEL), lambda i: (i, 0))
    tiles_per_batch = SEQ // tm
    col = pl.BlockSpec((None, D_MODEL, tm), lambda i: (i // tiles_per_batch, 0, i % tiles_per_batch))
    return pl.pallas_call(
        functools.partial(_mod_linear_kernel, scale=scale,
                          blocks=tuple((o[0], o[2]) for o in outs)),
        grid=(n // tm,),
        in_specs=[row, mods.spec(layer, j_scale, batch_of), mods.spec(layer, j_shift, batch_of),
                  _const_spec(w.shape)],
        out_specs=[col if o[2] else row for o in outs],
        out_shape=[jax.ShapeDtypeStruct((BATCH, D_MODEL, SEQ) if o[2] else (n, D_MODEL), o[1])
                   for o in outs],
        compiler_params=_params(("arbitrary",)),
        name="mod_linear",
    )(x, mods.arr, mods.arr, w)


def _block_tail_kernel(a_ref, x_ref, gm_ref, sc_ref, sh_ref, gf_ref, wm_ref, w1_ref, w2_ref,
                       g_ref, b_ref, o_ref, *, glu):
    y = jnp.dot(a_ref[...], wm_ref[...], preferred_element_type=F32)
    if glu:
        y = y[:, :D_MODEL] * jax.nn.sigmoid(y[:, D_MODEL:])
    x = _layer_norm(ALPHA * x_ref[...] + (1.0 + gm_ref[...]) * y, g_ref[0:1, :], b_ref[0:1, :])
    h = (x * (1.0 + sc_ref[...]) + sh_ref[...]).astype(BF16)
    acc = jnp.zeros(x.shape, F32)
    for c in range(D_FF // FF_CHUNK):
        cols = slice(c * FF_CHUNK, (c + 1) * FF_CHUNK)
        hid = jnp.dot(h, w1_ref[:, cols], preferred_element_type=F32)
        hid = jnp.square(jnp.maximum(hid, 0.0)).astype(BF16)
        acc = acc + jnp.dot(hid, w2_ref[cols, :], preferred_element_type=F32)
    z = ALPHA * x + (1.0 + gf_ref[...]) * acc
    o_ref[...] = _layer_norm(z, g_ref[1:2, :], b_ref[1:2, :])


def _block_tail(a, x, mods, layer, w_mix, w1, w2, ln_g, ln_b, glu):
    n = x.shape[0]
    tm, batch_of = _row_tiling(mods.mode)
    row = pl.BlockSpec((tm, D_MODEL), lambda i: (i, 0))
    ln = pl.BlockSpec((None, 2, D_MODEL), lambda i: (layer, 0, 0))
    weight = lambda w: pl.BlockSpec((None,) + w.shape[1:], lambda i: (layer % w.shape[0], 0, 0),
                                    pipeline_mode=pl.Buffered(1))
    return pl.pallas_call(
        functools.partial(_block_tail_kernel, glu=glu),
        grid=(n // tm,),
        in_specs=[row, row] + [mods.spec(layer, j, batch_of) for j in (2, 4, 3, 5)]
                 + [weight(w_mix), weight(w1), weight(w2), ln, ln],
        out_specs=row,
        out_shape=jax.ShapeDtypeStruct((n, D_MODEL), F32),
        compiler_params=_params(("arbitrary",)),
        name="block_tail",
    )(a, x, mods.arr, mods.arr, mods.arr, mods.arr, w_mix, w1, w2, ln_g, ln_b)


def _cmul(ar, ai, br, bi):
    return ar * br - ai * bi, ar * bi + ai * br


def _ssm_prep(lam_re, lam_im, log_dt, b_re, b_im, c_re, c_im):
    n_l = lam_re.shape[0]
    dt = jnp.exp(log_dt)[..., None]
    ns = list(range(TILE + 1)) + [TILE * m for m in range(2, SUBLANES + 1)]
    n = jnp.asarray(ns, F32).reshape(-1, 1, 1, 1)
    mag = jnp.exp(n * (lam_re * dt))
    pr = mag * jnp.cos(n * (lam_im * dt))
    pi = mag * jnp.sin(n * (lam_im * dt))
    ar, ai = pr[1], pi[1]
    den = lam_re * lam_re + lam_im * lam_im
    er = ((ar - 1.0) * lam_re + ai * lam_im) / den
    ei = (ai * lam_re - (ar - 1.0) * lam_im) / den
    bbr = er[..., None] * b_re - ei[..., None] * b_im
    bbi = er[..., None] * b_im + ei[..., None] * b_re
    gpc = CHUNK // SSM_GROUP
    exact = lax.Precision.HIGHEST

    def block_diag(m, rows_per_group, cols_per_group, steps=1):
        y = steps * cols_per_group
        c = jnp.arange(steps * gpc * cols_per_group)
        step, group, inner = c // (gpc * cols_per_group), (c // cols_per_group) % gpc, c % cols_per_group
        select = (jnp.arange(y)[:, None] == (step * cols_per_group + inner)[None, :]).astype(F32)
        wide = jnp.einsum("...ry,yc->...rc", m, select, precision=exact)
        row_group = (jnp.arange(m.shape[-2]) // rows_per_group) % gpc
        return jnp.where(row_group[:, None] == group[None, :], wide, 0.0).astype(BF16)

    def per_chunk(m, order):
        x, _, _, a, b = m.shape
        m = m.reshape(x, n_l, N_CHUNK, gpc, a, b)
        return jnp.transpose(m, (1, 2) + tuple({"x": 0, "g": 3, "a": 4, "b": 5}[o] for o in order))

    swap = lambda m: jnp.swapaxes(m, -1, -2)
    halves = lambda m: (m[0], m[1])
    b_in = per_chunk(jnp.stack([swap(bbr), swap(bbi)]), "xgab")
    bcat = jnp.concatenate(halves(jnp.moveaxis(block_diag(
        b_in.reshape(n_l, N_CHUNK, 2, CHUNK, SSM_STATE), SSM_GROUP, SSM_STATE), 2, 0)), axis=-1)
    c_out = per_chunk(jnp.stack([swap(c_re), -swap(c_im)]), "xgab")
    ccat = block_diag(c_out.reshape(n_l, N_CHUNK, 2 * CHUNK_STATE, SSM_GROUP), SSM_STATE, SSM_GROUP)

    abr, abi = _cmul(pr[:TILE, ..., None], pi[:TILE, ..., None], bbr, bbi)
    dot_p = functools.partial(jnp.einsum, "lghp,jlgpi->jlgih", precision=exact)
    taps = per_chunk(dot_p(c_re, abr) - dot_p(c_im, abi), "xgab")
    kcat = block_diag(taps.reshape(n_l, N_CHUNK, TILE * CHUNK, SSM_GROUP), SSM_GROUP, SSM_GROUP)

    ends = [per_chunk(swap(w[::-1]), "xgab").reshape(n_l, N_CHUNK, TILE * CHUNK, SSM_STATE)
            for w in (abr, abi)]
    wst = jnp.concatenate([block_diag(w, SSM_GROUP, SSM_STATE) for w in ends], axis=-1)

    mr, mi = _cmul(pr[1:TILE + 1, :, :, None, :], pi[1:TILE + 1, :, :, None, :], c_re, c_im)
    outs = [per_chunk(m, "gbxa").reshape(n_l, N_CHUNK, CHUNK_STATE, TILE * SSM_GROUP)
            for m in (mr, -mi)]
    call = jnp.concatenate([block_diag(m, SSM_STATE, SSM_GROUP, steps=TILE) for m in outs], axis=-2)

    flat = lambda a: jnp.moveaxis(a, 0, 1).reshape(n_l, -1, N_STATE)
    rows = lax.broadcasted_iota(jnp.int32, (SUBLANES, N_STATE), 0)
    parts = []
    for a in (flat(pr[TILE:]), flat(pi[TILE:])):
        masked = lambda m, first: jnp.where(rows >= first, a[:, m - 1:m, :], 0.0)
        parts.append([masked(1, 1), masked(2, 2), masked(4, 4), a, masked(SUBLANES, 0)])
    tab = jnp.stack([p for pair in zip(*parts) for p in pair], axis=1)
    atab = jnp.stack([jnp.broadcast_to(flat(a[1:2]), (n_l, SUBLANES, N_STATE)) for a in (pr, pi)],
                     axis=1)
    return dict(bcat=bcat, ccat=ccat, kcat=kcat, wst=wst, call=call, tab=tab, atab=atab)


def _ssm_prompt_kernel(x_ref, sc_ref, sh_ref, kcat_ref, wst_ref, call_ref, d_ref, tab_ref,
                       g_ref, hre_ref, him_ref, h_ref, buf_ref, y_ref):
    n_tiles = SEQ // TILE
    half = CHUNK_STATE
    h_ref[...] = x_ref[...] * (1.0 + sc_ref[...]) + sh_ref[...]

    steps = [h_ref[pl.ds(r, n_tiles, stride=TILE), :].astype(BF16) for r in range(TILE)]
    buf_ref[...] = jnp.dot(jnp.concatenate(steps, axis=1), wst_ref[...],
                           preferred_element_type=F32)

    row = lax.broadcasted_iota(jnp.int32, (SUBLANES, half), 0)

    def tile_scan(i, carry):
        cr, ci = carry
        r0 = pl.multiple_of(i * SUBLANES, SUBLANES)
        vr = buf_ref[pl.ds(r0, SUBLANES), 0:half]
        vi = buf_ref[pl.ds(r0, SUBLANES), half:2 * half]
        for lvl, shift in enumerate((1, 2, 4)):
            pr, pi = _cmul(tab_ref[2 * lvl], tab_ref[2 * lvl + 1],
                           pltpu.roll(vr, shift, 0), pltpu.roll(vi, shift, 0))
            vr, vi = vr + pr, vi + pi
        pr, pi = _cmul(tab_ref[6], tab_ref[7], cr, ci)
        buf_ref[pl.ds(r0, SUBLANES), 0:half] = jnp.where(row == 0, cr, pltpu.roll(vr + pr, 1, 0))
        buf_ref[pl.ds(r0, SUBLANES), half:2 * half] = jnp.where(row == 0, ci,
                                                                 pltpu.roll(vi + pi, 1, 0))
        pr, pi = _cmul(tab_ref[8], tab_ref[9], cr, ci)
        last = SUBLANES - 1
        return (jnp.broadcast_to(vr[last:, :], vr.shape) + pr,
                jnp.broadcast_to(vi[last:, :], vi.shape) + pi)

    zero = jnp.zeros((SUBLANES, half), F32)
    cr, ci = lax.fori_loop(0, n_tiles // SUBLANES, tile_scan, (zero, zero))
    hre_ref[...] = cr[0:1, :]
    him_ref[...] = ci[0:1, :]

    y_state = jnp.dot(buf_ref[...].astype(BF16), call_ref[...], preferred_element_type=F32)
    for s in range(TILE):
        y_ref[pl.ds(s, n_tiles, stride=TILE), :] = y_state[:, s * CHUNK:(s + 1) * CHUNK]

    step_in_tile = lax.broadcasted_iota(jnp.int32, (T_SSM, CHUNK), 0) % TILE
    for blk in range(SEQ // T_SSM):
        rows = slice(blk * T_SSM, (blk + 1) * T_SSM)
        h = h_ref[rows, :]
        lags = [h.astype(BF16)] + [
            jnp.where(step_in_tile >= j, pltpu.roll(h, j, 0), 0.0).astype(BF16)
            for j in range(1, TILE)]
        y = jnp.dot(jnp.concatenate(lags, axis=1), kcat_ref[...], preferred_element_type=F32)
        y = y + y_ref[rows, :] + d_ref[...] * h
        g_ref[rows, :] = jax.nn.gelu(y).astype(g_ref.dtype)


def _ssm_prompt(x, mods, layer, p, d_skip):
    col = pl.BlockSpec((SEQ, CHUNK), lambda k, b: (b, k))
    fin = pl.BlockSpec((None, 1, CHUNK_STATE), lambda k, b: (b, 0, k))
    mod = lambda j: pl.BlockSpec((None, None, 1, CHUNK),
                                 lambda k, b: (layer, DEC_BATCH + b, 0, j * N_CHUNK + k))
    weight = lambda a: pl.BlockSpec((None, None) + a.shape[2:], lambda k, b: (layer, k, 0, 0))
    g, hre, him = pl.pallas_call(
        _ssm_prompt_kernel,
        grid=(N_CHUNK, BATCH),
        in_specs=[col, mod(1), mod(0), weight(p["kcat"]), weight(p["wst"]), weight(p["call"]),
                  pl.BlockSpec((None, 1, CHUNK), lambda k, b: (layer, 0, k)),
                  pl.BlockSpec((None, p["tab"].shape[1], SUBLANES, CHUNK_STATE),
                               lambda k, b: (layer, 0, 0, k))],
        out_specs=[col, fin, fin],
        out_shape=[jax.ShapeDtypeStruct((BATCH * SEQ, D_MODEL), BF16),
                   jax.ShapeDtypeStruct((BATCH, 1, N_STATE), F32),
                   jax.ShapeDtypeStruct((BATCH, 1, N_STATE), F32)],
        scratch_shapes=[pltpu.VMEM((SEQ, CHUNK), F32),
                        pltpu.VMEM((SEQ // TILE, 2 * CHUNK_STATE), F32),
                        pltpu.VMEM((SEQ, CHUNK), F32)],
        compiler_params=_params(("arbitrary", "arbitrary")),
        name="ssm_prompt",
    )(x, mods.arr, mods.arr, p["kcat"], p["wst"], p["call"], d_skip[:, None, :], p["tab"])
    shape = (BATCH, SSM_GROUPS, SSM_STATE)
    return g, hre.reshape(shape), him.reshape(shape)


def _ssm_sample_kernel(x_ref, sc_ref, sh_ref, bcat_ref, ccat_ref, d_ref, a_ref, h0r_ref, h0i_ref,
                       g_ref, hre_ref, him_ref):
    tile = lambda m: jnp.concatenate([m] * DEC_SEQ, axis=0)
    h = x_ref[...] * (1.0 + tile(sc_ref[...])) + tile(sh_ref[...])
    bu = jnp.dot(h.astype(BF16), bcat_ref[...], preferred_element_type=F32)
    ar = a_ref[0, 0:1, :]
    ai = a_ref[1, 0:1, :]
    sr = h0r_ref[...]
    si = h0i_ref[...]
    states = []
    for t in range(DEC_SEQ):
        rows = slice(t * DEC_BATCH, (t + 1) * DEC_BATCH)
        pr, pi = _cmul(ar, ai, sr, si)
        sr = pr + bu[rows, :CHUNK_STATE]
        si = pi + bu[rows, CHUNK_STATE:]
        states.append(jnp.concatenate([sr, si], axis=1))
    xs = jnp.concatenate(states, axis=0).astype(BF16)
    y = jnp.dot(xs, ccat_ref[...], preferred_element_type=F32) + d_ref[...] * h
    g_ref[...] = jax.nn.gelu(y).astype(g_ref.dtype)
    hre_ref[...] = sr
    him_ref[...] = si


def _ssm_sample(x, mods, layer, p, d_skip, h0_re, h0_im):
    n = DEC_BATCH * DEC_SEQ
    col = pl.BlockSpec((n, CHUNK), lambda k: (0, k))
    st = pl.BlockSpec((None, DEC_BATCH, CHUNK_STATE), lambda k: (layer, 0, k))
    mod = lambda j: pl.BlockSpec((None, DEC_BATCH, CHUNK), lambda k: (layer, 0, j * N_CHUNK + k))
    g, hre, him = pl.pallas_call(
        _ssm_sample_kernel,
        grid=(N_CHUNK,),
        in_specs=[col, mod(1), mod(0),
                  pl.BlockSpec((None, None, CHUNK, 2 * CHUNK_STATE), lambda k: (layer, k, 0, 0)),
                  pl.BlockSpec((None, None, 2 * CHUNK_STATE, CHUNK), lambda k: (layer, k, 0, 0)),
                  pl.BlockSpec((None, 1, CHUNK), lambda k: (layer, 0, k)),
                  pl.BlockSpec((None, 2, SUBLANES, CHUNK_STATE), lambda k: (layer, 0, 0, k)),
                  st, st],
        out_specs=[col] + [pl.BlockSpec((DEC_BATCH, CHUNK_STATE), lambda k: (0, k))] * 2,
        out_shape=[jax.ShapeDtypeStruct((n, D_MODEL), BF16),
                   jax.ShapeDtypeStruct((DEC_BATCH, N_STATE), F32),
                   jax.ShapeDtypeStruct((DEC_BATCH, N_STATE), F32)],
        compiler_params=_params(("arbitrary",)),
        name="ssm_sample",
    )(x, mods.arr, mods.arr, p["bcat"], p["ccat"], d_skip[:, None, :], p["atab"],
      h0_re.reshape(-1, DEC_BATCH, N_STATE), h0_im.reshape(-1, DEC_BATCH, N_STATE))
    shape = (DEC_BATCH, SSM_GROUPS, SSM_STATE)
    return g, hre.reshape(shape), him.reshape(shape)


def _rel_bucket(rel):
    n = jnp.maximum(rel, 0)
    max_exact = NUM_BUCKETS // 2
    large = max_exact + (jnp.log(jnp.maximum(n, 1).astype(F32) / max_exact)
                         / math.log(MAX_DISTANCE / max_exact) * (NUM_BUCKETS - max_exact)).astype(jnp.int32)
    large = jnp.minimum(large, NUM_BUCKETS - 1)
    return jnp.where(n < max_exact, n, large)


def _attn_scalars(attn_lam, subln_g, layer):
    lam_init = 0.8 - 0.6 * math.exp(-0.3 * layer)
    lam = (jnp.exp(jnp.sum(attn_lam[0] * attn_lam[1])) - jnp.exp(jnp.sum(attn_lam[2] * attn_lam[3]))
           + lam_init)
    lam_tile = jnp.full((SUBLANES, LANES), lam, F32)
    gain = (subln_g * (1.0 - lam_init)).reshape(1, V_DIM)
    return lam_tile, gain


def _bias_by_distance(rel_bias, n):
    bv = rel_bias[_rel_bucket(jnp.arange(n, dtype=jnp.int32))]
    return ((bv - rel_bias[NUM_BUCKETS - 1]) * LOG2E).T


def _toeplitz(w, n):
    u = jnp.concatenate([w[::-1], w[:1]])
    r = jnp.tile(u, n)[:n * (2 * n - 1)].reshape(n, 2 * n - 1)
    return r[:, n - 1:]


def _softmax_update(s, m_prev):
    m_new = jnp.maximum(m_prev, jnp.max(s, axis=1, keepdims=True))
    p = jnp.exp2(s - jnp.concatenate([m_new] * (s.shape[1] // LANES), axis=1))
    return m_new, p.astype(BF16), jnp.exp2(m_prev - m_new)


def _with_ones(v):
    return jnp.concatenate([v, jnp.ones((v.shape[0], V_DIM), BF16)], axis=1)


def _acc_update(acc, alpha, p, vx):
    return (jnp.concatenate([alpha, alpha], axis=1) * acc
            + jnp.dot(p, vx, preferred_element_type=F32))


_NT = (((1,), (1,)), ((), ()))


def _attn_prompt_kernel(q_ref, k_ref, v_ref, tab_ref, lam_ref, gain_ref, o_ref,
                        q2_ref, m_ref, acc_ref, s_ref):
    i = pl.program_id(2)
    for hh in range(HEADS_PER_STEP):
        q = q_ref[:, hh * V_DIM:(hh + 1) * V_DIM]
        lane = lax.broadcasted_iota(jnp.int32, q.shape, 1)
        zero = jnp.zeros_like(q)
        q2_ref[hh, 0:TQ, :] = jnp.where(lane < HEAD_DIM, q, zero)
        q2_ref[hh, TQ:2 * TQ, :] = jnp.where(lane >= HEAD_DIM, q, zero)
    m_ref[...] = jnp.full(m_ref.shape, NEG_INF, F32)
    acc_ref[...] = jnp.zeros_like(acc_ref)

    def key_rows(j):
        return pl.ds(pl.multiple_of(j * TQ, TQ), TQ)

    def logits(hh, j):
        cols = slice(hh * V_DIM, (hh + 1) * V_DIM)
        return lax.dot_general(q2_ref[hh], k_ref[key_rows(j), cols].astype(BF16), _NT,
                               preferred_element_type=F32)

    def consume(hh, j, s, table):
        cols = slice(hh * V_DIM, (hh + 1) * V_DIM)
        if table is not None:
            bias = tab_ref[hh, table]
            s = s + jnp.concatenate([bias, bias], axis=0)
        m_new, p, alpha = _softmax_update(s, m_ref[hh])
        acc_ref[hh] = _acc_update(acc_ref[hh], alpha, p,
                                  _with_ones(v_ref[key_rows(j), cols].astype(BF16)))
        m_ref[hh] = m_new

    def step(j, table, more):
        consume(0, j, s_ref[...], table)
        for hh in range(1, HEADS_PER_STEP):
            consume(hh, j, logits(hh, j), table)
        if more:
            s_ref[...] = logits(0, j + 1)

    s_ref[...] = logits(0, 0)

    def far(j, c):
        step(j, None, True)
        return c

    lax.fori_loop(0, jnp.maximum(i - 1, 0), far, 0)

    @pl.when(i >= 1)
    def _():
        step(i - 1, 1, True)

    step(i, 0, False)

    for hh in range(HEADS_PER_STEP):
        acc = acc_ref[hh]
        o = acc[:, :V_DIM] / acc[:, V_DIM:]
        o = o[0:TQ, :] - lam_ref[0:1, 0:1] * o[TQ:2 * TQ, :]
        o = o * lax.rsqrt(jnp.mean(o * o, axis=-1, keepdims=True) + LN_EPS) * gain_ref[...]
        o_ref[:, hh * V_DIM:(hh + 1) * V_DIM] = o.astype(o_ref.dtype)


def _attn_prompt_tables(rel_bias):
    bv = _bias_by_distance(rel_bias, 2 * TQ)
    masked = jnp.concatenate([jnp.full((N_HEADS, TQ - 1), NEG_INF, F32), bv[:, :TQ]], axis=1)
    toeplitz = jax.vmap(lambda w: _toeplitz(w, TQ))
    return jnp.stack([toeplitz(masked), toeplitz(bv[:, 1:])], axis=1)


def _attn_prompt(q, k, v, tab, lam_tile, gain):
    n_q = SEQ // TQ
    width = HEADS_PER_STEP * V_DIM
    shape3 = (BATCH, SEQ, D_MODEL)
    qspec = pl.BlockSpec((None, TQ, width), lambda b, h, i: (b, i, h))
    kvspec = pl.BlockSpec((None, SEQ, width), lambda b, h, i: (b, 0, h))
    o = pl.pallas_call(
        _attn_prompt_kernel,
        grid=(BATCH, N_HEADS // HEADS_PER_STEP, n_q),
        in_specs=[qspec, kvspec, kvspec,
                  pl.BlockSpec((HEADS_PER_STEP, 2, TQ, TQ), lambda b, h, i: (h, 0, 0, 0)),
                  pl.BlockSpec((SUBLANES, LANES), lambda b, h, i: (0, 0)),
                  pl.BlockSpec((1, V_DIM), lambda b, h, i: (0, 0))],
        out_specs=qspec,
        out_shape=jax.ShapeDtypeStruct(shape3, BF16),
        scratch_shapes=[pltpu.VMEM((HEADS_PER_STEP, 2 * TQ, V_DIM), BF16),
                        pltpu.VMEM((HEADS_PER_STEP, 2 * TQ, LANES), F32),
                        pltpu.VMEM((HEADS_PER_STEP, 2 * TQ, 2 * V_DIM), F32),
                        pltpu.VMEM((2 * TQ, TQ), F32)],
        compiler_params=_params(("arbitrary", "arbitrary", "arbitrary")),
        name="attn_prompt",
    )(q.reshape(shape3), k.reshape(shape3), v.reshape(shape3), tab, lam_tile, gain)
    return o.reshape(BATCH * SEQ, D_MODEL)


N_SROWS = N_HEADS * 2 * DEC_SEQ


def _attn_sample_kernel(pt_ref, q_ref, qmask_ref, knew_ref, vnew_ref, bias_ref, bnew_ref,
                        lam_ref, gain_ref, *rest):
    del pt_ref
    k_refs = rest[:PAGES_PER_STEP]
    v_refs = rest[PAGES_PER_STEP:2 * PAGES_PER_STEP]
    o_ref, qx_ref, m_ref, acc_ref = rest[2 * PAGES_PER_STEP:]
    step = pl.program_id(1)
    rows_per_head = 2 * DEC_SEQ

    @pl.when(step == 0)
    def _():
        q = q_ref[...].astype(F32)
        q = jnp.concatenate([q, q], axis=0)
        qx_ref[...] = (jnp.concatenate([q] * N_HEADS, axis=0) * qmask_ref[...]).astype(BF16)
        m_ref[...] = jnp.full(m_ref.shape, NEG_INF, F32)
        acc_ref[...] = jnp.zeros_like(acc_ref)

    def update(s, values_of_head):
        m_new, p, alpha = _softmax_update(s, m_ref[...])
        for h in range(N_HEADS):
            r = slice(h * rows_per_head, (h + 1) * rows_per_head)
            acc_ref[r, :] = _acc_update(acc_ref[r, :], alpha[r, :], p[r, :],
                                        _with_ones(values_of_head(h)))
        m_ref[...] = m_new

    qx = qx_ref[...]
    kt = jnp.concatenate([r[...].astype(BF16) for r in k_refs], axis=1)
    s = jnp.dot(qx, kt, preferred_element_type=F32) + bias_ref[...]
    update(s, lambda h: jnp.concatenate(
        [r[pl.ds(h, PAGE_SIZE, stride=N_HEADS), :].astype(BF16) for r in v_refs], axis=0))

    @pl.when(step == pl.num_programs(1) - 1)
    def _():
        pad = jnp.zeros((LANES - SUBLANES, D_MODEL), BF16)
        kn = jnp.concatenate([knew_ref[...], pad], axis=0)
        vn = jnp.concatenate([vnew_ref[...], pad], axis=0)
        sn = lax.dot_general(qx, kn, _NT, preferred_element_type=F32) + bnew_ref[...]
        update(sn, lambda h: vn[:, h * V_DIM:(h + 1) * V_DIM])
        acc = acc_ref[...]
        o = acc[:, :V_DIM] / acc[:, V_DIM:]
        heads = []
        for h in range(N_HEADS):
            r0 = h * rows_per_head
            oh = o[r0:r0 + DEC_SEQ, :] - lam_ref[0:1, 0:1] * o[r0 + DEC_SEQ:r0 + rows_per_head, :]
            oh = oh * lax.rsqrt(jnp.mean(oh * oh, axis=-1, keepdims=True) + LN_EPS) * gain_ref[...]
            heads.append(oh)
        o_ref[...] = jnp.concatenate(heads, axis=1)


def _attn_sample_tables(rel_bias):
    bv = _bias_by_distance(rel_bias, PAST_LEN + DEC_SEQ)
    past = jnp.stack([bv[:, t + 1:t + 1 + PAST_LEN][:, ::-1] for t in range(DEC_SEQ)], axis=1)
    rel = (jnp.arange(DEC_SEQ, dtype=jnp.int32)[:, None]
           - jnp.arange(LANES, dtype=jnp.int32)[None, :])
    new = jnp.where(rel >= 0, bv[:, jnp.clip(rel, 0, DEC_SEQ - 1)], NEG_INF)
    rows = lambda a: jnp.broadcast_to(a[:, None], (N_HEADS, 2) + a.shape[1:]).reshape(
        N_SROWS, a.shape[-1])
    col = jnp.arange(D_MODEL, dtype=jnp.int32)[None, :]
    row = jnp.arange(N_SROWS, dtype=jnp.int32)[:, None]
    qmask = ((col // V_DIM == row // (2 * DEC_SEQ))
             & ((col % V_DIM) // HEAD_DIM == (row // DEC_SEQ) % 2)).astype(F32)
    return rows(past), rows(new), qmask


def _attn_sample(q, k_new, v_new, cache_k, cache_v, page_table, tables, lam_tile, gain):
    past, new, qmask = tables
    n_phys = cache_k.shape[0]
    ck = jnp.transpose(cache_k, (0, 2, 3, 4, 1)).reshape(n_phys, D_MODEL, PAGE_SIZE)
    cv = cache_v.reshape(n_phys, PAGE_SIZE * N_HEADS, V_DIM)
    n_steps = N_PAGES // PAGES_PER_STEP
    keys_per_step = PAGES_PER_STEP * PAGE_SIZE

    def page_spec(r):
        return pl.BlockSpec((None, D_MODEL, PAGE_SIZE),
                            lambda b, s, pt: (pt[b * N_PAGES + s * PAGES_PER_STEP + r], 0, 0))

    per_seq = lambda rows: pl.BlockSpec((None, rows, D_MODEL), lambda b, s, pt: (b, 0, 0))
    const = lambda shape: pl.BlockSpec(shape, lambda b, s, pt: (0,) * len(shape))
    grid_spec = pltpu.PrefetchScalarGridSpec(
        num_scalar_prefetch=1,
        grid=(DEC_BATCH, n_steps),
        in_specs=[per_seq(DEC_SEQ), const(qmask.shape), per_seq(SUBLANES), per_seq(SUBLANES),
                  pl.BlockSpec((N_SROWS, keys_per_step), lambda b, s, pt: (0, s)),
                  const(new.shape), const(lam_tile.shape), const(gain.shape)]
                 + [page_spec(r) for r in range(PAGES_PER_STEP)] * 2,
        out_specs=per_seq(DEC_SEQ),
        scratch_shapes=[pltpu.VMEM((N_SROWS, D_MODEL), BF16), pltpu.VMEM((N_SROWS, LANES), F32),
                        pltpu.VMEM((N_SROWS, 2 * V_DIM), F32)],
    )
    return pl.pallas_call(
        _attn_sample_kernel,
        grid_spec=grid_spec,
        out_shape=jax.ShapeDtypeStruct((DEC_BATCH, DEC_SEQ, D_MODEL), F32),
        compiler_params=_params(("arbitrary", "arbitrary")),
        name="attn_sample",
    )(page_table.reshape(-1), q, qmask, k_new, v_new, past, new, lam_tile, gain,
      *([ck] * PAGES_PER_STEP), *([cv] * PAGES_PER_STEP))


def _to_batch_major(a):
    return jnp.transpose(a.reshape(DEC_SEQ, DEC_BATCH, D_MODEL), (1, 0, 2))


def _to_time_major(a):
    return jnp.transpose(a, (1, 0, 2)).reshape(DEC_SEQ * DEC_BATCH, D_MODEL)


def _trunk(x, mods, mods_kv, h0_re, h0_im, attend, w):
    ssm_re, ssm_im = [], []
    k = v = None
    for layer in range(DEPTH):
        if layer < N_A_LAYERS:
            if mods.mode == "p":
                g, fr, fi = _ssm_prompt(x, mods, layer, w["ssm"], w["ssm_d"])
            else:
                g, fr, fi = _ssm_sample(x, mods, layer, w["ssm"], w["ssm_d"], h0_re, h0_im)
            ssm_re.append(fr)
            ssm_im.append(fi)
            mixed, w_mix = g, w["glu"]
        else:
            j = layer - N_A_LAYERS
            (q,) = _mod_linear(x, mods, layer, 1, 0, w["wq"][j], [(0, BF16, False)],
                               scale=HEAD_DIM ** -0.5 * LOG2E)
            mixed, w_mix = attend(q, k, v, j), w["wo"]
        x = _block_tail(mixed, x, mods, layer, w_mix, w["w1"], w["w2"], w["ln_g"], w["ln_b"],
                        glu=layer < N_A_LAYERS)
        if layer == N_A_LAYERS - 1:
            outs = [(0, F32, False), (1, F32, False)] + [(0, F32, True)] * (mods.mode == "p")
            k, v, *k_out = _mod_linear(x, mods_kv, 0, 1, 0, w["wkv"], outs)
    return x, jnp.stack(ssm_re), jnp.stack(ssm_im), (k_out or [k])[0], v


def kernel(x_prompt, x_sample, state_ssm_re, state_ssm_im, cache_k, cache_v, page_table, c_prompt, c_sample, rel_bias, w_ada, b_ada, ln_g, ln_b, ssm_lam_re, ssm_lam_im, ssm_log_dt, ssm_b_re, ssm_b_im, ssm_c_re, ssm_c_im, ssm_d, ssm_w_glu_a, ssm_w_glu_b, w_ada_kv, b_ada_kv, w_kv, attn_w_q, attn_lam, attn_subln_g, attn_w_o, mlp_w1, mlp_w2):
    w = {
        "ssm": _ssm_prep(ssm_lam_re, ssm_lam_im, ssm_log_dt, ssm_b_re, ssm_b_im, ssm_c_re, ssm_c_im),
        "ssm_d": ssm_d,
        "glu": jnp.concatenate([ssm_w_glu_a, ssm_w_glu_b], axis=-1).astype(BF16),
        "wq": attn_w_q.astype(BF16),
        "wo": attn_w_o.astype(BF16),
        "wkv": w_kv.astype(BF16),
        "w1": mlp_w1.astype(BF16),
        "w2": mlp_w2.astype(BF16),
        "ln_g": ln_g,
        "ln_b": ln_b,
    }
    c_all = jnp.concatenate([c_sample, c_prompt, jnp.zeros((BATCH, D_MODEL), F32)], axis=0)
    ada = _ada(c_all, w_ada, b_ada)
    ada_kv = _ada(c_all, w_ada_kv[None], b_ada_kv[None])
    scalars = [_attn_scalars(attn_lam[j], attn_subln_g[j], N_A_LAYERS + j)
               for j in range(DEPTH - N_A_LAYERS)]

    prompt_tab = _attn_prompt_tables(rel_bias)

    def attend_prompt(q, k, v, j):
        return _attn_prompt(q, k, v, prompt_tab, *scalars[j])

    y_p, re_p, im_p, k_p, v_p = _trunk(x_prompt.reshape(BATCH * SEQ, D_MODEL), _Mods(ada, "p"),
                                       _Mods(ada_kv, "p"), None, None, attend_prompt, w)

    sample_tab = _attn_sample_tables(rel_bias)

    def attend_sample(q, k, v, j):
        pad = lambda a: jnp.pad(_to_batch_major(a).astype(BF16),
                                ((0, 0), (0, SUBLANES - DEC_SEQ), (0, 0)))
        o = _attn_sample(_to_batch_major(q), pad(k), pad(v), cache_k, cache_v, page_table,
                         sample_tab, *scalars[j])
        return _to_time_major(o).astype(BF16)

    y_s, re_s, im_s, k_s, v_s = _trunk(_to_time_major(x_sample), _Mods(ada, "s"), _Mods(ada_kv, "s"),
                                       state_ssm_re, state_ssm_im, attend_sample, w)

    return (y_p.reshape(BATCH, SEQ, D_MODEL), _to_batch_major(y_s), re_p, im_p,
            jnp.transpose(k_p.reshape(BATCH, N_HEADS, 2, HEAD_DIM, SEQ), (0, 4, 1, 2, 3)),
            v_p.reshape(BATCH, SEQ, N_HEADS, V_DIM),
            re_s, im_s,
            _to_batch_major(k_s).reshape(DEC_BATCH, DEC_SEQ, N_HEADS, 2, HEAD_DIM),
            _to_batch_major(v_s).reshape(DEC_BATCH, DEC_SEQ, N_HEADS, V_DIM))
```

```python
import functools
import math

import jax
import jax.numpy as jnp
from jax import lax
from jax.experimental import pallas as pl
from jax.experimental.pallas import tpu as pltpu

F32 = jnp.float32
BF16 = jnp.bfloat16

D_MODEL = 1024
BATCH = 4
SEQ = 4096
DEPTH = 4
DEC_BATCH = 128
DEC_SEQ = 4
PAST_LEN = 2048
PAGE_SIZE = 128
N_PAGES = PAST_LEN // PAGE_SIZE
N_A_LAYERS = DEPTH // 2
SSM_GROUP = 16
SSM_GROUPS = D_MODEL // SSM_GROUP
SSM_STATE = 64
N_STATE = SSM_GROUPS * SSM_STATE
N_HEADS = 8
HEAD_DIM = D_MODEL // (2 * N_HEADS)
V_DIM = 2 * HEAD_DIM
D_FF = 4 * D_MODEL
NUM_BUCKETS = 32
MAX_DISTANCE = 128
N_MOD = 6
ALPHA = (2.0 * DEPTH) ** 0.25
LN_EPS = 1e-5
NEG_INF = -1e30

SUBLANES = 8
LANES = 128
VMEM_LIMIT = 48 * 1024 * 1024

MOD_ROWS = DEC_BATCH + 2 * BATCH
CHUNK = LANES
N_CHUNK = D_MODEL // CHUNK
CHUNK_STATE = N_STATE // N_CHUNK
TM_PROMPT = 512
TILE = SUBLANES
T_SSM = 512
TQ = 512
HEADS_PER_STEP = 2
PAGES_PER_STEP = 16
LOG2E = math.log2(math.e)
FF_CHUNK = 1024


def _params(sem, vmem=VMEM_LIMIT):
    return pltpu.CompilerParams(dimension_semantics=sem, vmem_limit_bytes=vmem)


def _layer_norm(z, g, b):
    mu = jnp.mean(z, axis=-1, keepdims=True)
    zc = z - mu
    var = jnp.mean(zc * zc, axis=-1, keepdims=True)
    return zc * lax.rsqrt(var + LN_EPS) * g + b


def _ada_kernel(c_ref, w_ref, b_ref, o_ref):
    c = c_ref[...]
    sc = (c * jax.nn.sigmoid(c)).astype(BF16)
    w = w_ref[...].astype(BF16)
    o_ref[...] = jnp.dot(sc, w, preferred_element_type=F32) + b_ref[...]


def _ada(c_all, w, b):
    n_l, _, width = w.shape
    tn = 1024
    return pl.pallas_call(
        _ada_kernel,
        grid=(n_l, width // tn),
        in_specs=[
            pl.BlockSpec((MOD_ROWS, D_MODEL), lambda l, j: (0, 0)),
            pl.BlockSpec((None, D_MODEL, tn), lambda l, j: (l, 0, j)),
            pl.BlockSpec((None, 1, tn), lambda l, j: (l, 0, j)),
        ],
        out_specs=pl.BlockSpec((None, MOD_ROWS, tn), lambda l, j: (l, 0, j)),
        out_shape=jax.ShapeDtypeStruct((n_l, MOD_ROWS, width), F32),
        compiler_params=_params(("arbitrary", "arbitrary")),
        name="ada_mod",
    )(c_all, w, b.reshape(n_l, 1, width))


class _Mods:
    def __init__(self, arr, mode):
        self.mode = mode
        self.arr = arr if mode == "s" else arr.reshape(arr.shape[0], MOD_ROWS, 1, arr.shape[-1])

    def spec(self, layer, j, batch_of):
        if self.mode == "s":
            return pl.BlockSpec((None, DEC_BATCH, D_MODEL), lambda *g: (layer, 0, j))
        return pl.BlockSpec((None, None, 1, D_MODEL),
                            lambda *g: (layer, DEC_BATCH + batch_of(*g), 0, j))


def _row_tiling(mode):
    if mode == "s":
        return DEC_BATCH, lambda i: 0
    tiles_per_batch = SEQ // TM_PROMPT
    return TM_PROMPT, lambda i: i // tiles_per_batch


def _const_spec(shape):
    zeros = (0,) * len(shape)
    return pl.BlockSpec(shape, lambda *g: zeros, pipeline_mode=pl.Buffered(1))


def _mod_linear_kernel(x_ref, sc_ref, sh_ref, w_ref, *o_refs, scale, blocks):
    h = (x_ref[...] * (1.0 + sc_ref[...]) + sh_ref[...]).astype(BF16)
    y = jnp.dot(h, w_ref[...], preferred_element_type=F32)
    if scale != 1.0:
        y = y * scale
    for (n, transposed), o_ref in zip(blocks, o_refs):
        block = y[:, n * D_MODEL:(n + 1) * D_MODEL].astype(o_ref.dtype)
        o_ref[...] = block.T if transposed else block


def _mod_linear(x, mods, layer, j_scale, j_shift, w, outs, scale=1.0):
    n = x.shape[0]
    tm, batch_of = _row_tiling(mods.mode)
    row = pl.BlockSpec((tm, D_MODEL), lambda i: (i, 0))
    tiles_per_batch = SEQ // tm
    col = pl.BlockSpec((None, D_MODEL, tm), lambda i: (i // tiles_per_batch, 0, i % tiles_per_batch))
    return pl.pallas_call(
        functools.partial(_mod_linear_kernel, scale=scale,
                          blocks=tuple((o[0], o[2]) for o in outs)),
        grid=(n // tm,),
        in_specs=[row, mods.spec(layer, j_scale, batch_of), mods.spec(layer, j_shift, batch_of),
                  _const_spec(w.shape)],
        out_specs=[col if o[2] else row for o in outs],
        out_shape=[jax.ShapeDtypeStruct((BATCH, D_MODEL, SEQ) if o[2] else (n, D_MODEL), o[1])
                   for o in outs],
        compiler_params=_params(("arbitrary",)),
        name="mod_linear",
    )(x, mods.arr, mods.arr, w)


def _block_tail_kernel(a_ref, x_ref, gm_ref, sc_ref, sh_ref, gf_ref, wm_ref, w1_ref, w2_ref,
                       g_ref, b_ref, o_ref, *, glu):
    y = jnp.dot(a_ref[...], wm_ref[...], preferred_element_type=F32)
    if glu:
        y = y[:, :D_MODEL] * jax.nn.sigmoid(y[:, D_MODEL:])
    x = _layer_norm(ALPHA * x_ref[...] + (1.0 + gm_ref[...]) * y, g_ref[0:1, :], b_ref[0:1, :])
    h = (x * (1.0 + sc_ref[...]) + sh_ref[...]).astype(BF16)
    acc = jnp.zeros(x.shape, F32)
    for c in range(D_FF // FF_CHUNK):
        cols = slice(c * FF_CHUNK, (c + 1) * FF_CHUNK)
        hid = jnp.dot(h, w1_ref[:, cols], preferred_element_type=F32)
        hid = jnp.square(jnp.maximum(hid, 0.0)).astype(BF16)
        acc = acc + jnp.dot(hid, w2_ref[cols, :], preferred_element_type=F32)
    z = ALPHA * x + (1.0 + gf_ref[...]) * acc
    o_ref[...] = _layer_norm(z, g_ref[1:2, :], b_ref[1:2, :])


def _block_tail(a, x, mods, layer, w_mix, w1, w2, ln_g, ln_b, glu):
    n = x.shape[0]
    tm, batch_of = _row_tiling(mods.mode)
    row = pl.BlockSpec((tm, D_MODEL), lambda i: (i, 0))
    ln = pl.BlockSpec((None, 2, D_MODEL), lambda i: (layer, 0, 0))
    weight = lambda w: pl.BlockSpec((None,) + w.shape[1:], lambda i: (layer % w.shape[0], 0, 0),
                                    pipeline_mode=pl.Buffered(1))
    return pl.pallas_call(
        functools.partial(_block_tail_kernel, glu=glu),
        grid=(n // tm,),
        in_specs=[row, row] + [mods.spec(layer, j, batch_of) for j in (2, 4, 3, 5)]
                 + [weight(w_mix), weight(w1), weight(w2), ln, ln],
        out_specs=row,
        out_shape=jax.ShapeDtypeStruct((n, D_MODEL), F32),
        compiler_params=_params(("arbitrary",)),
        name="block_tail",
    )(a, x, mods.arr, mods.arr, mods.arr, mods.arr, w_mix, w1, w2, ln_g, ln_b)


def _cmul(ar, ai, br, bi):
    return ar * br - ai * bi, ar * bi + ai * br


def _ssm_prep(lam_re, lam_im, log_dt, b_re, b_im, c_re, c_im):
    n_l = lam_re.shape[0]
    dt = jnp.exp(log_dt)[..., None]
    ns = list(range(TILE + 1)) + [TILE * m for m in range(2, SUBLANES + 1)]
    n = jnp.asarray(ns, F32).reshape(-1, 1, 1, 1)
    mag = jnp.exp(n * (lam_re * dt))
    pr = mag * jnp.cos(n * (lam_im * dt))
    pi = mag * jnp.sin(n * (lam_im * dt))
    ar, ai = pr[1], pi[1]
    den = lam_re * lam_re + lam_im * lam_im
    er = ((ar - 1.0) * lam_re + ai * lam_im) / den
    ei = (ai * lam_re - (ar - 1.0) * lam_im) / den
    bbr = er[..., None] * b_re - ei[..., None] * b_im
    bbi = er[..., None] * b_im + ei[..., None] * b_re
    gpc = CHUNK // SSM_GROUP
    exact = lax.Precision.HIGHEST

    def block_diag(m, rows_per_group, cols_per_group, steps=1):
        y = steps * cols_per_group
        c = jnp.arange(steps * gpc * cols_per_group)
        step, group, inner = c // (gpc * cols_per_group), (c // cols_per_group) % gpc, c % cols_per_group
        select = (jnp.arange(y)[:, None] == (step * cols_per_group + inner)[None, :]).astype(F32)
        wide = jnp.einsum("...ry,yc->...rc", m, select, precision=exact)
        row_group = (jnp.arange(m.shape[-2]) // rows_per_group) % gpc
        return jnp.where(row_group[:, None] == group[None, :], wide, 0.0).astype(BF16)

    def per_chunk(m, order):
        x, _, _, a, b = m.shape
        m = m.reshape(x, n_l, N_CHUNK, gpc, a, b)
        return jnp.transpose(m, (1, 2) + tuple({"x": 0, "g": 3, "a": 4, "b": 5}[o] for o in order))

    swap = lambda m: jnp.swapaxes(m, -1, -2)
    b_in = per_chunk(jnp.concatenate([swap(bbr), swap(bbi)], axis=-1)[None], "gaxb")
    bcat = block_diag(b_in.reshape(n_l, N_CHUNK, CHUNK, 2 * SSM_STATE), SSM_GROUP, SSM_STATE,
                      steps=2)
    c_out = per_chunk(jnp.stack([swap(c_re), -swap(c_im)]), "xgab")
    ccat = block_diag(c_out.reshape(n_l, N_CHUNK, 2 * CHUNK_STATE, SSM_GROUP), SSM_STATE, SSM_GROUP)

    abr, abi = _cmul(pr[:TILE, ..., None], pi[:TILE, ..., None], bbr, bbi)
    dot_p = functools.partial(jnp.einsum, "lghp,jlgpi->jlgih", precision=exact)
    taps = per_chunk(dot_p(c_re, abr) - dot_p(c_im, abi), "xgab")
    kcat = block_diag(taps.reshape(n_l, N_CHUNK, TILE * CHUNK, SSM_GROUP), SSM_GROUP, SSM_GROUP)

    ends = per_chunk(jnp.concatenate([swap(abr[::-1]), swap(abi[::-1])], axis=-1), "xgab")
    wst = block_diag(ends.reshape(n_l, N_CHUNK, TILE * CHUNK, 2 * SSM_STATE), SSM_GROUP, SSM_STATE,
                     steps=2)

    mr, mi = _cmul(pr[1:TILE + 1, :, :, None, :], pi[1:TILE + 1, :, :, None, :], c_re, c_im)
    outs = jnp.stack([per_chunk(m, "gbxa") for m in (mr, -mi)], axis=2)
    call = block_diag(outs.reshape(n_l, N_CHUNK, 2 * CHUNK_STATE, TILE * SSM_GROUP), SSM_STATE,
                      SSM_GROUP, steps=TILE)

    flat = lambda a: jnp.moveaxis(a, 0, 1).reshape(n_l, -1, N_STATE)
    rows = lax.broadcasted_iota(jnp.int32, (SUBLANES, N_STATE), 0)
    parts = []
    for a in (flat(pr[TILE:]), flat(pi[TILE:])):
        masked = lambda m, first: jnp.where(rows >= first, a[:, m - 1:m, :], 0.0)
        parts.append([masked(1, 1), masked(2, 2), masked(4, 4), a, masked(SUBLANES, 0)])
    tab = jnp.stack([p for pair in zip(*parts) for p in pair], axis=1)
    atab = jnp.stack([jnp.broadcast_to(flat(a[1:2]), (n_l, SUBLANES, N_STATE)) for a in (pr, pi)],
                     axis=1)
    return dict(bcat=bcat, ccat=ccat, kcat=kcat, wst=wst, call=call, tab=tab, atab=atab)


def _ssm_prompt_kernel(x_ref, sc_ref, sh_ref, kcat_ref, wst_ref, call_ref, d_ref, tab_ref,
                       g_ref, hre_ref, him_ref, h_ref, buf_ref, y_ref):
    n_tiles = SEQ // TILE
    half = CHUNK_STATE
    h_ref[...] = x_ref[...] * (1.0 + sc_ref[...]) + sh_ref[...]

    steps = [h_ref[pl.ds(r, n_tiles, stride=TILE), :].astype(BF16) for r in range(TILE)]
    buf_ref[...] = jnp.dot(jnp.concatenate(steps, axis=1), wst_ref[...],
                           preferred_element_type=F32)

    row = lax.broadcasted_iota(jnp.int32, (SUBLANES, half), 0)

    def tile_scan(i, carry):
        cr, ci = carry
        r0 = pl.multiple_of(i * SUBLANES, SUBLANES)
        vr = buf_ref[pl.ds(r0, SUBLANES), 0:half]
        vi = buf_ref[pl.ds(r0, SUBLANES), half:2 * half]
        for lvl, shift in enumerate((1, 2, 4)):
            pr, pi = _cmul(tab_ref[2 * lvl], tab_ref[2 * lvl + 1],
                           pltpu.roll(vr, shift, 0), pltpu.roll(vi, shift, 0))
            vr, vi = vr + pr, vi + pi
        pr, pi = _cmul(tab_ref[6], tab_ref[7], cr, ci)
        buf_ref[pl.ds(r0, SUBLANES), 0:half] = jnp.where(row == 0, cr, pltpu.roll(vr + pr, 1, 0))
        buf_ref[pl.ds(r0, SUBLANES), half:2 * half] = jnp.where(row == 0, ci,
                                                                 pltpu.roll(vi + pi, 1, 0))
        pr, pi = _cmul(tab_ref[8], tab_ref[9], cr, ci)
        last = SUBLANES - 1
        return (jnp.broadcast_to(vr[last:, :], vr.shape) + pr,
                jnp.broadcast_to(vi[last:, :], vi.shape) + pi)

    zero = jnp.zeros((SUBLANES, half), F32)
    cr, ci = lax.fori_loop(0, n_tiles // SUBLANES, tile_scan, (zero, zero))
    hre_ref[...] = cr[0:1, :]
    him_ref[...] = ci[0:1, :]

    y_state = jnp.dot(buf_ref[...].astype(BF16), call_ref[...], preferred_element_type=F32)
    for s in range(TILE):
        y_ref[pl.ds(s, n_tiles, stride=TILE), :] = y_state[:, s * CHUNK:(s + 1) * CHUNK]

    step_in_tile = lax.broadcasted_iota(jnp.int32, (T_SSM, CHUNK), 0) % TILE
    for blk in range(SEQ // T_SSM):
        rows = slice(blk * T_SSM, (blk + 1) * T_SSM)
        h = h_ref[rows, :]
        lags = [h.astype(BF16)] + [
            jnp.where(step_in_tile >= j, pltpu.roll(h, j, 0), 0.0).astype(BF16)
            for j in range(1, TILE)]
        y = jnp.dot(jnp.concatenate(lags, axis=1), kcat_ref[...], preferred_element_type=F32)
        y = y + y_ref[rows, :] + d_ref[...] * h
        g_ref[rows, :] = jax.nn.gelu(y).astype(g_ref.dtype)


def _ssm_prompt(x, mods, layer, p, d_skip):
    col = pl.BlockSpec((SEQ, CHUNK), lambda k, b: (b, k))
    fin = pl.BlockSpec((None, 1, CHUNK_STATE), lambda k, b: (b, 0, k))
    mod = lambda j: pl.BlockSpec((None, None, 1, CHUNK),
                                 lambda k, b: (layer, DEC_BATCH + b, 0, j * N_CHUNK + k))
    weight = lambda a: pl.BlockSpec((None, None) + a.shape[2:], lambda k, b: (layer, k, 0, 0))
    g, hre, him = pl.pallas_call(
        _ssm_prompt_kernel,
        grid=(N_CHUNK, BATCH),
        in_specs=[col, mod(1), mod(0), weight(p["kcat"]), weight(p["wst"]), weight(p["call"]),
                  pl.BlockSpec((None, 1, CHUNK), lambda k, b: (layer, 0, k)),
                  pl.BlockSpec((None, p["tab"].shape[1], SUBLANES, CHUNK_STATE),
                               lambda k, b: (layer, 0, 0, k))],
        out_specs=[col, fin, fin],
        out_shape=[jax.ShapeDtypeStruct((BATCH * SEQ, D_MODEL), BF16),
                   jax.ShapeDtypeStruct((BATCH, 1, N_STATE), F32),
                   jax.ShapeDtypeStruct((BATCH, 1, N_STATE), F32)],
        scratch_shapes=[pltpu.VMEM((SEQ, CHUNK), F32),
                        pltpu.VMEM((SEQ // TILE, 2 * CHUNK_STATE), F32),
                        pltpu.VMEM((SEQ, CHUNK), F32)],
        compiler_params=_params(("arbitrary", "arbitrary")),
        name="ssm_prompt",
    )(x, mods.arr, mods.arr, p["kcat"], p["wst"], p["call"], d_skip[:, None, :], p["tab"])
    shape = (BATCH, SSM_GROUPS, SSM_STATE)
    return g, hre.reshape(shape), him.reshape(shape)


def _ssm_sample_kernel(x_ref, sc_ref, sh_ref, bcat_ref, ccat_ref, d_ref, a_ref, h0r_ref, h0i_ref,
                       g_ref, hre_ref, him_ref):
    tile = lambda m: jnp.concatenate([m] * DEC_SEQ, axis=0)
    h = x_ref[...] * (1.0 + tile(sc_ref[...])) + tile(sh_ref[...])
    bu = jnp.dot(h.astype(BF16), bcat_ref[...], preferred_element_type=F32)
    ar = a_ref[0, 0:1, :]
    ai = a_ref[1, 0:1, :]
    sr = h0r_ref[...].T
    si = h0i_ref[...].T
    states = []
    for t in range(DEC_SEQ):
        rows = slice(t * DEC_BATCH, (t + 1) * DEC_BATCH)
        pr, pi = _cmul(ar, ai, sr, si)
        sr = pr + bu[rows, :CHUNK_STATE]
        si = pi + bu[rows, CHUNK_STATE:]
        states.append(jnp.concatenate([sr, si], axis=1))
    xs = jnp.concatenate(states, axis=0).astype(BF16)
    y = jnp.dot(xs, ccat_ref[...], preferred_element_type=F32) + d_ref[...] * h
    g_ref[...] = jax.nn.gelu(y).astype(g_ref.dtype)
    hre_ref[...] = sr.T
    him_ref[...] = si.T


def _ssm_sample(x, mods, layer, p, d_skip, h0_re, h0_im):
    n = DEC_BATCH * DEC_SEQ
    col = pl.BlockSpec((n, CHUNK), lambda k: (0, k))
    st = pl.BlockSpec((None, CHUNK_STATE, DEC_BATCH), lambda k: (layer, k, 0))
    mod = lambda j: pl.BlockSpec((None, DEC_BATCH, CHUNK), lambda k: (layer, 0, j * N_CHUNK + k))
    state_major = lambda a: jnp.transpose(a, (0, 2, 3, 1)).reshape(-1, N_STATE, DEC_BATCH)
    g, hre, him = pl.pallas_call(
        _ssm_sample_kernel,
        grid=(N_CHUNK,),
        in_specs=[col, mod(1), mod(0),
                  pl.BlockSpec((None, None, CHUNK, 2 * CHUNK_STATE), lambda k: (layer, k, 0, 0)),
                  pl.BlockSpec((None, None, 2 * CHUNK_STATE, CHUNK), lambda k: (layer, k, 0, 0)),
                  pl.BlockSpec((None, 1, CHUNK), lambda k: (layer, 0, k)),
                  pl.BlockSpec((None, 2, SUBLANES, CHUNK_STATE), lambda k: (layer, 0, 0, k)),
                  st, st],
        out_specs=[col] + [pl.BlockSpec((CHUNK_STATE, DEC_BATCH), lambda k: (k, 0))] * 2,
        out_shape=[jax.ShapeDtypeStruct((n, D_MODEL), BF16),
                   jax.ShapeDtypeStruct((N_STATE, DEC_BATCH), F32),
                   jax.ShapeDtypeStruct((N_STATE, DEC_BATCH), F32)],
        compiler_params=_params(("arbitrary",)),
        name="ssm_sample",
    )(x, mods.arr, mods.arr, p["bcat"], p["ccat"], d_skip[:, None, :], p["atab"],
      state_major(h0_re), state_major(h0_im))
    batch_major = lambda a: jnp.transpose(a.reshape(SSM_GROUPS, SSM_STATE, DEC_BATCH), (2, 0, 1))
    return g, batch_major(hre), batch_major(him)


def _rel_bucket(rel):
    n = jnp.maximum(rel, 0)
    max_exact = NUM_BUCKETS // 2
    large = max_exact + (jnp.log(jnp.maximum(n, 1).astype(F32) / max_exact)
                         / math.log(MAX_DISTANCE / max_exact) * (NUM_BUCKETS - max_exact)).astype(jnp.int32)
    large = jnp.minimum(large, NUM_BUCKETS - 1)
    return jnp.where(n < max_exact, n, large)


def _attn_scalars(attn_lam, subln_g, layer):
    lam_init = 0.8 - 0.6 * math.exp(-0.3 * layer)
    lam = (jnp.exp(jnp.sum(attn_lam[0] * attn_lam[1])) - jnp.exp(jnp.sum(attn_lam[2] * attn_lam[3]))
           + lam_init)
    lam_tile = jnp.full((SUBLANES, LANES), lam, F32)
    gain = (subln_g * (1.0 - lam_init)).reshape(1, V_DIM)
    return lam_tile, gain


def _bias_by_distance(rel_bias, n):
    bv = rel_bias[_rel_bucket(jnp.arange(n, dtype=jnp.int32))]
    return ((bv - rel_bias[NUM_BUCKETS - 1]) * LOG2E).T


def _toeplitz(w, n):
    u = jnp.concatenate([w[::-1], w[:1]])
    r = jnp.tile(u, n)[:n * (2 * n - 1)].reshape(n, 2 * n - 1)
    return r[:, n - 1:]


def _softmax_update(s, m_prev):
    m_new = jnp.maximum(m_prev, jnp.max(s, axis=1, keepdims=True))
    p = jnp.exp2(s - jnp.concatenate([m_new] * (s.shape[1] // LANES), axis=1))
    return m_new, p.astype(BF16), jnp.exp2(m_prev - m_new)


def _with_ones(v):
    return jnp.concatenate([v, jnp.ones((v.shape[0], V_DIM), BF16)], axis=1)


def _acc_update(acc, alpha, p, vx):
    return (jnp.concatenate([alpha, alpha], axis=1) * acc
            + jnp.dot(p, vx, preferred_element_type=F32))


_NT = (((1,), (1,)), ((), ()))


def _attn_prompt_kernel(q_ref, k_ref, v_ref, tab_ref, lam_ref, gain_ref, o_ref,
                        q2_ref, m_ref, acc_ref, s_ref):
    i = pl.program_id(2)
    for hh in range(HEADS_PER_STEP):
        q = q_ref[:, hh * V_DIM:(hh + 1) * V_DIM]
        lane = lax.broadcasted_iota(jnp.int32, q.shape, 1)
        zero = jnp.zeros_like(q)
        q2_ref[hh, 0:TQ, :] = jnp.where(lane < HEAD_DIM, q, zero)
        q2_ref[hh, TQ:2 * TQ, :] = jnp.where(lane >= HEAD_DIM, q, zero)
    m_ref[...] = jnp.full(m_ref.shape, NEG_INF, F32)
    acc_ref[...] = jnp.zeros_like(acc_ref)

    def key_rows(j):
        return pl.ds(pl.multiple_of(j * TQ, TQ), TQ)

    def logits(hh, j):
        cols = slice(hh * V_DIM, (hh + 1) * V_DIM)
        return lax.dot_general(q2_ref[hh], k_ref[key_rows(j), cols].astype(BF16), _NT,
                               preferred_element_type=F32)

    def consume(hh, j, s, table):
        cols = slice(hh * V_DIM, (hh + 1) * V_DIM)
        if table is not None:
            bias = tab_ref[hh, table]
            s = s + jnp.concatenate([bias, bias], axis=0)
        m_new, p, alpha = _softmax_update(s, m_ref[hh])
        acc_ref[hh] = _acc_update(acc_ref[hh], alpha, p,
                                  _with_ones(v_ref[key_rows(j), cols].astype(BF16)))
        m_ref[hh] = m_new

    def step(j, table, more):
        consume(0, j, s_ref[...], table)
        for hh in range(1, HEADS_PER_STEP):
            consume(hh, j, logits(hh, j), table)
        if more:
            s_ref[...] = logits(0, j + 1)

    s_ref[...] = logits(0, 0)

    def far(j, c):
        step(j, None, True)
        return c

    lax.fori_loop(0, jnp.maximum(i - 1, 0), far, 0)

    @pl.when(i >= 1)
    def _():
        step(i - 1, 1, True)

    step(i, 0, False)

    for hh in range(HEADS_PER_STEP):
        acc = acc_ref[hh]
        o = acc[:, :V_DIM] / acc[:, V_DIM:]
        o = o[0:TQ, :] - lam_ref[0:1, 0:1] * o[TQ:2 * TQ, :]
        o = o * lax.rsqrt(jnp.mean(o * o, axis=-1, keepdims=True) + LN_EPS) * gain_ref[...]
        o_ref[:, hh * V_DIM:(hh + 1) * V_DIM] = o.astype(o_ref.dtype)


def _attn_prompt_tables(rel_bias):
    bv = _bias_by_distance(rel_bias, 2 * TQ)
    masked = jnp.concatenate([jnp.full((N_HEADS, TQ - 1), NEG_INF, F32), bv[:, :TQ]], axis=1)
    toeplitz = jax.vmap(lambda w: _toeplitz(w, TQ))
    return jnp.stack([toeplitz(masked), toeplitz(bv[:, 1:])], axis=1)


def _attn_prompt(q, k, v, tab, lam_tile, gain):
    n_q = SEQ // TQ
    width = HEADS_PER_STEP * V_DIM
    shape3 = (BATCH, SEQ, D_MODEL)
    qspec = pl.BlockSpec((None, TQ, width), lambda b, h, i: (b, i, h))
    kvspec = pl.BlockSpec((None, SEQ, width), lambda b, h, i: (b, 0, h))
    o = pl.pallas_call(
        _attn_prompt_kernel,
        grid=(BATCH, N_HEADS // HEADS_PER_STEP, n_q),
        in_specs=[qspec, kvspec, kvspec,
                  pl.BlockSpec((HEADS_PER_STEP, 2, TQ, TQ), lambda b, h, i: (h, 0, 0, 0)),
                  pl.BlockSpec((SUBLANES, LANES), lambda b, h, i: (0, 0)),
                  pl.BlockSpec((1, V_DIM), lambda b, h, i: (0, 0))],
        out_specs=qspec,
        out_shape=jax.ShapeDtypeStruct(shape3, BF16),
        scratch_shapes=[pltpu.VMEM((HEADS_PER_STEP, 2 * TQ, V_DIM), BF16),
                        pltpu.VMEM((HEADS_PER_STEP, 2 * TQ, LANES), F32),
                        pltpu.VMEM((HEADS_PER_STEP, 2 * TQ, 2 * V_DIM), F32),
                        pltpu.VMEM((2 * TQ, TQ), F32)],
        compiler_params=_params(("arbitrary", "arbitrary", "arbitrary")),
        name="attn_prompt",
    )(q.reshape(shape3), k.reshape(shape3), v.reshape(shape3), tab, lam_tile, gain)
    return o.reshape(BATCH * SEQ, D_MODEL)


N_SROWS = N_HEADS * 2 * DEC_SEQ


def _attn_sample_kernel(pt_ref, q_ref, qmask_ref, knew_ref, vnew_ref, bias_ref, bnew_ref,
                        lam_ref, gain_ref, *rest):
    del pt_ref
    k_refs = rest[:PAGES_PER_STEP]
    v_refs = rest[PAGES_PER_STEP:2 * PAGES_PER_STEP]
    o_ref, qx_ref, m_ref, acc_ref = rest[2 * PAGES_PER_STEP:]
    step = pl.program_id(1)
    rows_per_head = 2 * DEC_SEQ

    @pl.when(step == 0)
    def _():
        q = q_ref[...].astype(F32)
        q = jnp.concatenate([q, q], axis=0)
        qx_ref[...] = (jnp.concatenate([q] * N_HEADS, axis=0) * qmask_ref[...]).astype(BF16)
        m_ref[...] = jnp.full(m_ref.shape, NEG_INF, F32)
        acc_ref[...] = jnp.zeros_like(acc_ref)

    def update(s, values_of_head):
        m_new, p, alpha = _softmax_update(s, m_ref[...])
        for h in range(N_HEADS):
            r = slice(h * rows_per_head, (h + 1) * rows_per_head)
            acc_ref[r, :] = _acc_update(acc_ref[r, :], alpha[r, :], p[r, :],
                                        _with_ones(values_of_head(h)))
        m_ref[...] = m_new

    qx = qx_ref[...]
    kt = jnp.concatenate([r[...].astype(BF16) for r in k_refs], axis=1)
    s = jnp.dot(qx, kt, preferred_element_type=F32) + bias_ref[...]
    update(s, lambda h: jnp.concatenate(
        [r[pl.ds(h, PAGE_SIZE, stride=N_HEADS), :].astype(BF16) for r in v_refs], axis=0))

    @pl.when(step == pl.num_programs(1) - 1)
    def _():
        pad = jnp.zeros((LANES - SUBLANES, D_MODEL), BF16)
        kn = jnp.concatenate([knew_ref[...], pad], axis=0)
        vn = jnp.concatenate([vnew_ref[...], pad], axis=0)
        sn = lax.dot_general(qx, kn, _NT, preferred_element_type=F32) + bnew_ref[...]
        update(sn, lambda h: vn[:, h * V_DIM:(h + 1) * V_DIM])
        acc = acc_ref[...]
        o = acc[:, :V_DIM] / acc[:, V_DIM:]
        heads = []
        for h in range(N_HEADS):
            r0 = h * rows_per_head
            oh = o[r0:r0 + DEC_SEQ, :] - lam_ref[0:1, 0:1] * o[r0 + DEC_SEQ:r0 + rows_per_head, :]
            oh = oh * lax.rsqrt(jnp.mean(oh * oh, axis=-1, keepdims=True) + LN_EPS) * gain_ref[...]
            heads.append(oh)
        o_ref[...] = jnp.concatenate(heads, axis=1)


def _attn_sample_tables(rel_bias):
    bv = _bias_by_distance(rel_bias, PAST_LEN + DEC_SEQ)
    past = jnp.stack([bv[:, t + 1:t + 1 + PAST_LEN][:, ::-1] for t in range(DEC_SEQ)], axis=1)
    rel = (jnp.arange(DEC_SEQ, dtype=jnp.int32)[:, None]
           - jnp.arange(LANES, dtype=jnp.int32)[None, :])
    new = jnp.where(rel >= 0, bv[:, jnp.clip(rel, 0, DEC_SEQ - 1)], NEG_INF)
    rows = lambda a: jnp.broadcast_to(a[:, None], (N_HEADS, 2) + a.shape[1:]).reshape(
        N_SROWS, a.shape[-1])
    col = jnp.arange(D_MODEL, dtype=jnp.int32)[None, :]
    row = jnp.arange(N_SROWS, dtype=jnp.int32)[:, None]
    qmask = ((col // V_DIM == row // (2 * DEC_SEQ))
             & ((col % V_DIM) // HEAD_DIM == (row // DEC_SEQ) % 2)).astype(F32)
    return rows(past), rows(new), qmask


def _attn_sample(q, k_new, v_new, cache_k, cache_v, page_table, tables, lam_tile, gain):
    past, new, qmask = tables
    n_phys = cache_k.shape[0]
    ck = jnp.transpose(cache_k, (0, 2, 3, 4, 1)).reshape(n_phys, D_MODEL, PAGE_SIZE)
    cv = cache_v.reshape(n_phys, PAGE_SIZE * N_HEADS, V_DIM)
    n_steps = N_PAGES // PAGES_PER_STEP
    keys_per_step = PAGES_PER_STEP * PAGE_SIZE

    def page_spec(r):
        return pl.BlockSpec((None, D_MODEL, PAGE_SIZE),
                            lambda b, s, pt: (pt[b * N_PAGES + s * PAGES_PER_STEP + r], 0, 0))

    per_seq = lambda rows: pl.BlockSpec((None, rows, D_MODEL), lambda b, s, pt: (b, 0, 0))
    const = lambda shape: pl.BlockSpec(shape, lambda b, s, pt: (0,) * len(shape))
    grid_spec = pltpu.PrefetchScalarGridSpec(
        num_scalar_prefetch=1,
        grid=(DEC_BATCH, n_steps),
        in_specs=[per_seq(DEC_SEQ), const(qmask.shape), per_seq(SUBLANES), per_seq(SUBLANES),
                  pl.BlockSpec((N_SROWS, keys_per_step), lambda b, s, pt: (0, s)),
                  const(new.shape), const(lam_tile.shape), const(gain.shape)]
                 + [page_spec(r) for r in range(PAGES_PER_STEP)] * 2,
        out_specs=per_seq(DEC_SEQ),
        scratch_shapes=[pltpu.VMEM((N_SROWS, D_MODEL), BF16), pltpu.VMEM((N_SROWS, LANES), F32),
                        pltpu.VMEM((N_SROWS, 2 * V_DIM), F32)],
    )
    return pl.pallas_call(
        _attn_sample_kernel,
        grid_spec=grid_spec,
        out_shape=jax.ShapeDtypeStruct((DEC_BATCH, DEC_SEQ, D_MODEL), F32),
        compiler_params=_params(("arbitrary", "arbitrary")),
        name="attn_sample",
    )(page_table.reshape(-1), q, qmask, k_new, v_new, past, new, lam_tile, gain,
      *([ck] * PAGES_PER_STEP), *([cv] * PAGES_PER_STEP))


def _to_batch_major(a):
    return jnp.transpose(a.reshape(DEC_SEQ, DEC_BATCH, D_MODEL), (1, 0, 2))


def _to_time_major(a):
    return jnp.transpose(a, (1, 0, 2)).reshape(DEC_SEQ * DEC_BATCH, D_MODEL)


def _trunk(x, mods, mods_kv, h0_re, h0_im, attend, w):
    ssm_re, ssm_im = [], []
    k = v = None
    for layer in range(DEPTH):
        if layer < N_A_LAYERS:
            if mods.mode == "p":
                g, fr, fi = _ssm_prompt(x, mods, layer, w["ssm"], w["ssm_d"])
            else:
                g, fr, fi = _ssm_sample(x, mods, layer, w["ssm"], w["ssm_d"], h0_re, h0_im)
            ssm_re.append(fr)
            ssm_im.append(fi)
            mixed, w_mix = g, w["glu"]
        else:
            j = layer - N_A_LAYERS
            (q,) = _mod_linear(x, mods, layer, 1, 0, w["wq"][j], [(0, BF16, False)],
                               scale=HEAD_DIM ** -0.5 * LOG2E)
            mixed, w_mix = attend(q, k, v, j), w["wo"]
        x = _block_tail(mixed, x, mods, layer, w_mix, w["w1"], w["w2"], w["ln_g"], w["ln_b"],
                        glu=layer < N_A_LAYERS)
        if layer == N_A_LAYERS - 1:
            outs = [(0, F32, False), (1, F32, False)] + [(0, F32, True)] * (mods.mode == "p")
            k, v, *k_out = _mod_linear(x, mods_kv, 0, 1, 0, w["wkv"], outs)
    return x, jnp.stack(ssm_re), jnp.stack(ssm_im), (k_out or [k])[0], v


def kernel(x_prompt, x_sample, state_ssm_re, state_ssm_im, cache_k, cache_v, page_table, c_prompt, c_sample, rel_bias, w_ada, b_ada, ln_g, ln_b, ssm_lam_re, ssm_lam_im, ssm_log_dt, ssm_b_re, ssm_b_im, ssm_c_re, ssm_c_im, ssm_d, ssm_w_glu_a, ssm_w_glu_b, w_ada_kv, b_ada_kv, w_kv, attn_w_q, attn_lam, attn_subln_g, attn_w_o, mlp_w1, mlp_w2):
    w = {
        "ssm": _ssm_prep(ssm_lam_re, ssm_lam_im, ssm_log_dt, ssm_b_re, ssm_b_im, ssm_c_re, ssm_c_im),
        "ssm_d": ssm_d,
        "glu": jnp.concatenate([ssm_w_glu_a, ssm_w_glu_b], axis=-1).astype(BF16),
        "wq": attn_w_q.astype(BF16),
        "wo": attn_w_o.astype(BF16),
        "wkv": w_kv.astype(BF16),
        "w1": mlp_w1.astype(BF16),
        "w2": mlp_w2.astype(BF16),
        "ln_g": ln_g,
        "ln_b": ln_b,
    }
    c_all = jnp.concatenate([c_sample, c_prompt, jnp.zeros((BATCH, D_MODEL), F32)], axis=0)
    ada = _ada(c_all, w_ada, b_ada)
    ada_kv = _ada(c_all, w_ada_kv[None], b_ada_kv[None])
    scalars = [_attn_scalars(attn_lam[j], attn_subln_g[j], N_A_LAYERS + j)
               for j in range(DEPTH - N_A_LAYERS)]

    prompt_tab = _attn_prompt_tables(rel_bias)

    def attend_prompt(q, k, v, j):
        return _attn_prompt(q, k, v, prompt_tab, *scalars[j])

    y_p, re_p, im_p, k_p, v_p = _trunk(x_prompt.reshape(BATCH * SEQ, D_MODEL), _Mods(ada, "p"),
                                       _Mods(ada_kv, "p"), None, None, attend_prompt, w)

    sample_tab = _attn_sample_tables(rel_bias)

    def attend_sample(q, k, v, j):
        pad = lambda a: jnp.pad(_to_batch_major(a).astype(BF16),
                                ((0, 0), (0, SUBLANES - DEC_SEQ), (0, 0)))
        o = _attn_sample(_to_batch_major(q), pad(k), pad(v), cache_k, cache_v, page_table,
                         sample_tab, *scalars[j])
        return _to_time_major(o).astype(BF16)

    y_s, re_s, im_s, k_s, v_s = _trunk(_to_time_major(x_sample), _Mods(ada, "s"), _Mods(ada_kv, "s"),
                                       state_ssm_re, state_ssm_im, attend_sample, w)

    return (y_p.reshape(BATCH, SEQ, D_MODEL), _to_batch_major(y_s), re_p, im_p,
            jnp.transpose(k_p.reshape(BATCH, N_HEADS, 2, HEAD_DIM, SEQ), (0, 4, 1, 2, 3)),
            v_p.reshape(BATCH, SEQ, N_HEADS, V_DIM),
            re_s, im_s,
            _to_batch_major(k_s).reshape(DEC_BATCH, DEC_SEQ, N_HEADS, 2, HEAD_DIM),
            _to_batch_major(v_s).reshape(DEC_BATCH, DEC_SEQ, N_HEADS, V_DIM))
```

```python
import functools
import math

import jax
import jax.numpy as jnp
from jax import lax
from jax.experimental import pallas as pl
from jax.experimental.pallas import tpu as pltpu

F32 = jnp.float32
BF16 = jnp.bfloat16

D_MODEL = 1024
BATCH = 4
SEQ = 4096
DEPTH = 4
DEC_BATCH = 128
DEC_SEQ = 4
PAST_LEN = 2048
PAGE_SIZE = 128
N_PAGES = PAST_LEN // PAGE_SIZE
N_A_LAYERS = DEPTH // 2
SSM_GROUP = 16
SSM_GROUPS = D_MODEL // SSM_GROUP
SSM_STATE = 64
N_STATE = SSM_GROUPS * SSM_STATE
N_HEADS = 8
HEAD_DIM = D_MODEL // (2 * N_HEADS)
V_DIM = 2 * HEAD_DIM
D_FF = 4 * D_MODEL
NUM_BUCKETS = 32
MAX_DISTANCE = 128
N_MOD = 6
ALPHA = (2.0 * DEPTH) ** 0.25
LN_EPS = 1e-5
NEG_INF = -1e30

SUBLANES = 8
LANES = 128
VMEM_LIMIT = 48 * 1024 * 1024

MOD_ROWS = DEC_BATCH + 2 * BATCH
CHUNK = LANES
N_CHUNK = D_MODEL // CHUNK
CHUNK_STATE = N_STATE // N_CHUNK
TM_PROMPT = 512
TILE = SUBLANES
T_SSM = 512
TQ = 512
HEADS_PER_STEP = 2
PAGES_PER_STEP = 16
LOG2E = math.log2(math.e)
FF_CHUNK = 1024


def _params(sem, vmem=VMEM_LIMIT):
    return pltpu.CompilerParams(dimension_semantics=sem, vmem_limit_bytes=vmem)


def _layer_norm(z, g, b):
    mu = jnp.mean(z, axis=-1, keepdims=True)
    zc = z - mu
    var = jnp.mean(zc * zc, axis=-1, keepdims=True)
    return zc * lax.rsqrt(var + LN_EPS) * g + b


def _ada_kernel(c_ref, w_ref, b_ref, o_ref):
    c = c_ref[...]
    sc = (c * jax.nn.sigmoid(c)).astype(BF16)
    w = w_ref[...].astype(BF16)
    o_ref[...] = jnp.dot(sc, w, preferred_element_type=F32) + b_ref[...]


def _ada(c_all, w, b):
    n_l, _, width = w.shape
    tn = 1024
    return pl.pallas_call(
        _ada_kernel,
        grid=(n_l, width // tn),
        in_specs=[
            pl.BlockSpec((MOD_ROWS, D_MODEL), lambda l, j: (0, 0)),
            pl.BlockSpec((None, D_MODEL, tn), lambda l, j: (l, 0, j)),
            pl.BlockSpec((None, 1, tn), lambda l, j: (l, 0, j)),
        ],
        out_specs=pl.BlockSpec((None, MOD_ROWS, tn), lambda l, j: (l, 0, j)),
        out_shape=jax.ShapeDtypeStruct((n_l, MOD_ROWS, width), F32),
        compiler_params=_params(("arbitrary", "arbitrary")),
        name="ada_mod",
    )(c_all, w, b.reshape(n_l, 1, width))


class _Mods:
    def __init__(self, arr, mode):
        self.mode = mode
        self.arr = arr if mode == "s" else arr.reshape(arr.shape[0], MOD_ROWS, 1, arr.shape[-1])

    def spec(self, layer, j, batch_of):
        if self.mode == "s":
            return pl.BlockSpec((None, DEC_BATCH, D_MODEL), lambda *g: (layer, 0, j))
        return pl.BlockSpec((None, None, 1, D_MODEL),
                            lambda *g: (layer, DEC_BATCH + batch_of(*g), 0, j))


def _row_tiling(mode):
    if mode == "s":
        return DEC_BATCH, lambda i: 0
    tiles_per_batch = SEQ // TM_PROMPT
    return TM_PROMPT, lambda i: i // tiles_per_batch


def _const_spec(shape):
    zeros = (0,) * len(shape)
    return pl.BlockSpec(shape, lambda *g: zeros, pipeline_mode=pl.Buffered(1))


def _mod_linear_kernel(x_ref, sc_ref, sh_ref, w_ref, *o_refs, scale, blocks):
    h = (x_ref[...] * (1.0 + sc_ref[...]) + sh_ref[...]).astype(BF16)
    y = jnp.dot(h, w_ref[...], preferred_element_type=F32)
    if scale != 1.0:
        y = y * scale
    for (n, transposed), o_ref in zip(blocks, o_refs):
        block = y[:, n * D_MODEL:(n + 1) * D_MODEL].astype(o_ref.dtype)
        o_ref[...] = block.T if transposed else block


def _mod_linear(x, mods, layer, j_scale, j_shift, w, outs, scale=1.0):
    n = x.shape[0]
    tm, batch_of = _row_tiling(mods.mode)
    row = pl.BlockSpec((tm, D_MODEL), lambda i: (i, 0))
    tiles_per_batch = SEQ // tm
    col = pl.BlockSpec((None, D_MODEL, tm), lambda i: (i // tiles_per_batch, 0, i % tiles_per_batch))
    return pl.pallas_call(
        functools.partial(_mod_linear_kernel, scale=scale,
                          blocks=tuple((o[0], o[2]) for o in outs)),
        grid=(n // tm,),
        in_specs=[row, mods.spec(layer, j_scale, batch_of), mods.spec(layer, j_shift, batch_of),
                  _const_spec(w.shape)],
        out_specs=[col if o[2] else row for o in outs],
        out_shape=[jax.ShapeDtypeStruct((BATCH, D_MODEL, SEQ) if o[2] else (n, D_MODEL), o[1])
                   for o in outs],
        compiler_params=_params(("arbitrary",)),
        name="mod_linear",
    )(x, mods.arr, mods.arr, w)


def _block_tail_kernel(a_ref, x_ref, gm_ref, sc_ref, sh_ref, gf_ref, wm_ref, w1_ref, w2_ref,
                       g_ref, b_ref, o_ref, *, glu):
    y = jnp.dot(a_ref[...], wm_ref[...], preferred_element_type=F32)
    if glu:
        y = y[:, :D_MODEL] * jax.nn.sigmoid(y[:, D_MODEL:])
    x = _layer_norm(ALPHA * x_ref[...] + (1.0 + gm_ref[...]) * y, g_ref[0:1, :], b_ref[0:1, :])
    h = (x * (1.0 + sc_ref[...]) + sh_ref[...]).astype(BF16)
    acc = jnp.zeros(x.shape, F32)
    for c in range(D_FF // FF_CHUNK):
        cols = slice(c * FF_CHUNK, (c + 1) * FF_CHUNK)
        hid = jnp.dot(h, w1_ref[:, cols], preferred_element_type=F32)
        hid = jnp.square(jnp.maximum(hid, 0.0)).astype(BF16)
        acc = acc + jnp.dot(hid, w2_ref[cols, :], preferred_element_type=F32)
    z = ALPHA * x + (1.0 + gf_ref[...]) * acc
    o_ref[...] = _layer_norm(z, g_ref[1:2, :], b_ref[1:2, :])


def _block_tail(a, x, mods, layer, w_mix, w1, w2, ln_g, ln_b, glu):
    n = x.shape[0]
    tm, batch_of = _row_tiling(mods.mode)
    row = pl.BlockSpec((tm, D_MODEL), lambda i: (i, 0))
    ln = pl.BlockSpec((None, 2, D_MODEL), lambda i: (layer, 0, 0))
    weight = lambda w: pl.BlockSpec((None,) + w.shape[1:], lambda i: (layer % w.shape[0], 0, 0),
                                    pipeline_mode=pl.Buffered(1))
    return pl.pallas_call(
        functools.partial(_block_tail_kernel, glu=glu),
        grid=(n // tm,),
        in_specs=[row, row] + [mods.spec(layer, j, batch_of) for j in (2, 4, 3, 5)]
                 + [weight(w_mix), weight(w1), weight(w2), ln, ln],
        out_specs=row,
        out_shape=jax.ShapeDtypeStruct((n, D_MODEL), F32),
        compiler_params=_params(("arbitrary",)),
        name="block_tail",
    )(a, x, mods.arr, mods.arr, mods.arr, mods.arr, w_mix, w1, w2, ln_g, ln_b)


def _cmul(ar, ai, br, bi):
    return ar * br - ai * bi, ar * bi + ai * br


def _ssm_prep(lam_re, lam_im, log_dt, b_re, b_im, c_re, c_im):
    n_l = lam_re.shape[0]
    dt = jnp.exp(log_dt)[..., None]
    ns = list(range(TILE + 1)) + [TILE * m for m in range(2, SUBLANES + 1)]
    n = jnp.asarray(ns, F32).reshape(-1, 1, 1, 1)
    mag = jnp.exp(n * (lam_re * dt))
    pr = mag * jnp.cos(n * (lam_im * dt))
    pi = mag * jnp.sin(n * (lam_im * dt))
    ar, ai = pr[1], pi[1]
    den = lam_re * lam_re + lam_im * lam_im
    er = ((ar - 1.0) * lam_re + ai * lam_im) / den
    ei = (ai * lam_re - (ar - 1.0) * lam_im) / den
    bbr = er[..., None] * b_re - ei[..., None] * b_im
    bbi = er[..., None] * b_im + ei[..., None] * b_re
    gpc = CHUNK // SSM_GROUP
    exact = lax.Precision.HIGHEST

    def block_diag(m, rows_per_group, cols_per_group, steps=1):
        y = steps * cols_per_group
        c = jnp.arange(steps * gpc * cols_per_group)
        step, group, inner = c // (gpc * cols_per_group), (c // cols_per_group) % gpc, c % cols_per_group
        select = (jnp.arange(y)[:, None] == (step * cols_per_group + inner)[None, :]).astype(BF16)
        wide = jnp.einsum("...ry,yc->...rc", m.astype(BF16), select, preferred_element_type=BF16)
        row_group = (jnp.arange(m.shape[-2]) // rows_per_group) % gpc
        return jnp.where(row_group[:, None] == group[None, :], wide, jnp.zeros((), BF16))

    def per_chunk(m, order):
        x, _, _, a, b = m.shape
        m = m.reshape(x, n_l, N_CHUNK, gpc, a, b)
        return jnp.transpose(m, (1, 2) + tuple({"x": 0, "g": 3, "a": 4, "b": 5}[o] for o in order))

    swap = lambda m: jnp.swapaxes(m, -1, -2)
    b_in = per_chunk(jnp.concatenate([swap(bbr), swap(bbi)], axis=-1)[None], "gaxb")
    bcat = block_diag(b_in.reshape(n_l, N_CHUNK, CHUNK, 2 * SSM_STATE), SSM_GROUP, SSM_STATE,
                      steps=2)
    c_out = per_chunk(jnp.stack([swap(c_re), -swap(c_im)]), "xgab")
    ccat = block_diag(c_out.reshape(n_l, N_CHUNK, 2 * CHUNK_STATE, SSM_GROUP), SSM_STATE, SSM_GROUP)

    abr, abi = _cmul(pr[:TILE, ..., None], pi[:TILE, ..., None], bbr, bbi)
    dot_p = functools.partial(jnp.einsum, "lghp,jlgpi->jlgih", precision=exact)
    taps = per_chunk(dot_p(c_re, abr) - dot_p(c_im, abi), "xgab")
    kcat = block_diag(taps.reshape(n_l, N_CHUNK, TILE * CHUNK, SSM_GROUP), SSM_GROUP, SSM_GROUP)

    ends = per_chunk(jnp.concatenate([swap(abr[::-1]), swap(abi[::-1])], axis=-1), "xgab")
    wst = block_diag(ends.reshape(n_l, N_CHUNK, TILE * CHUNK, 2 * SSM_STATE), SSM_GROUP, SSM_STATE,
                     steps=2)

    mr, mi = _cmul(pr[1:TILE + 1, :, :, None, :], pi[1:TILE + 1, :, :, None, :], c_re, c_im)
    outs = jnp.stack([per_chunk(m, "gbxa") for m in (mr, -mi)], axis=2)
    call = block_diag(outs.reshape(n_l, N_CHUNK, 2 * CHUNK_STATE, TILE * SSM_GROUP), SSM_STATE,
                      SSM_GROUP, steps=TILE)

    flat = lambda a: jnp.moveaxis(a, 0, 1).reshape(n_l, -1, N_STATE)
    rows = lax.broadcasted_iota(jnp.int32, (SUBLANES, N_STATE), 0)
    parts = []
    for a in (flat(pr[TILE:]), flat(pi[TILE:])):
        masked = lambda m, first: jnp.where(rows >= first, a[:, m - 1:m, :], 0.0)
        parts.append([masked(1, 1), masked(2, 2), masked(4, 4), a, masked(SUBLANES, 0)])
    tab = jnp.stack([p for pair in zip(*parts) for p in pair], axis=1)
    atab = jnp.stack([jnp.broadcast_to(flat(a[1:2]), (n_l, SUBLANES, N_STATE)) for a in (pr, pi)],
                     axis=1)
    return dict(bcat=bcat, ccat=ccat, kcat=kcat, wst=wst, call=call, tab=tab, atab=atab)


def _ssm_prompt_kernel(x_ref, sc_ref, sh_ref, kcat_ref, wst_ref, call_ref, d_ref, tab_ref,
                       g_ref, hre_ref, him_ref, h_ref, buf_ref, y_ref):
    n_tiles = SEQ // TILE
    half = CHUNK_STATE
    h_ref[...] = x_ref[...] * (1.0 + sc_ref[...]) + sh_ref[...]

    steps = [h_ref[pl.ds(r, n_tiles, stride=TILE), :].astype(BF16) for r in range(TILE)]
    buf_ref[...] = jnp.dot(jnp.concatenate(steps, axis=1), wst_ref[...],
                           preferred_element_type=F32)

    row = lax.broadcasted_iota(jnp.int32, (SUBLANES, half), 0)

    def tile_scan(i, carry):
        cr, ci = carry
        r0 = pl.multiple_of(i * SUBLANES, SUBLANES)
        vr = buf_ref[pl.ds(r0, SUBLANES), 0:half]
        vi = buf_ref[pl.ds(r0, SUBLANES), half:2 * half]
        for lvl, shift in enumerate((1, 2, 4)):
            pr, pi = _cmul(tab_ref[2 * lvl], tab_ref[2 * lvl + 1],
                           pltpu.roll(vr, shift, 0), pltpu.roll(vi, shift, 0))
            vr, vi = vr + pr, vi + pi
        pr, pi = _cmul(tab_ref[6], tab_ref[7], cr, ci)
        buf_ref[pl.ds(r0, SUBLANES), 0:half] = jnp.where(row == 0, cr, pltpu.roll(vr + pr, 1, 0))
        buf_ref[pl.ds(r0, SUBLANES), half:2 * half] = jnp.where(row == 0, ci,
                                                                 pltpu.roll(vi + pi, 1, 0))
        pr, pi = _cmul(tab_ref[8], tab_ref[9], cr, ci)
        last = SUBLANES - 1
        return (jnp.broadcast_to(vr[last:, :], vr.shape) + pr,
                jnp.broadcast_to(vi[last:, :], vi.shape) + pi)

    zero = jnp.zeros((SUBLANES, half), F32)
    cr, ci = lax.fori_loop(0, n_tiles // SUBLANES, tile_scan, (zero, zero))
    hre_ref[...] = cr[0:1, :]
    him_ref[...] = ci[0:1, :]

    y_state = jnp.dot(buf_ref[...].astype(BF16), call_ref[...], preferred_element_type=F32)
    for s in range(TILE):
        y_ref[pl.ds(s, n_tiles, stride=TILE), :] = y_state[:, s * CHUNK:(s + 1) * CHUNK]

    step_in_tile = lax.broadcasted_iota(jnp.int32, (T_SSM, CHUNK), 0) % TILE
    for blk in range(SEQ // T_SSM):
        rows = slice(blk * T_SSM, (blk + 1) * T_SSM)
        h = h_ref[rows, :]
        lags = [h.astype(BF16)] + [
            jnp.where(step_in_tile >= j, pltpu.roll(h, j, 0), 0.0).astype(BF16)
            for j in range(1, TILE)]
        y = jnp.dot(jnp.concatenate(lags, axis=1), kcat_ref[...], preferred_element_type=F32)
        y = y + y_ref[rows, :] + d_ref[...] * h
        g_ref[rows, :] = jax.nn.gelu(y).astype(g_ref.dtype)


def _ssm_prompt(x, mods, layer, p, d_skip):
    col = pl.BlockSpec((SEQ, CHUNK), lambda k, b: (b, k))
    fin = pl.BlockSpec((None, 1, CHUNK_STATE), lambda k, b: (b, 0, k))
    mod = lambda j: pl.BlockSpec((None, None, 1, CHUNK),
                                 lambda k, b: (layer, DEC_BATCH + b, 0, j * N_CHUNK + k))
    weight = lambda a: pl.BlockSpec((None, None) + a.shape[2:], lambda k, b: (layer, k, 0, 0))
    g, hre, him = pl.pallas_call(
        _ssm_prompt_kernel,
        grid=(N_CHUNK, BATCH),
        in_specs=[col, mod(1), mod(0), weight(p["kcat"]), weight(p["wst"]), weight(p["call"]),
                  pl.BlockSpec((None, 1, CHUNK), lambda k, b: (layer, 0, k)),
                  pl.BlockSpec((None, p["tab"].shape[1], SUBLANES, CHUNK_STATE),
                               lambda k, b: (layer, 0, 0, k))],
        out_specs=[col, fin, fin],
        out_shape=[jax.ShapeDtypeStruct((BATCH * SEQ, D_MODEL), BF16),
                   jax.ShapeDtypeStruct((BATCH, 1, N_STATE), F32),
                   jax.ShapeDtypeStruct((BATCH, 1, N_STATE), F32)],
        scratch_shapes=[pltpu.VMEM((SEQ, CHUNK), F32),
                        pltpu.VMEM((SEQ // TILE, 2 * CHUNK_STATE), F32),
                        pltpu.VMEM((SEQ, CHUNK), F32)],
        compiler_params=_params(("arbitrary", "arbitrary")),
        name="ssm_prompt",
    )(x, mods.arr, mods.arr, p["kcat"], p["wst"], p["call"], d_skip[:, None, :], p["tab"])
    shape = (BATCH, SSM_GROUPS, SSM_STATE)
    return g, hre.reshape(shape), him.reshape(shape)


def _ssm_sample_kernel(x_ref, sc_ref, sh_ref, bcat_ref, ccat_ref, d_ref, a_ref, h0r_ref, h0i_ref,
                       g_ref, hre_ref, him_ref):
    tile = lambda m: jnp.concatenate([m] * DEC_SEQ, axis=0)
    h = x_ref[...] * (1.0 + tile(sc_ref[...])) + tile(sh_ref[...])
    bu = jnp.dot(h.astype(BF16), bcat_ref[...], preferred_element_type=F32)
    ar = a_ref[0, 0:1, :]
    ai = a_ref[1, 0:1, :]
    sr = h0r_ref[...].T
    si = h0i_ref[...].T
    states = []
    for t in range(DEC_SEQ):
        rows = slice(t * DEC_BATCH, (t + 1) * DEC_BATCH)
        pr, pi = _cmul(ar, ai, sr, si)
        sr = pr + bu[rows, :CHUNK_STATE]
        si = pi + bu[rows, CHUNK_STATE:]
        states.append(jnp.concatenate([sr, si], axis=1))
    xs = jnp.concatenate(states, axis=0).astype(BF16)
    y = jnp.dot(xs, ccat_ref[...], preferred_element_type=F32) + d_ref[...] * h
    g_ref[...] = jax.nn.gelu(y).astype(g_ref.dtype)
    hre_ref[...] = sr.T
    him_ref[...] = si.T


def _ssm_sample(x, mods, layer, p, d_skip, h0_re, h0_im):
    n = DEC_BATCH * DEC_SEQ
    col = pl.BlockSpec((n, CHUNK), lambda k: (0, k))
    st = pl.BlockSpec((None, CHUNK_STATE, DEC_BATCH), lambda k: (layer, k, 0))
    mod = lambda j: pl.BlockSpec((None, DEC_BATCH, CHUNK), lambda k: (layer, 0, j * N_CHUNK + k))
    state_major = lambda a: jnp.transpose(a, (0, 2, 3, 1)).reshape(-1, N_STATE, DEC_BATCH)
    g, hre, him = pl.pallas_call(
        _ssm_sample_kernel,
        grid=(N_CHUNK,),
        in_specs=[col, mod(1), mod(0),
                  pl.BlockSpec((None, None, CHUNK, 2 * CHUNK_STATE), lambda k: (layer, k, 0, 0)),
                  pl.BlockSpec((None, None, 2 * CHUNK_STATE, CHUNK), lambda k: (layer, k, 0, 0)),
                  pl.BlockSpec((None, 1, CHUNK), lambda k: (layer, 0, k)),
                  pl.BlockSpec((None, 2, SUBLANES, CHUNK_STATE), lambda k: (layer, 0, 0, k)),
                  st, st],
        out_specs=[col] + [pl.BlockSpec((CHUNK_STATE, DEC_BATCH), lambda k: (k, 0))] * 2,
        out_shape=[jax.ShapeDtypeStruct((n, D_MODEL), BF16),
                   jax.ShapeDtypeStruct((N_STATE, DEC_BATCH), F32),
                   jax.ShapeDtypeStruct((N_STATE, DEC_BATCH), F32)],
        compiler_params=_params(("arbitrary",)),
        name="ssm_sample",
    )(x, mods.arr, mods.arr, p["bcat"], p["ccat"], d_skip[:, None, :], p["atab"],
      state_major(h0_re), state_major(h0_im))
    batch_major = lambda a: jnp.transpose(a.reshape(SSM_GROUPS, SSM_STATE, DEC_BATCH), (2, 0, 1))
    return g, batch_major(hre), batch_major(him)


def _rel_bucket(rel):
    n = jnp.maximum(rel, 0)
    max_exact = NUM_BUCKETS // 2
    large = max_exact + (jnp.log(jnp.maximum(n, 1).astype(F32) / max_exact)
                         / math.log(MAX_DISTANCE / max_exact) * (NUM_BUCKETS - max_exact)).astype(jnp.int32)
    large = jnp.minimum(large, NUM_BUCKETS - 1)
    return jnp.where(n < max_exact, n, large)


def _attn_scalars(attn_lam, subln_g, layer):
    lam_init = 0.8 - 0.6 * math.exp(-0.3 * layer)
    lam = (jnp.exp(jnp.sum(attn_lam[0] * attn_lam[1])) - jnp.exp(jnp.sum(attn_lam[2] * attn_lam[3]))
           + lam_init)
    lam_tile = jnp.full((SUBLANES, LANES), lam, F32)
    gain = (subln_g * (1.0 - lam_init)).reshape(1, V_DIM)
    return lam_tile, gain


def _bias_by_distance(rel_bias, n):
    bv = rel_bias[_rel_bucket(jnp.arange(n, dtype=jnp.int32))]
    return ((bv - rel_bias[NUM_BUCKETS - 1]) * LOG2E).T


def _softmax_update(s, m_prev):
    m_new = jnp.maximum(m_prev, jnp.max(s, axis=1, keepdims=True))
    p = jnp.exp2(s - jnp.concatenate([m_new] * (s.shape[1] // LANES), axis=1))
    return m_new, p.astype(BF16), jnp.exp2(m_prev - m_new)


def _with_ones(v):
    return jnp.concatenate([v, jnp.ones((v.shape[0], V_DIM), BF16)], axis=1)


def _acc_update(acc, alpha, p, vx):
    return (jnp.concatenate([alpha, alpha], axis=1) * acc
            + jnp.dot(p, vx, preferred_element_type=F32))


_NT = (((1,), (1,)), ((), ()))


def _attn_prompt_kernel(q_ref, k_ref, v_ref, tab_ref, lam_ref, gain_ref, o_ref,
                        q2_ref, m_ref, acc_ref, s_ref):
    i = pl.program_id(2)
    last = pl.num_programs(2) - 1

    def key_rows(j):
        return pl.ds(pl.multiple_of(j * TQ, TQ), TQ)

    def stack_queries(hh, tile):
        q = q_ref[key_rows(tile), hh * V_DIM:(hh + 1) * V_DIM]
        lane = lax.broadcasted_iota(jnp.int32, q.shape, 1)
        zero = jnp.zeros_like(q)
        q2_ref[hh, 0:TQ, :] = jnp.where(lane < HEAD_DIM, q, zero)
        q2_ref[hh, TQ:2 * TQ, :] = jnp.where(lane >= HEAD_DIM, q, zero)

    def logits(hh, j):
        cols = slice(hh * V_DIM, (hh + 1) * V_DIM)
        return lax.dot_general(q2_ref[hh], k_ref[key_rows(j), cols].astype(BF16), _NT,
                               preferred_element_type=F32)

    @pl.when(i == 0)
    def _():
        stack_queries(0, 0)
        s_ref[...] = logits(0, 0)

    for hh in range(1, HEADS_PER_STEP):
        stack_queries(hh, i)
    m_ref[...] = jnp.full(m_ref.shape, NEG_INF, F32)
    acc_ref[...] = jnp.zeros_like(acc_ref)

    def consume(hh, j, s, table):
        cols = slice(hh * V_DIM, (hh + 1) * V_DIM)
        if table is not None:
            bias = tab_ref[hh, table]
            s = s + jnp.concatenate([bias, bias], axis=0)
        m_new, p, alpha = _softmax_update(s, m_ref[hh])
        acc_ref[hh] = _acc_update(acc_ref[hh], alpha, p,
                                  _with_ones(v_ref[key_rows(j), cols].astype(BF16)))
        m_ref[hh] = m_new

    def step(j, table, same_tile):
        consume(0, j, s_ref[...], table)
        for hh in range(1, HEADS_PER_STEP):
            consume(hh, j, logits(hh, j), table)
        if same_tile:
            s_ref[...] = logits(0, j + 1)
        else:
            stack_queries(0, jnp.minimum(i + 1, last))
            s_ref[...] = logits(0, 0)

    def far(j, c):
        step(j, None, True)
        return c

    lax.fori_loop(0, jnp.maximum(i - 1, 0), far, 0)

    @pl.when(i >= 1)
    def _():
        step(i - 1, 1, True)

    step(i, 0, False)

    for hh in range(HEADS_PER_STEP):
        acc = acc_ref[hh]
        o = acc[:, :V_DIM] / acc[:, V_DIM:]
        o = o[0:TQ, :] - lam_ref[0:1, 0:1] * o[TQ:2 * TQ, :]
        o = o * lax.rsqrt(jnp.mean(o * o, axis=-1, keepdims=True) + LN_EPS) * gain_ref[...]
        o_ref[:, hh * V_DIM:(hh + 1) * V_DIM] = o.astype(o_ref.dtype)


def _attn_prompt_tables(rel_bias):
    bv = _bias_by_distance(rel_bias, 2 * TQ)
    masked = jnp.full((N_HEADS, TQ), NEG_INF, F32)
    diag = jnp.concatenate([bv[:, 0:1], masked, bv[:, TQ - 1:0:-1]], axis=1)
    prev = jnp.concatenate([bv[:, TQ:0:-1], bv[:, TQ:TQ + 1], bv[:, 2 * TQ - 1:TQ:-1]], axis=1)
    vec = jnp.broadcast_to(jnp.stack([diag, prev], axis=1)[:, :, None, :],
                           (N_HEADS, 2, SUBLANES, 2 * TQ))
    return pl.pallas_call(
        _toeplitz_kernel,
        grid=(N_HEADS,),
        in_specs=[pl.BlockSpec((None, 2, SUBLANES, 2 * TQ), lambda h: (h, 0, 0, 0))],
        out_specs=pl.BlockSpec((None, 2, TQ, TQ), lambda h: (h, 0, 0, 0)),
        out_shape=jax.ShapeDtypeStruct((N_HEADS, 2, TQ, TQ), F32),
        compiler_params=_params(("arbitrary",)),
        name="bias_tiles",
    )(vec)


def _toeplitz_kernel(vec_ref, o_ref):
    for t in range(o_ref.shape[0]):
        rows = jnp.broadcast_to(vec_ref[t, 0:1, :], (TQ, 2 * TQ))
        o_ref[t] = pltpu.roll(rows, 0, 1, stride=1, stride_axis=0)[:, :TQ]


def _attn_prompt(q, k, v, tab, lam_tile, gain):
    n_q = SEQ // TQ
    width = HEADS_PER_STEP * V_DIM
    shape3 = (BATCH, SEQ, D_MODEL)
    qspec = pl.BlockSpec((None, TQ, width), lambda b, h, i: (b, i, h))
    kvspec = pl.BlockSpec((None, SEQ, width), lambda b, h, i: (b, 0, h))
    o = pl.pallas_call(
        _attn_prompt_kernel,
        grid=(BATCH, N_HEADS // HEADS_PER_STEP, n_q),
        in_specs=[kvspec, kvspec, kvspec,
                  pl.BlockSpec((HEADS_PER_STEP, 2, TQ, TQ), lambda b, h, i: (h, 0, 0, 0)),
                  pl.BlockSpec((SUBLANES, LANES), lambda b, h, i: (0, 0)),
                  pl.BlockSpec((1, V_DIM), lambda b, h, i: (0, 0))],
        out_specs=qspec,
        out_shape=jax.ShapeDtypeStruct(shape3, BF16),
        scratch_shapes=[pltpu.VMEM((HEADS_PER_STEP, 2 * TQ, V_DIM), BF16),
                        pltpu.VMEM((HEADS_PER_STEP, 2 * TQ, LANES), F32),
                        pltpu.VMEM((HEADS_PER_STEP, 2 * TQ, 2 * V_DIM), F32),
                        pltpu.VMEM((2 * TQ, TQ), F32)],
        compiler_params=_params(("arbitrary", "arbitrary", "arbitrary")),
        name="attn_prompt",
    )(q.reshape(shape3), k.reshape(shape3), v.reshape(shape3), tab, lam_tile, gain)
    return o.reshape(BATCH * SEQ, D_MODEL)


N_SROWS = N_HEADS * 2 * DEC_SEQ


def _attn_sample_kernel(pt_ref, q_ref, qmask_ref, knew_ref, vnew_ref, bias_ref, bnew_ref,
                        lam_ref, gain_ref, *rest):
    del pt_ref
    k_refs = rest[:PAGES_PER_STEP]
    v_refs = rest[PAGES_PER_STEP:2 * PAGES_PER_STEP]
    o_ref, qx_ref, m_ref, acc_ref = rest[2 * PAGES_PER_STEP:]
    step = pl.program_id(1)
    rows_per_head = 2 * DEC_SEQ

    @pl.when(step == 0)
    def _():
        q = q_ref[...].astype(F32)
        q = jnp.concatenate([q, q], axis=0)
        qx_ref[...] = (jnp.concatenate([q] * N_HEADS, axis=0) * qmask_ref[...]).astype(BF16)
        m_ref[...] = jnp.full(m_ref.shape, NEG_INF, F32)
        acc_ref[...] = jnp.zeros_like(acc_ref)

    def update(s, values_of_head):
        m_new, p, alpha = _softmax_update(s, m_ref[...])
        for h in range(N_HEADS):
            r = slice(h * rows_per_head, (h + 1) * rows_per_head)
            acc_ref[r, :] = _acc_update(acc_ref[r, :], alpha[r, :], p[r, :],
                                        _with_ones(values_of_head(h)))
        m_ref[...] = m_new

    qx = qx_ref[...]
    kt = jnp.concatenate([r[...].astype(BF16) for r in k_refs], axis=1)
    s = jnp.dot(qx, kt, preferred_element_type=F32) + bias_ref[...]
    update(s, lambda h: jnp.concatenate(
        [r[pl.ds(h, PAGE_SIZE, stride=N_HEADS), :].astype(BF16) for r in v_refs], axis=0))

    @pl.when(step == pl.num_programs(1) - 1)
    def _():
        pad = jnp.zeros((LANES - SUBLANES, D_MODEL), BF16)
        kn = jnp.concatenate([knew_ref[...], pad], axis=0)
        vn = jnp.concatenate([vnew_ref[...], pad], axis=0)
        sn = lax.dot_general(qx, kn, _NT, preferred_element_type=F32) + bnew_ref[...]
        update(sn, lambda h: vn[:, h * V_DIM:(h + 1) * V_DIM])
        acc = acc_ref[...]
        o = acc[:, :V_DIM] / acc[:, V_DIM:]
        heads = []
        for h in range(N_HEADS):
            r0 = h * rows_per_head
            oh = o[r0:r0 + DEC_SEQ, :] - lam_ref[0:1, 0:1] * o[r0 + DEC_SEQ:r0 + rows_per_head, :]
            oh = oh * lax.rsqrt(jnp.mean(oh * oh, axis=-1, keepdims=True) + LN_EPS) * gain_ref[...]
            heads.append(oh)
        o_ref[...] = jnp.concatenate(heads, axis=1)


def _attn_sample_tables(rel_bias):
    bv = _bias_by_distance(rel_bias, PAST_LEN + DEC_SEQ)
    far_first = bv[:, ::-1]
    past = jnp.stack([far_first[:, DEC_SEQ - 1 - t:DEC_SEQ - 1 - t + PAST_LEN]
                      for t in range(DEC_SEQ)], axis=1)
    rel = (jnp.arange(DEC_SEQ, dtype=jnp.int32)[:, None]
           - jnp.arange(LANES, dtype=jnp.int32)[None, :])
    new = jnp.where(rel >= 0, bv[:, jnp.clip(rel, 0, DEC_SEQ - 1)], NEG_INF)
    rows = lambda a: jnp.broadcast_to(a[:, None], (N_HEADS, 2) + a.shape[1:]).reshape(
        N_SROWS, a.shape[-1])
    col = jnp.arange(D_MODEL, dtype=jnp.int32)[None, :]
    row = jnp.arange(N_SROWS, dtype=jnp.int32)[:, None]
    qmask = ((col // V_DIM == row // (2 * DEC_SEQ))
             & ((col % V_DIM) // HEAD_DIM == (row // DEC_SEQ) % 2)).astype(F32)
    return rows(past), rows(new), qmask


def _attn_sample(q, k_new, v_new, cache_k, cache_v, page_table, tables, lam_tile, gain):
    past, new, qmask = tables
    n_phys = cache_k.shape[0]
    ck = jnp.transpose(cache_k, (0, 2, 3, 4, 1)).reshape(n_phys, D_MODEL, PAGE_SIZE)
    cv = cache_v.reshape(n_phys, PAGE_SIZE * N_HEADS, V_DIM)
    n_steps = N_PAGES // PAGES_PER_STEP
    keys_per_step = PAGES_PER_STEP * PAGE_SIZE

    def page_spec(r):
        return pl.BlockSpec((None, D_MODEL, PAGE_SIZE),
                            lambda b, s, pt: (pt[b * N_PAGES + s * PAGES_PER_STEP + r], 0, 0))

    per_seq = lambda rows: pl.BlockSpec((None, rows, D_MODEL), lambda b, s, pt: (b, 0, 0))
    const = lambda shape: pl.BlockSpec(shape, lambda b, s, pt: (0,) * len(shape))
    grid_spec = pltpu.PrefetchScalarGridSpec(
        num_scalar_prefetch=1,
        grid=(DEC_BATCH, n_steps),
        in_specs=[per_seq(DEC_SEQ), const(qmask.shape), per_seq(SUBLANES), per_seq(SUBLANES),
                  pl.BlockSpec((N_SROWS, keys_per_step), lambda b, s, pt: (0, s)),
                  const(new.shape), const(lam_tile.shape), const(gain.shape)]
                 + [page_spec(r) for r in range(PAGES_PER_STEP)] * 2,
        out_specs=per_seq(DEC_SEQ),
        scratch_shapes=[pltpu.VMEM((N_SROWS, D_MODEL), BF16), pltpu.VMEM((N_SROWS, LANES), F32),
                        pltpu.VMEM((N_SROWS, 2 * V_DIM), F32)],
    )
    return pl.pallas_call(
        _attn_sample_kernel,
        grid_spec=grid_spec,
        out_shape=jax.ShapeDtypeStruct((DEC_BATCH, DEC_SEQ, D_MODEL), F32),
        compiler_params=_params(("arbitrary", "arbitrary")),
        name="attn_sample",
    )(page_table.reshape(-1), q, qmask, k_new, v_new, past, new, lam_tile, gain,
      *([ck] * PAGES_PER_STEP), *([cv] * PAGES_PER_STEP))


def _to_batch_major(a):
    return jnp.transpose(a.reshape(DEC_SEQ, DEC_BATCH, D_MODEL), (1, 0, 2))


def _to_time_major(a):
    return jnp.transpose(a, (1, 0, 2)).reshape(DEC_SEQ * DEC_BATCH, D_MODEL)


def _trunk(x, mods, mods_kv, h0_re, h0_im, attend, w):
    ssm_re, ssm_im = [], []
    k = v = None
    for layer in range(DEPTH):
        if layer < N_A_LAYERS:
            if mods.mode == "p":
                g, fr, fi = _ssm_prompt(x, mods, layer, w["ssm"], w["ssm_d"])
            else:
                g, fr, fi = _ssm_sample(x, mods, layer, w["ssm"], w["ssm_d"], h0_re, h0_im)
            ssm_re.append(fr)
            ssm_im.append(fi)
            mixed, w_mix = g, w["glu"]
        else:
            j = layer - N_A_LAYERS
            (q,) = _mod_linear(x, mods, layer, 1, 0, w["wq"][j], [(0, BF16, False)],
                               scale=HEAD_DIM ** -0.5 * LOG2E)
            mixed, w_mix = attend(q, k, v, j), w["wo"]
        x = _block_tail(mixed, x, mods, layer, w_mix, w["w1"], w["w2"], w["ln_g"], w["ln_b"],
                        glu=layer < N_A_LAYERS)
        if layer == N_A_LAYERS - 1:
            outs = [(0, F32, False), (1, F32, False)] + [(0, F32, True)] * (mods.mode == "p")
            k, v, *k_out = _mod_linear(x, mods_kv, 0, 1, 0, w["wkv"], outs)
    return x, jnp.stack(ssm_re), jnp.stack(ssm_im), (k_out or [k])[0], v


def kernel(x_prompt, x_sample, state_ssm_re, state_ssm_im, cache_k, cache_v, page_table, c_prompt, c_sample, rel_bias, w_ada, b_ada, ln_g, ln_b, ssm_lam_re, ssm_lam_im, ssm_log_dt, ssm_b_re, ssm_b_im, ssm_c_re, ssm_c_im, ssm_d, ssm_w_glu_a, ssm_w_glu_b, w_ada_kv, b_ada_kv, w_kv, attn_w_q, attn_lam, attn_subln_g, attn_w_o, mlp_w1, mlp_w2):
    w = {
        "ssm": _ssm_prep(ssm_lam_re, ssm_lam_im, ssm_log_dt, ssm_b_re, ssm_b_im, ssm_c_re, ssm_c_im),
        "ssm_d": ssm_d,
        "glu": jnp.concatenate([ssm_w_glu_a, ssm_w_glu_b], axis=-1).astype(BF16),
        "wq": attn_w_q.astype(BF16),
        "wo": attn_w_o.astype(BF16),
        "wkv": w_kv.astype(BF16),
        "w1": mlp_w1.astype(BF16),
        "w2": mlp_w2.astype(BF16),
        "ln_g": ln_g,
        "ln_b": ln_b,
    }
    c_all = jnp.concatenate([c_sample, c_prompt, jnp.zeros((BATCH, D_MODEL), F32)], axis=0)
    ada = _ada(c_all, w_ada, b_ada)
    ada_kv = _ada(c_all, w_ada_kv[None], b_ada_kv[None])
    scalars = [_attn_scalars(attn_lam[j], attn_subln_g[j], N_A_LAYERS + j)
               for j in range(DEPTH - N_A_LAYERS)]

    prompt_tab = _attn_prompt_tables(rel_bias)

    def attend_prompt(q, k, v, j):
        return _attn_prompt(q, k, v, prompt_tab, *scalars[j])

    y_p, re_p, im_p, k_p, v_p = _trunk(x_prompt.reshape(BATCH * SEQ, D_MODEL), _Mods(ada, "p"),
                                       _Mods(ada_kv, "p"), None, None, attend_prompt, w)

    sample_tab = _attn_sample_tables(rel_bias)

    def attend_sample(q, k, v, j):
        pad = lambda a: jnp.pad(_to_batch_major(a).astype(BF16),
                                ((0, 0), (0, SUBLANES - DEC_SEQ), (0, 0)))
        o = _attn_sample(_to_batch_major(q), pad(k), pad(v), cache_k, cache_v, page_table,
                         sample_tab, *scalars[j])
        return _to_time_major(o).astype(BF16)

    y_s, re_s, im_s, k_s, v_s = _trunk(_to_time_major(x_sample), _Mods(ada, "s"), _Mods(ada_kv, "s"),
                                       state_ssm_re, state_ssm_im, attend_sample, w)

    return (y_p.reshape(BATCH, SEQ, D_MODEL), _to_batch_major(y_s), re_p, im_p,
            jnp.transpose(k_p.reshape(BATCH, N_HEADS, 2, HEAD_DIM, SEQ), (0, 4, 1, 2, 3)),
            v_p.reshape(BATCH, SEQ, N_HEADS, V_DIM),
            re_s, im_s,
            _to_batch_major(k_s).reshape(DEC_BATCH, DEC_SEQ, N_HEADS, 2, HEAD_DIM),
            _to_batch_major(v_s).reshape(DEC_BATCH, DEC_SEQ, N_HEADS, V_DIM))
```

```python
import functools
import math

import jax
import jax.numpy as jnp
from jax import lax
from jax.experimental import pallas as pl
from jax.experimental.pallas import tpu as pltpu

F32 = jnp.float32
BF16 = jnp.bfloat16

D_MODEL = 1024
BATCH = 4
SEQ = 4096
DEPTH = 4
DEC_BATCH = 128
DEC_SEQ = 4
PAST_LEN = 2048
PAGE_SIZE = 128
N_PAGES = PAST_LEN // PAGE_SIZE
N_A_LAYERS = DEPTH // 2
SSM_GROUP = 16
SSM_GROUPS = D_MODEL // SSM_GROUP
SSM_STATE = 64
N_STATE = SSM_GROUPS * SSM_STATE
N_HEADS = 8
HEAD_DIM = D_MODEL // (2 * N_HEADS)
V_DIM = 2 * HEAD_DIM
D_FF = 4 * D_MODEL
NUM_BUCKETS = 32
MAX_DISTANCE = 128
N_MOD = 6
ALPHA = (2.0 * DEPTH) ** 0.25
LN_EPS = 1e-5
NEG_INF = -1e30

SUBLANES = 8
LANES = 128
VMEM_LIMIT = 48 * 1024 * 1024

MOD_ROWS = DEC_BATCH + 2 * BATCH
CHUNK = LANES
N_CHUNK = D_MODEL // CHUNK
CHUNK_STATE = N_STATE // N_CHUNK
TM_PROMPT = 512
TILE = SUBLANES
T_SSM = 512
TQ = 512
HEADS_PER_STEP = 2
PAGES_PER_STEP = 16
LOG2E = math.log2(math.e)
FF_CHUNK = 1024


def _params(sem, vmem=VMEM_LIMIT):
    return pltpu.CompilerParams(dimension_semantics=sem, vmem_limit_bytes=vmem)


def _layer_norm(z, g, b):
    mu = jnp.mean(z, axis=-1, keepdims=True)
    zc = z - mu
    var = jnp.mean(zc * zc, axis=-1, keepdims=True)
    return zc * lax.rsqrt(var + LN_EPS) * g + b


def _ada_kernel(c_ref, w_ref, b_ref, o_ref):
    c = c_ref[...]
    sc = (c * jax.nn.sigmoid(c)).astype(BF16)
    w = w_ref[...].astype(BF16)
    o_ref[...] = jnp.dot(sc, w, preferred_element_type=F32) + b_ref[...]


def _ada(c_all, w, b):
    n_l, _, width = w.shape
    tn = 1024
    return pl.pallas_call(
        _ada_kernel,
        grid=(n_l, width // tn),
        in_specs=[
            pl.BlockSpec((MOD_ROWS, D_MODEL), lambda l, j: (0, 0)),
            pl.BlockSpec((None, D_MODEL, tn), lambda l, j: (l, 0, j)),
            pl.BlockSpec((None, 1, tn), lambda l, j: (l, 0, j)),
        ],
        out_specs=pl.BlockSpec((None, MOD_ROWS, tn), lambda l, j: (l, 0, j)),
        out_shape=jax.ShapeDtypeStruct((n_l, MOD_ROWS, width), F32),
        compiler_params=_params(("arbitrary", "arbitrary")),
        name="ada_mod",
    )(c_all, w, b.reshape(n_l, 1, width))


class _Mods:
    def __init__(self, arr, mode):
        self.mode = mode
        self.arr = arr if mode == "s" else arr.reshape(arr.shape[0], MOD_ROWS, 1, arr.shape[-1])

    def spec(self, layer, j, batch_of):
        if self.mode == "s":
            return pl.BlockSpec((None, DEC_BATCH, D_MODEL), lambda *g: (layer, 0, j))
        return pl.BlockSpec((None, None, 1, D_MODEL),
                            lambda *g: (layer, DEC_BATCH + batch_of(*g), 0, j))


def _row_tiling(mode):
    if mode == "s":
        return DEC_BATCH, lambda i: 0
    tiles_per_batch = SEQ // TM_PROMPT
    return TM_PROMPT, lambda i: i // tiles_per_batch


def _const_spec(shape):
    zeros = (0,) * len(shape)
    return pl.BlockSpec(shape, lambda *g: zeros, pipeline_mode=pl.Buffered(1))


def _mod_linear_kernel(x_ref, sc_ref, sh_ref, w_ref, *o_refs, scale, blocks):
    h = (x_ref[...] * (1.0 + sc_ref[...]) + sh_ref[...]).astype(BF16)
    y = jnp.dot(h, w_ref[...], preferred_element_type=F32)
    if scale != 1.0:
        y = y * scale
    for (n, transposed), o_ref in zip(blocks, o_refs):
        block = y[:, n * D_MODEL:(n + 1) * D_MODEL].astype(o_ref.dtype)
        o_ref[...] = block.T if transposed else block


def _mod_linear(x, mods, layer, j_scale, j_shift, w, outs, scale=1.0):
    n = x.shape[0]
    tm, batch_of = _row_tiling(mods.mode)
    row = pl.BlockSpec((tm, D_MODEL), lambda i: (i, 0))
    tiles_per_batch = SEQ // tm
    col = pl.BlockSpec((None, D_MODEL, tm), lambda i: (i // tiles_per_batch, 0, i % tiles_per_batch))
    return pl.pallas_call(
        functools.partial(_mod_linear_kernel, scale=scale,
                          blocks=tuple((o[0], o[2]) for o in outs)),
        grid=(n // tm,),
        in_specs=[row, mods.spec(layer, j_scale, batch_of), mods.spec(layer, j_shift, batch_of),
                  _const_spec(w.shape)],
        out_specs=[col if o[2] else row for o in outs],
        out_shape=[jax.ShapeDtypeStruct((BATCH, D_MODEL, SEQ) if o[2] else (n, D_MODEL), o[1])
                   for o in outs],
        compiler_params=_params(("arbitrary",)),
        name="mod_linear",
    )(x, mods.arr, mods.arr, w)


def _block_tail_kernel(a_ref, x_ref, gm_ref, sc_ref, sh_ref, gf_ref, wm_ref, w1_ref, w2_ref,
                       g_ref, b_ref, o_ref, *, glu):
    y = jnp.dot(a_ref[...], wm_ref[...], preferred_element_type=F32)
    if glu:
        y = y[:, :D_MODEL] * jax.nn.sigmoid(y[:, D_MODEL:])
    x = _layer_norm(ALPHA * x_ref[...] + (1.0 + gm_ref[...]) * y, g_ref[0:1, :], b_ref[0:1, :])
    h = (x * (1.0 + sc_ref[...]) + sh_ref[...]).astype(BF16)
    acc = jnp.zeros(x.shape, F32)
    for c in range(D_FF // FF_CHUNK):
        cols = slice(c * FF_CHUNK, (c + 1) * FF_CHUNK)
        hid = jnp.dot(h, w1_ref[:, cols], preferred_element_type=F32)
        hid = jnp.square(jnp.maximum(hid, 0.0)).astype(BF16)
        acc = acc + jnp.dot(hid, w2_ref[cols, :], preferred_element_type=F32)
    z = ALPHA * x + (1.0 + gf_ref[...]) * acc
    o_ref[...] = _layer_norm(z, g_ref[1:2, :], b_ref[1:2, :])


def _block_tail(a, x, mods, layer, w_mix, w1, w2, ln_g, ln_b, glu):
    n = x.shape[0]
    tm, batch_of = _row_tiling(mods.mode)
    row = pl.BlockSpec((tm, D_MODEL), lambda i: (i, 0))
    ln = pl.BlockSpec((None, 2, D_MODEL), lambda i: (layer, 0, 0))
    weight = lambda w: pl.BlockSpec((None,) + w.shape[1:], lambda i: (layer % w.shape[0], 0, 0),
                                    pipeline_mode=pl.Buffered(1))
    return pl.pallas_call(
        functools.partial(_block_tail_kernel, glu=glu),
        grid=(n // tm,),
        in_specs=[row, row] + [mods.spec(layer, j, batch_of) for j in (2, 4, 3, 5)]
                 + [weight(w_mix), weight(w1), weight(w2), ln, ln],
        out_specs=row,
        out_shape=jax.ShapeDtypeStruct((n, D_MODEL), F32),
        compiler_params=_params(("arbitrary",)),
        name="block_tail",
    )(a, x, mods.arr, mods.arr, mods.arr, mods.arr, w_mix, w1, w2, ln_g, ln_b)


def _cmul(ar, ai, br, bi):
    return ar * br - ai * bi, ar * bi + ai * br


def _ssm_prep(lam_re, lam_im, log_dt, b_re, b_im, c_re, c_im):
    n_l = lam_re.shape[0]
    dt = jnp.exp(log_dt)[..., None]
    ns = list(range(TILE + 1)) + [TILE * m for m in range(2, SUBLANES + 1)]
    n = jnp.asarray(ns, F32).reshape(-1, 1, 1, 1)
    mag = jnp.exp(n * (lam_re * dt))
    pr = mag * jnp.cos(n * (lam_im * dt))
    pi = mag * jnp.sin(n * (lam_im * dt))
    ar, ai = pr[1], pi[1]
    den = lam_re * lam_re + lam_im * lam_im
    er = ((ar - 1.0) * lam_re + ai * lam_im) / den
    ei = (ai * lam_re - (ar - 1.0) * lam_im) / den
    bbr = er[..., None] * b_re - ei[..., None] * b_im
    bbi = er[..., None] * b_im + ei[..., None] * b_re
    gpc = CHUNK // SSM_GROUP
    exact = lax.Precision.HIGHEST

    def block_diag(m, rows_per_group, cols_per_group, steps=1):
        y = steps * cols_per_group
        c = jnp.arange(steps * gpc * cols_per_group)
        step, group, inner = c // (gpc * cols_per_group), (c // cols_per_group) % gpc, c % cols_per_group
        select = (jnp.arange(y)[:, None] == (step * cols_per_group + inner)[None, :]).astype(BF16)
        wide = jnp.einsum("...ry,yc->...rc", m.astype(BF16), select, preferred_element_type=BF16)
        row_group = (jnp.arange(m.shape[-2]) // rows_per_group) % gpc
        return jnp.where(row_group[:, None] == group[None, :], wide, jnp.zeros((), BF16))

    def per_chunk(m, order):
        x, _, _, a, b = m.shape
        m = m.reshape(x, n_l, N_CHUNK, gpc, a, b)
        return jnp.transpose(m, (1, 2) + tuple({"x": 0, "g": 3, "a": 4, "b": 5}[o] for o in order))

    swap = lambda m: jnp.swapaxes(m, -1, -2)
    b_in = per_chunk(jnp.concatenate([swap(bbr), swap(bbi)], axis=-1)[None], "gaxb")
    bcat = block_diag(b_in.reshape(n_l, N_CHUNK, CHUNK, 2 * SSM_STATE), SSM_GROUP, SSM_STATE,
                      steps=2)
    c_out = per_chunk(jnp.stack([swap(c_re), -swap(c_im)]), "xgab")
    ccat = block_diag(c_out.reshape(n_l, N_CHUNK, 2 * CHUNK_STATE, SSM_GROUP), SSM_STATE, SSM_GROUP)

    abr, abi = _cmul(pr[:TILE, ..., None], pi[:TILE, ..., None], bbr, bbi)
    dot_p = functools.partial(jnp.einsum, "lghp,jlgpi->jlgih", precision=exact)
    taps = dot_p(c_re, abr) - dot_p(c_im, abi)
    taps = jnp.concatenate([taps, jnp.zeros_like(taps[:1])])
    lag = jnp.arange(TILE)[None, :] - jnp.arange(TILE)[:, None]
    taps = taps[jnp.where(lag >= 0, lag, TILE)]
    taps = taps.reshape(TILE, TILE, n_l, N_CHUNK, gpc, SSM_GROUP, SSM_GROUP)
    taps = jnp.transpose(taps, (2, 3, 0, 4, 5, 1, 6))
    kcat = block_diag(taps.reshape(n_l, N_CHUNK, TILE * CHUNK, TILE * SSM_GROUP), SSM_GROUP,
                      SSM_GROUP, steps=TILE)

    ends = per_chunk(jnp.concatenate([swap(abr[::-1]), swap(abi[::-1])], axis=-1), "xgab")
    wst = block_diag(ends.reshape(n_l, N_CHUNK, TILE * CHUNK, 2 * SSM_STATE), SSM_GROUP, SSM_STATE,
                     steps=2)

    mr, mi = _cmul(pr[1:TILE + 1, :, :, None, :], pi[1:TILE + 1, :, :, None, :], c_re, c_im)
    outs = jnp.stack([per_chunk(m, "gbxa") for m in (mr, -mi)], axis=2)
    call = block_diag(outs.reshape(n_l, N_CHUNK, 2 * CHUNK_STATE, TILE * SSM_GROUP), SSM_STATE,
                      SSM_GROUP, steps=TILE)

    flat = lambda a: jnp.moveaxis(a, 0, 1).reshape(n_l, -1, N_STATE)
    rows = lax.broadcasted_iota(jnp.int32, (SUBLANES, N_STATE), 0)
    parts = []
    for a in (flat(pr[TILE:]), flat(pi[TILE:])):
        masked = lambda m, first: jnp.where(rows >= first, a[:, m - 1:m, :], 0.0)
        parts.append([masked(1, 1), masked(2, 2), masked(4, 4), a, masked(SUBLANES, 0)])
    tab = jnp.stack([p for pair in zip(*parts) for p in pair], axis=1)
    atab = jnp.stack([jnp.broadcast_to(flat(a[1:2]), (n_l, SUBLANES, N_STATE)) for a in (pr, pi)],
                     axis=1)
    return dict(bcat=bcat, ccat=ccat, kcat=kcat, wst=wst, call=call, tab=tab, atab=atab)


def _ssm_prompt_kernel(x_ref, sc_ref, sh_ref, kcat_ref, wst_ref, call_ref, d_ref, tab_ref,
                       g_ref, hre_ref, him_ref, h_ref, buf_ref, y_ref):
    n_tiles = SEQ // TILE
    half = CHUNK_STATE
    h_ref[...] = x_ref[...] * (1.0 + sc_ref[...]) + sh_ref[...]

    steps = [h_ref[pl.ds(r, n_tiles, stride=TILE), :].astype(BF16) for r in range(TILE)]
    tiles = jnp.concatenate(steps, axis=1)
    buf_ref[...] = jnp.dot(tiles, wst_ref[...], preferred_element_type=F32)
    y_tiles = jnp.dot(tiles, kcat_ref[...], preferred_element_type=F32)

    row = lax.broadcasted_iota(jnp.int32, (SUBLANES, half), 0)

    def tile_scan(i, carry):
        cr, ci = carry
        r0 = pl.multiple_of(i * SUBLANES, SUBLANES)
        vr = buf_ref[pl.ds(r0, SUBLANES), 0:half]
        vi = buf_ref[pl.ds(r0, SUBLANES), half:2 * half]
        for lvl, shift in enumerate((1, 2, 4)):
            pr, pi = _cmul(tab_ref[2 * lvl], tab_ref[2 * lvl + 1],
                           pltpu.roll(vr, shift, 0), pltpu.roll(vi, shift, 0))
            vr, vi = vr + pr, vi + pi
        pr, pi = _cmul(tab_ref[6], tab_ref[7], cr, ci)
        buf_ref[pl.ds(r0, SUBLANES), 0:half] = jnp.where(row == 0, cr, pltpu.roll(vr + pr, 1, 0))
        buf_ref[pl.ds(r0, SUBLANES), half:2 * half] = jnp.where(row == 0, ci,
                                                                 pltpu.roll(vi + pi, 1, 0))
        pr, pi = _cmul(tab_ref[8], tab_ref[9], cr, ci)
        last = SUBLANES - 1
        return (jnp.broadcast_to(vr[last:, :], vr.shape) + pr,
                jnp.broadcast_to(vi[last:, :], vi.shape) + pi)

    zero = jnp.zeros((SUBLANES, half), F32)
    cr, ci = lax.fori_loop(0, n_tiles // SUBLANES, tile_scan, (zero, zero))
    hre_ref[...] = cr[0:1, :]
    him_ref[...] = ci[0:1, :]

    y_tiles = y_tiles + jnp.dot(buf_ref[...].astype(BF16), call_ref[...],
                                preferred_element_type=F32)
    for s in range(TILE):
        y_ref[pl.ds(s, n_tiles, stride=TILE), :] = y_tiles[:, s * CHUNK:(s + 1) * CHUNK]

    y = y_ref[...] + d_ref[...] * h_ref[...]
    g_ref[...] = jax.nn.gelu(y).astype(g_ref.dtype)


def _ssm_prompt(x, mods, layer, p, d_skip):
    col = pl.BlockSpec((SEQ, CHUNK), lambda k, b: (b, k))
    fin = pl.BlockSpec((None, 1, CHUNK_STATE), lambda k, b: (b, 0, k))
    mod = lambda j: pl.BlockSpec((None, None, 1, CHUNK),
                                 lambda k, b: (layer, DEC_BATCH + b, 0, j * N_CHUNK + k))
    weight = lambda a: pl.BlockSpec((None, None) + a.shape[2:], lambda k, b: (layer, k, 0, 0))
    g, hre, him = pl.pallas_call(
        _ssm_prompt_kernel,
        grid=(N_CHUNK, BATCH),
        in_specs=[col, mod(1), mod(0), weight(p["kcat"]), weight(p["wst"]), weight(p["call"]),
                  pl.BlockSpec((None, 1, CHUNK), lambda k, b: (layer, 0, k)),
                  pl.BlockSpec((None, p["tab"].shape[1], SUBLANES, CHUNK_STATE),
                               lambda k, b: (layer, 0, 0, k))],
        out_specs=[col, fin, fin],
        out_shape=[jax.ShapeDtypeStruct((BATCH * SEQ, D_MODEL), BF16),
                   jax.ShapeDtypeStruct((BATCH, 1, N_STATE), F32),
                   jax.ShapeDtypeStruct((BATCH, 1, N_STATE), F32)],
        scratch_shapes=[pltpu.VMEM((SEQ, CHUNK), F32),
                        pltpu.VMEM((SEQ // TILE, 2 * CHUNK_STATE), F32),
                        pltpu.VMEM((SEQ, CHUNK), F32)],
        compiler_params=_params(("arbitrary", "arbitrary")),
        name="ssm_prompt",
    )(x, mods.arr, mods.arr, p["kcat"], p["wst"], p["call"], d_skip[:, None, :], p["tab"])
    shape = (BATCH, SSM_GROUPS, SSM_STATE)
    return g, hre.reshape(shape), him.reshape(shape)


def _ssm_sample_kernel(x_ref, sc_ref, sh_ref, bcat_ref, ccat_ref, d_ref, a_ref, h0r_ref, h0i_ref,
                       g_ref, hre_ref, him_ref):
    tile = lambda m: jnp.concatenate([m] * DEC_SEQ, axis=0)
    h = x_ref[...] * (1.0 + tile(sc_ref[...])) + tile(sh_ref[...])
    bu = jnp.dot(h.astype(BF16), bcat_ref[...], preferred_element_type=F32)
    ar = a_ref[0, 0:1, :]
    ai = a_ref[1, 0:1, :]
    sr = h0r_ref[...].T
    si = h0i_ref[...].T
    states = []
    for t in range(DEC_SEQ):
        rows = slice(t * DEC_BATCH, (t + 1) * DEC_BATCH)
        pr, pi = _cmul(ar, ai, sr, si)
        sr = pr + bu[rows, :CHUNK_STATE]
        si = pi + bu[rows, CHUNK_STATE:]
        states.append(jnp.concatenate([sr, si], axis=1))
    xs = jnp.concatenate(states, axis=0).astype(BF16)
    y = jnp.dot(xs, ccat_ref[...], preferred_element_type=F32) + d_ref[...] * h
    g_ref[...] = jax.nn.gelu(y).astype(g_ref.dtype)
    hre_ref[...] = sr.T
    him_ref[...] = si.T


def _ssm_sample(x, mods, layer, p, d_skip, h0_re, h0_im):
    n = DEC_BATCH * DEC_SEQ
    col = pl.BlockSpec((n, CHUNK), lambda k: (0, k))
    st = pl.BlockSpec((None, CHUNK_STATE, DEC_BATCH), lambda k: (layer, k, 0))
    mod = lambda j: pl.BlockSpec((None, DEC_BATCH, CHUNK), lambda k: (layer, 0, j * N_CHUNK + k))
    state_major = lambda a: jnp.transpose(a, (0, 2, 3, 1)).reshape(-1, N_STATE, DEC_BATCH)
    g, hre, him = pl.pallas_call(
        _ssm_sample_kernel,
        grid=(N_CHUNK,),
        in_specs=[col, mod(1), mod(0),
                  pl.BlockSpec((None, None, CHUNK, 2 * CHUNK_STATE), lambda k: (layer, k, 0, 0)),
                  pl.BlockSpec((None, None, 2 * CHUNK_STATE, CHUNK), lambda k: (layer, k, 0, 0)),
                  pl.BlockSpec((None, 1, CHUNK), lambda k: (layer, 0, k)),
                  pl.BlockSpec((None, 2, SUBLANES, CHUNK_STATE), lambda k: (layer, 0, 0, k)),
                  st, st],
        out_specs=[col] + [pl.BlockSpec((CHUNK_STATE, DEC_BATCH), lambda k: (k, 0))] * 2,
        out_shape=[jax.ShapeDtypeStruct((n, D_MODEL), BF16),
                   jax.ShapeDtypeStruct((N_STATE, DEC_BATCH), F32),
                   jax.ShapeDtypeStruct((N_STATE, DEC_BATCH), F32)],
        compiler_params=_params(("arbitrary",)),
        name="ssm_sample",
    )(x, mods.arr, mods.arr, p["bcat"], p["ccat"], d_skip[:, None, :], p["atab"],
      state_major(h0_re), state_major(h0_im))
    batch_major = lambda a: jnp.transpose(a.reshape(SSM_GROUPS, SSM_STATE, DEC_BATCH), (2, 0, 1))
    return g, batch_major(hre), batch_major(him)


def _rel_bucket(rel):
    n = jnp.maximum(rel, 0)
    max_exact = NUM_BUCKETS // 2
    large = max_exact + (jnp.log(jnp.maximum(n, 1).astype(F32) / max_exact)
                         / math.log(MAX_DISTANCE / max_exact) * (NUM_BUCKETS - max_exact)).astype(jnp.int32)
    large = jnp.minimum(large, NUM_BUCKETS - 1)
    return jnp.where(n < max_exact, n, large)


def _attn_scalars(attn_lam, subln_g, layer):
    lam_init = 0.8 - 0.6 * math.exp(-0.3 * layer)
    lam = (jnp.exp(jnp.sum(attn_lam[0] * attn_lam[1])) - jnp.exp(jnp.sum(attn_lam[2] * attn_lam[3]))
           + lam_init)
    lam_tile = jnp.full((SUBLANES, LANES), lam, F32)
    gain = (subln_g * (1.0 - lam_init)).reshape(1, V_DIM)
    return lam_tile, gain


def _bias_by_distance(rel_bias, n):
    bv = rel_bias[_rel_bucket(jnp.arange(n, dtype=jnp.int32))]
    return ((bv - rel_bias[NUM_BUCKETS - 1]) * LOG2E).T


def _softmax_update(s, m_prev):
    m_new = jnp.maximum(m_prev, jnp.max(s, axis=1, keepdims=True))
    p = jnp.exp2(s - jnp.concatenate([m_new] * (s.shape[1] // LANES), axis=1))
    return m_new, p.astype(BF16), jnp.exp2(m_prev - m_new)


def _with_ones(v):
    return jnp.concatenate([v, jnp.ones((v.shape[0], V_DIM), BF16)], axis=1)


def _acc_update(acc, alpha, p, vx):
    return (jnp.concatenate([alpha, alpha], axis=1) * acc
            + jnp.dot(p, vx, preferred_element_type=F32))


_NT = (((1,), (1,)), ((), ()))


def _attn_prompt_kernel(q_ref, k_ref, v_ref, tab_ref, lam_ref, gain_ref, o_ref,
                        q2_ref, m_ref, acc_ref, s_ref):
    i = pl.program_id(2)
    last = pl.num_programs(2) - 1

    def key_rows(j):
        return pl.ds(pl.multiple_of(j * TQ, TQ), TQ)

    def stack_queries(hh, tile):
        q = q_ref[key_rows(tile), hh * V_DIM:(hh + 1) * V_DIM]
        lane = lax.broadcasted_iota(jnp.int32, q.shape, 1)
        zero = jnp.zeros_like(q)
        q2_ref[hh, 0:TQ, :] = jnp.where(lane < HEAD_DIM, q, zero)
        q2_ref[hh, TQ:2 * TQ, :] = jnp.where(lane >= HEAD_DIM, q, zero)

    def logits(hh, j):
        cols = slice(hh * V_DIM, (hh + 1) * V_DIM)
        return lax.dot_general(q2_ref[hh], k_ref[key_rows(j), cols].astype(BF16), _NT,
                               preferred_element_type=F32)

    @pl.when(i == 0)
    def _():
        stack_queries(0, 0)
        s_ref[...] = logits(0, 0)

    for hh in range(1, HEADS_PER_STEP):
        stack_queries(hh, i)
    m_ref[...] = jnp.full(m_ref.shape, NEG_INF, F32)
    acc_ref[...] = jnp.zeros_like(acc_ref)

    def consume(hh, j, s, table):
        cols = slice(hh * V_DIM, (hh + 1) * V_DIM)
        if table is not None:
            bias = tab_ref[hh, table]
            s = s + jnp.concatenate([bias, bias], axis=0)
        m_new, p, alpha = _softmax_update(s, m_ref[hh])
        acc_ref[hh] = _acc_update(acc_ref[hh], alpha, p,
                                  _with_ones(v_ref[key_rows(j), cols].astype(BF16)))
        m_ref[hh] = m_new

    def step(j, table, same_tile):
        consume(0, j, s_ref[...], table)
        for hh in range(1, HEADS_PER_STEP):
            consume(hh, j, logits(hh, j), table)
        if same_tile:
            s_ref[...] = logits(0, j + 1)
        else:
            stack_queries(0, jnp.minimum(i + 1, last))
            s_ref[...] = logits(0, 0)

    def far(j, c):
        step(j, None, True)
        return c

    lax.fori_loop(0, jnp.maximum(i - 1, 0), far, 0)

    @pl.when(i >= 1)
    def _():
        step(i - 1, 1, True)

    step(i, 0, False)

    for hh in range(HEADS_PER_STEP):
        acc = acc_ref[hh]
        o = acc[:, :V_DIM] / acc[:, V_DIM:]
        o = o[0:TQ, :] - lam_ref[0:1, 0:1] * o[TQ:2 * TQ, :]
        o = o * lax.rsqrt(jnp.mean(o * o, axis=-1, keepdims=True) + LN_EPS) * gain_ref[...]
        o_ref[:, hh * V_DIM:(hh + 1) * V_DIM] = o.astype(o_ref.dtype)


def _attn_prompt_tables(rel_bias):
    bv = _bias_by_distance(rel_bias, 2 * TQ)
    masked = jnp.full((N_HEADS, TQ), NEG_INF, F32)
    diag = jnp.concatenate([bv[:, 0:1], masked, bv[:, TQ - 1:0:-1]], axis=1)
    prev = jnp.concatenate([bv[:, TQ:0:-1], bv[:, TQ:TQ + 1], bv[:, 2 * TQ - 1:TQ:-1]], axis=1)
    vec = jnp.broadcast_to(jnp.stack([diag, prev], axis=1)[:, :, None, :],
                           (N_HEADS, 2, SUBLANES, 2 * TQ))
    return pl.pallas_call(
        _toeplitz_kernel,
        grid=(N_HEADS,),
        in_specs=[pl.BlockSpec((None, 2, SUBLANES, 2 * TQ), lambda h: (h, 0, 0, 0))],
        out_specs=pl.BlockSpec((None, 2, TQ, TQ), lambda h: (h, 0, 0, 0)),
        out_shape=jax.ShapeDtypeStruct((N_HEADS, 2, TQ, TQ), F32),
        compiler_params=_params(("arbitrary",)),
        name="bias_tiles",
    )(vec)


def _toeplitz_kernel(vec_ref, o_ref):
    for t in range(o_ref.shape[0]):
        rows = jnp.broadcast_to(vec_ref[t, 0:1, :], (TQ, 2 * TQ))
        o_ref[t] = pltpu.roll(rows, 0, 1, stride=1, stride_axis=0)[:, :TQ]


def _attn_prompt(q, k, v, tab, lam_tile, gain):
    n_q = SEQ // TQ
    width = HEADS_PER_STEP * V_DIM
    shape3 = (BATCH, SEQ, D_MODEL)
    qspec = pl.BlockSpec((None, TQ, width), lambda b, h, i: (b, i, h))
    kvspec = pl.BlockSpec((None, SEQ, width), lambda b, h, i: (b, 0, h))
    o = pl.pallas_call(
        _attn_prompt_kernel,
        grid=(BATCH, N_HEADS // HEADS_PER_STEP, n_q),
        in_specs=[kvspec, kvspec, kvspec,
                  pl.BlockSpec((HEADS_PER_STEP, 2, TQ, TQ), lambda b, h, i: (h, 0, 0, 0)),
                  pl.BlockSpec((SUBLANES, LANES), lambda b, h, i: (0, 0)),
                  pl.BlockSpec((1, V_DIM), lambda b, h, i: (0, 0))],
        out_specs=qspec,
        out_shape=jax.ShapeDtypeStruct(shape3, BF16),
        scratch_shapes=[pltpu.VMEM((HEADS_PER_STEP, 2 * TQ, V_DIM), BF16),
                        pltpu.VMEM((HEADS_PER_STEP, 2 * TQ, LANES), F32),
                        pltpu.VMEM((HEADS_PER_STEP, 2 * TQ, 2 * V_DIM), F32),
                        pltpu.VMEM((2 * TQ, TQ), F32)],
        compiler_params=_params(("arbitrary", "arbitrary", "arbitrary")),
        name="attn_prompt",
    )(q.reshape(shape3), k.reshape(shape3), v.reshape(shape3), tab, lam_tile, gain)
    return o.reshape(BATCH * SEQ, D_MODEL)


N_SROWS = N_HEADS * 2 * DEC_SEQ


def _attn_sample_kernel(pt_ref, q_ref, qmask_ref, knew_ref, vnew_ref, bias_ref, bnew_ref,
                        lam_ref, gain_ref, *rest):
    del pt_ref
    k_refs = rest[:PAGES_PER_STEP]
    v_refs = rest[PAGES_PER_STEP:2 * PAGES_PER_STEP]
    o_ref, qx_ref, m_ref, acc_ref = rest[2 * PAGES_PER_STEP:]
    step = pl.program_id(1)
    rows_per_head = 2 * DEC_SEQ

    @pl.when(step == 0)
    def _():
        q = q_ref[...].astype(F32)
        q = jnp.concatenate([q, q], axis=0)
        qx_ref[...] = (jnp.concatenate([q] * N_HEADS, axis=0) * qmask_ref[...]).astype(BF16)
        m_ref[...] = jnp.full(m_ref.shape, NEG_INF, F32)
        acc_ref[...] = jnp.zeros_like(acc_ref)

    def update(s, values_of_head):
        m_new, p, alpha = _softmax_update(s, m_ref[...])
        for h in range(N_HEADS):
            r = slice(h * rows_per_head, (h + 1) * rows_per_head)
            acc_ref[r, :] = _acc_update(acc_ref[r, :], alpha[r, :], p[r, :],
                                        _with_ones(values_of_head(h)))
        m_ref[...] = m_new

    qx = qx_ref[...]
    kt = jnp.concatenate([r[...].astype(BF16) for r in k_refs], axis=1)
    s = jnp.dot(qx, kt, preferred_element_type=F32) + bias_ref[...]
    update(s, lambda h: jnp.concatenate(
        [r[pl.ds(h, PAGE_SIZE, stride=N_HEADS), :].astype(BF16) for r in v_refs], axis=0))

    @pl.when(step == pl.num_programs(1) - 1)
    def _():
        pad = jnp.zeros((LANES - SUBLANES, D_MODEL), BF16)
        kn = jnp.concatenate([knew_ref[...], pad], axis=0)
        vn = jnp.concatenate([vnew_ref[...], pad], axis=0)
        sn = lax.dot_general(qx, kn, _NT, preferred_element_type=F32) + bnew_ref[...]
        update(sn, lambda h: vn[:, h * V_DIM:(h + 1) * V_DIM])
        acc = acc_ref[...]
        o = acc[:, :V_DIM] / acc[:, V_DIM:]
        heads = []
        for h in range(N_HEADS):
            r0 = h * rows_per_head
            oh = o[r0:r0 + DEC_SEQ, :] - lam_ref[0:1, 0:1] * o[r0 + DEC_SEQ:r0 + rows_per_head, :]
            oh = oh * lax.rsqrt(jnp.mean(oh * oh, axis=-1, keepdims=True) + LN_EPS) * gain_ref[...]
            heads.append(oh)
        o_ref[...] = jnp.concatenate(heads, axis=1)


def _attn_sample_tables(rel_bias):
    bv = _bias_by_distance(rel_bias, PAST_LEN + DEC_SEQ)
    far_first = bv[:, ::-1]
    past = jnp.stack([far_first[:, DEC_SEQ - 1 - t:DEC_SEQ - 1 - t + PAST_LEN]
                      for t in range(DEC_SEQ)], axis=1)
    rel = (jnp.arange(DEC_SEQ, dtype=jnp.int32)[:, None]
           - jnp.arange(LANES, dtype=jnp.int32)[None, :])
    new = jnp.where(rel >= 0, bv[:, jnp.clip(rel, 0, DEC_SEQ - 1)], NEG_INF)
    rows = lambda a: jnp.broadcast_to(a[:, None], (N_HEADS, 2) + a.shape[1:]).reshape(
        N_SROWS, a.shape[-1])
    col = jnp.arange(D_MODEL, dtype=jnp.int32)[None, :]
    row = jnp.arange(N_SROWS, dtype=jnp.int32)[:, None]
    qmask = ((col // V_DIM == row // (2 * DEC_SEQ))
             & ((col % V_DIM) // HEAD_DIM == (row // DEC_SEQ) % 2)).astype(F32)
    return rows(past), rows(new), qmask


def _attn_sample(q, k_new, v_new, cache_k, cache_v, page_table, tables, lam_tile, gain):
    past, new, qmask = tables
    n_phys = cache_k.shape[0]
    ck = jnp.transpose(cache_k, (0, 2, 3, 4, 1)).reshape(n_phys, D_MODEL, PAGE_SIZE)
    cv = cache_v.reshape(n_phys, PAGE_SIZE * N_HEADS, V_DIM)
    n_steps = N_PAGES // PAGES_PER_STEP
    keys_per_step = PAGES_PER_STEP * PAGE_SIZE

    def page_spec(r):
        return pl.BlockSpec((None, D_MODEL, PAGE_SIZE),
                            lambda b, s, pt: (pt[b * N_PAGES + s * PAGES_PER_STEP + r], 0, 0))

    per_seq = lambda rows: pl.BlockSpec((None, rows, D_MODEL), lambda b, s, pt: (b, 0, 0))
    const = lambda shape: pl.BlockSpec(shape, lambda b, s, pt: (0,) * len(shape))
    grid_spec = pltpu.PrefetchScalarGridSpec(
        num_scalar_prefetch=1,
        grid=(DEC_BATCH, n_steps),
        in_specs=[per_seq(DEC_SEQ), const(qmask.shape), per_seq(SUBLANES), per_seq(SUBLANES),
                  pl.BlockSpec((N_SROWS, keys_per_step), lambda b, s, pt: (0, s)),
                  const(new.shape), const(lam_tile.shape), const(gain.shape)]
                 + [page_spec(r) for r in range(PAGES_PER_STEP)] * 2,
        out_specs=per_seq(DEC_SEQ),
        scratch_shapes=[pltpu.VMEM((N_SROWS, D_MODEL), BF16), pltpu.VMEM((N_SROWS, LANES), F32),
                        pltpu.VMEM((N_SROWS, 2 * V_DIM), F32)],
    )
    return pl.pallas_call(
        _attn_sample_kernel,
        grid_spec=grid_spec,
        out_shape=jax.ShapeDtypeStruct((DEC_BATCH, DEC_SEQ, D_MODEL), F32),
        compiler_params=_params(("arbitrary", "arbitrary")),
        name="attn_sample",
    )(page_table.reshape(-1), q, qmask, k_new, v_new, past, new, lam_tile, gain,
      *([ck] * PAGES_PER_STEP), *([cv] * PAGES_PER_STEP))


def _to_batch_major(a):
    return jnp.transpose(a.reshape(DEC_SEQ, DEC_BATCH, D_MODEL), (1, 0, 2))


def _to_time_major(a):
    return jnp.transpose(a, (1, 0, 2)).reshape(DEC_SEQ * DEC_BATCH, D_MODEL)


def _trunk(x, mods, mods_kv, h0_re, h0_im, attend, w):
    ssm_re, ssm_im = [], []
    k = v = None
    for layer in range(DEPTH):
        if layer < N_A_LAYERS:
            if mods.mode == "p":
                g, fr, fi = _ssm_prompt(x, mods, layer, w["ssm"], w["ssm_d"])
            else:
                g, fr, fi = _ssm_sample(x, mods, layer, w["ssm"], w["ssm_d"], h0_re, h0_im)
            ssm_re.append(fr)
            ssm_im.append(fi)
            mixed, w_mix = g, w["glu"]
        else:
            j = layer - N_A_LAYERS
            (q,) = _mod_linear(x, mods, layer, 1, 0, w["wq"][j], [(0, BF16, False)],
                               scale=HEAD_DIM ** -0.5 * LOG2E)
            mixed, w_mix = attend(q, k, v, j), w["wo"]
        x = _block_tail(mixed, x, mods, layer, w_mix, w["w1"], w["w2"], w["ln_g"], w["ln_b"],
                        glu=layer < N_A_LAYERS)
        if layer == N_A_LAYERS - 1:
            outs = [(0, F32, False), (1, F32, False)] + [(0, F32, True)] * (mods.mode == "p")
            k, v, *k_out = _mod_linear(x, mods_kv, 0, 1, 0, w["wkv"], outs)
    return x, jnp.stack(ssm_re), jnp.stack(ssm_im), (k_out or [k])[0], v


def kernel(x_prompt, x_sample, state_ssm_re, state_ssm_im, cache_k, cache_v, page_table, c_prompt, c_sample, rel_bias, w_ada, b_ada, ln_g, ln_b, ssm_lam_re, ssm_lam_im, ssm_log_dt, ssm_b_re, ssm_b_im, ssm_c_re, ssm_c_im, ssm_d, ssm_w_glu_a, ssm_w_glu_b, w_ada_kv, b_ada_kv, w_kv, attn_w_q, attn_lam, attn_subln_g, attn_w_o, mlp_w1, mlp_w2):
    w = {
        "ssm": _ssm_prep(ssm_lam_re, ssm_lam_im, ssm_log_dt, ssm_b_re, ssm_b_im, ssm_c_re, ssm_c_im),
        "ssm_d": ssm_d,
        "glu": jnp.concatenate([ssm_w_glu_a, ssm_w_glu_b], axis=-1).astype(BF16),
        "wq": attn_w_q.astype(BF16),
        "wo": attn_w_o.astype(BF16),
        "wkv": w_kv.astype(BF16),
        "w1": mlp_w1.astype(BF16),
        "w2": mlp_w2.astype(BF16),
        "ln_g": ln_g,
        "ln_b": ln_b,
    }
    c_all = jnp.concatenate([c_sample, c_prompt, jnp.zeros((BATCH, D_MODEL), F32)], axis=0)
    ada = _ada(c_all, w_ada, b_ada)
    ada_kv = _ada(c_all, w_ada_kv[None], b_ada_kv[None])
    scalars = [_attn_scalars(attn_lam[j], attn_subln_g[j], N_A_LAYERS + j)
               for j in range(DEPTH - N_A_LAYERS)]

    prompt_tab = _attn_prompt_tables(rel_bias)

    def attend_prompt(q, k, v, j):
        return _attn_prompt(q, k, v, prompt_tab, *scalars[j])

    y_p, re_p, im_p, k_p, v_p = _trunk(x_prompt.reshape(BATCH * SEQ, D_MODEL), _Mods(ada, "p"),
                                       _Mods(ada_kv, "p"), None, None, attend_prompt, w)

    sample_tab = _attn_sample_tables(rel_bias)

    def attend_sample(q, k, v, j):
        pad = lambda a: jnp.pad(_to_batch_major(a).astype(BF16),
                                ((0, 0), (0, SUBLANES - DEC_SEQ), (0, 0)))
        o = _attn_sample(_to_batch_major(q), pad(k), pad(v), cache_k, cache_v, page_table,
                         sample_tab, *scalars[j])
        return _to_time_major(o).astype(BF16)

    y_s, re_s, im_s, k_s, v_s = _trunk(_to_time_major(x_sample), _Mods(ada, "s"), _Mods(ada_kv, "s"),
                                       state_ssm_re, state_ssm_im, attend_sample, w)

    return (y_p.reshape(BATCH, SEQ, D_MODEL), _to_batch_major(y_s), re_p, im_p,
            jnp.transpose(k_p.reshape(BATCH, N_HEADS, 2, HEAD_DIM, SEQ), (0, 4, 1, 2, 3)),
            v_p.reshape(BATCH, SEQ, N_HEADS, V_DIM),
            re_s, im_s,
            _to_batch_major(k_s).reshape(DEC_BATCH, DEC_SEQ, N_HEADS, 2, HEAD_DIM),
            _to_batch_major(v_s).reshape(DEC_BATCH, DEC_SEQ, N_HEADS, V_DIM))
```

```python
import functools
import math

import jax
import jax.numpy as jnp
from jax import lax
from jax.experimental import pallas as pl
from jax.experimental.pallas import tpu as pltpu

F32 = jnp.float32
BF16 = jnp.bfloat16

D_MODEL = 1024
BATCH = 4
SEQ = 4096
DEPTH = 4
DEC_BATCH = 128
DEC_SEQ = 4
PAST_LEN = 2048
PAGE_SIZE = 128
N_PAGES = PAST_LEN // PAGE_SIZE
N_A_LAYERS = DEPTH // 2
SSM_GROUP = 16
SSM_GROUPS = D_MODEL // SSM_GROUP
SSM_STATE = 64
N_STATE = SSM_GROUPS * SSM_STATE
N_HEADS = 8
HEAD_DIM = D_MODEL // (2 * N_HEADS)
V_DIM = 2 * HEAD_DIM
D_FF = 4 * D_MODEL
NUM_BUCKETS = 32
MAX_DISTANCE = 128
N_MOD = 6
ALPHA = (2.0 * DEPTH) ** 0.25
LN_EPS = 1e-5
NEG_INF = -1e30

SUBLANES = 8
LANES = 128
VMEM_LIMIT = 48 * 1024 * 1024

MOD_ROWS = DEC_BATCH + 2 * BATCH
CHUNK = LANES
N_CHUNK = D_MODEL // CHUNK
CHUNK_STATE = N_STATE // N_CHUNK
TM_PROMPT = 512
TILE = SUBLANES
T_SSM = 512
TQ = 512
HEADS_PER_STEP = 2
PAGES_PER_STEP = 16
LOG2E = math.log2(math.e)
FF_CHUNK = 1024


def _params(sem, vmem=VMEM_LIMIT):
    return pltpu.CompilerParams(dimension_semantics=sem, vmem_limit_bytes=vmem)


def _layer_norm(z, g, b):
    mu = jnp.mean(z, axis=-1, keepdims=True)
    zc = z - mu
    var = jnp.mean(zc * zc, axis=-1, keepdims=True)
    return zc * lax.rsqrt(var + LN_EPS) * g + b


def _ada_kernel(c_ref, w_ref, b_ref, o_ref):
    c = c_ref[...]
    sc = (c * jax.nn.sigmoid(c)).astype(BF16)
    w = w_ref[...].astype(BF16)
    o_ref[...] = jnp.dot(sc, w, preferred_element_type=F32) + b_ref[...]


def _ada(c_all, w, b):
    n_l, _, width = w.shape
    tn = 1024
    return pl.pallas_call(
        _ada_kernel,
        grid=(n_l, width // tn),
        in_specs=[
            pl.BlockSpec((MOD_ROWS, D_MODEL), lambda l, j: (0, 0)),
            pl.BlockSpec((None, D_MODEL, tn), lambda l, j: (l, 0, j)),
            pl.BlockSpec((None, 1, tn), lambda l, j: (l, 0, j)),
        ],
        out_specs=pl.BlockSpec((None, MOD_ROWS, tn), lambda l, j: (l, 0, j)),
        out_shape=jax.ShapeDtypeStruct((n_l, MOD_ROWS, width), F32),
        compiler_params=_params(("arbitrary", "arbitrary")),
        name="ada_mod",
    )(c_all, w, b.reshape(n_l, 1, width))


class _Mods:
    def __init__(self, arr, mode):
        self.mode = mode
        self.arr = arr if mode == "s" else arr.reshape(arr.shape[0], MOD_ROWS, 1, arr.shape[-1])

    def spec(self, layer, j, batch_of):
        if self.mode == "s":
            return pl.BlockSpec((None, DEC_BATCH, D_MODEL), lambda *g: (layer, 0, j))
        return pl.BlockSpec((None, None, 1, D_MODEL),
                            lambda *g: (layer, DEC_BATCH + batch_of(*g), 0, j))


def _row_tiling(mode):
    if mode == "s":
        return DEC_BATCH, lambda i: 0
    tiles_per_batch = SEQ // TM_PROMPT
    return TM_PROMPT, lambda i: i // tiles_per_batch


def _const_spec(shape):
    zeros = (0,) * len(shape)
    return pl.BlockSpec(shape, lambda *g: zeros, pipeline_mode=pl.Buffered(1))


def _mod_linear_kernel(x_ref, sc_ref, sh_ref, w_ref, *o_refs, scale, blocks):
    h = (x_ref[...] * (1.0 + sc_ref[...]) + sh_ref[...]).astype(BF16)
    y = jnp.dot(h, w_ref[...], preferred_element_type=F32)
    if scale != 1.0:
        y = y * scale
    for (n, transposed), o_ref in zip(blocks, o_refs):
        block = y[:, n * D_MODEL:(n + 1) * D_MODEL].astype(o_ref.dtype)
        o_ref[...] = block.T if transposed else block


def _mod_linear(x, mods, layer, j_scale, j_shift, w, outs, scale=1.0):
    n = x.shape[0]
    tm, batch_of = _row_tiling(mods.mode)
    row = pl.BlockSpec((tm, D_MODEL), lambda i: (i, 0))
    tiles_per_batch = SEQ // tm
    col = pl.BlockSpec((None, D_MODEL, tm), lambda i: (i // tiles_per_batch, 0, i % tiles_per_batch))
    return pl.pallas_call(
        functools.partial(_mod_linear_kernel, scale=scale,
                          blocks=tuple((o[0], o[2]) for o in outs)),
        grid=(n // tm,),
        in_specs=[row, mods.spec(layer, j_scale, batch_of), mods.spec(layer, j_shift, batch_of),
                  _const_spec(w.shape)],
        out_specs=[col if o[2] else row for o in outs],
        out_shape=[jax.ShapeDtypeStruct((BATCH, D_MODEL, SEQ) if o[2] else (n, D_MODEL), o[1])
                   for o in outs],
        compiler_params=_params(("arbitrary",)),
        name="mod_linear",
    )(x, mods.arr, mods.arr, w)


def _block_tail_kernel(a_ref, x_ref, gm_ref, sc_ref, sh_ref, gf_ref, wm_ref, w1_ref, w2_ref,
                       g_ref, b_ref, o_ref, *, glu):
    y = jnp.dot(a_ref[...], wm_ref[...], preferred_element_type=F32)
    if glu:
        y = y[:, :D_MODEL] * jax.nn.sigmoid(y[:, D_MODEL:])
    x = _layer_norm(ALPHA * x_ref[...] + (1.0 + gm_ref[...]) * y, g_ref[0:1, :], b_ref[0:1, :])
    h = (x * (1.0 + sc_ref[...]) + sh_ref[...]).astype(BF16)
    acc = jnp.zeros(x.shape, F32)
    for c in range(D_FF // FF_CHUNK):
        cols = slice(c * FF_CHUNK, (c + 1) * FF_CHUNK)
        hid = jnp.dot(h, w1_ref[:, cols], preferred_element_type=F32)
        hid = jnp.square(jnp.maximum(hid, 0.0)).astype(BF16)
        acc = acc + jnp.dot(hid, w2_ref[cols, :], preferred_element_type=F32)
    z = ALPHA * x + (1.0 + gf_ref[...]) * acc
    o_ref[...] = _layer_norm(z, g_ref[1:2, :], b_ref[1:2, :])


def _block_tail(a, x, mods, layer, w_mix, w1, w2, ln_g, ln_b, glu):
    n = x.shape[0]
    tm, batch_of = _row_tiling(mods.mode)
    row = pl.BlockSpec((tm, D_MODEL), lambda i: (i, 0))
    ln = pl.BlockSpec((None, 2, D_MODEL), lambda i: (layer, 0, 0))
    weight = lambda w: pl.BlockSpec((None,) + w.shape[1:], lambda i: (layer % w.shape[0], 0, 0),
                                    pipeline_mode=pl.Buffered(1))
    return pl.pallas_call(
        functools.partial(_block_tail_kernel, glu=glu),
        grid=(n // tm,),
        in_specs=[row, row] + [mods.spec(layer, j, batch_of) for j in (2, 4, 3, 5)]
                 + [weight(w_mix), weight(w1), weight(w2), ln, ln],
        out_specs=row,
        out_shape=jax.ShapeDtypeStruct((n, D_MODEL), F32),
        compiler_params=_params(("arbitrary",)),
        name="block_tail",
    )(a, x, mods.arr, mods.arr, mods.arr, mods.arr, w_mix, w1, w2, ln_g, ln_b)


def _cmul(ar, ai, br, bi):
    return ar * br - ai * bi, ar * bi + ai * br


def _ssm_prep(lam_re, lam_im, log_dt, b_re, b_im, c_re, c_im):
    n_l = lam_re.shape[0]
    dt = jnp.exp(log_dt)[..., None]
    ns = list(range(TILE + 1)) + [TILE * m for m in range(2, SUBLANES + 1)]
    n = jnp.asarray(ns, F32).reshape(-1, 1, 1, 1)
    mag = jnp.exp(n * (lam_re * dt))
    pr = mag * jnp.cos(n * (lam_im * dt))
    pi = mag * jnp.sin(n * (lam_im * dt))
    ar, ai = pr[1], pi[1]
    den = lam_re * lam_re + lam_im * lam_im
    er = ((ar - 1.0) * lam_re + ai * lam_im) / den
    ei = (ai * lam_re - (ar - 1.0) * lam_im) / den
    bbr = er[..., None] * b_re - ei[..., None] * b_im
    bbi = er[..., None] * b_im + ei[..., None] * b_re
    gpc = CHUNK // SSM_GROUP
    exact = lax.Precision.HIGHEST

    def block_diag(m, rows_per_group, cols_per_group, steps=1):
        y = steps * cols_per_group
        c = jnp.arange(steps * gpc * cols_per_group)
        step, group, inner = c // (gpc * cols_per_group), (c // cols_per_group) % gpc, c % cols_per_group
        select = (jnp.arange(y)[:, None] == (step * cols_per_group + inner)[None, :]).astype(BF16)
        wide = jnp.einsum("...ry,yc->...rc", m.astype(BF16), select, preferred_element_type=BF16)
        row_group = (jnp.arange(m.shape[-2]) // rows_per_group) % gpc
        return jnp.where(row_group[:, None] == group[None, :], wide, jnp.zeros((), BF16))

    def per_chunk(m, order):
        x, _, _, a, b = m.shape
        m = m.reshape(x, n_l, N_CHUNK, gpc, a, b)
        return jnp.transpose(m, (1, 2) + tuple({"x": 0, "g": 3, "a": 4, "b": 5}[o] for o in order))

    swap = lambda m: jnp.swapaxes(m, -1, -2)
    b_in = per_chunk(jnp.concatenate([swap(bbr), swap(bbi)], axis=-1)[None], "gaxb")
    bcat = block_diag(b_in.reshape(n_l, N_CHUNK, CHUNK, 2 * SSM_STATE), SSM_GROUP, SSM_STATE,
                      steps=2)
    c_out = per_chunk(jnp.stack([swap(c_re), -swap(c_im)]), "xgab")
    ccat = block_diag(c_out.reshape(n_l, N_CHUNK, 2 * CHUNK_STATE, SSM_GROUP), SSM_STATE, SSM_GROUP)

    abr, abi = _cmul(pr[:TILE, ..., None], pi[:TILE, ..., None], bbr, bbi)
    dot_p = functools.partial(jnp.einsum, "lghp,jlgpi->jlgih", precision=exact)
    taps = per_chunk(dot_p(c_re, abr) - dot_p(c_im, abi), "xgab")
    kcat = block_diag(taps.reshape(n_l, N_CHUNK, TILE * CHUNK, SSM_GROUP), SSM_GROUP, SSM_GROUP)

    ends = per_chunk(jnp.concatenate([swap(abr[::-1]), swap(abi[::-1])], axis=-1), "xgab")
    wst = block_diag(ends.reshape(n_l, N_CHUNK, TILE * CHUNK, 2 * SSM_STATE), SSM_GROUP, SSM_STATE,
                     steps=2)

    mr, mi = _cmul(pr[1:TILE + 1, :, :, None, :], pi[1:TILE + 1, :, :, None, :], c_re, c_im)
    outs = jnp.stack([per_chunk(m, "gbxa") for m in (mr, -mi)], axis=2)
    call = block_diag(outs.reshape(n_l, N_CHUNK, 2 * CHUNK_STATE, TILE * SSM_GROUP), SSM_STATE,
                      SSM_GROUP, steps=TILE)

    flat = lambda a: jnp.moveaxis(a, 0, 1).reshape(n_l, -1, N_STATE)
    rows = lax.broadcasted_iota(jnp.int32, (SUBLANES, N_STATE), 0)
    parts = []
    for a in (flat(pr[TILE:]), flat(pi[TILE:])):
        masked = lambda m, first: jnp.where(rows >= first, a[:, m - 1:m, :], 0.0)
        parts.append([masked(1, 1), masked(2, 2), masked(4, 4), a, masked(SUBLANES, 0)])
    tab = jnp.stack([p for pair in zip(*parts) for p in pair], axis=1)
    atab = jnp.stack([jnp.broadcast_to(flat(a[1:2]), (n_l, SUBLANES, N_STATE)) for a in (pr, pi)],
                     axis=1)
    return dict(bcat=bcat, ccat=ccat, kcat=kcat, wst=wst, call=call, tab=tab, atab=atab)


def _ssm_prompt_kernel(x_ref, sc_ref, sh_ref, kcat_ref, wst_ref, call_ref, d_ref, tab_ref,
                       g_ref, hre_ref, him_ref, h_ref, buf_ref, y_ref, taps_ref):
    n_tiles = SEQ // TILE
    half = CHUNK_STATE

    @pl.when(pl.program_id(1) == 0)
    def _():
        zero = jnp.zeros((CHUNK, CHUNK), BF16)
        for r in range(TILE):
            for s in range(TILE):
                lag = s - r
                block = kcat_ref[lag * CHUNK:(lag + 1) * CHUNK, :] if lag >= 0 else zero
                taps_ref[r * CHUNK:(r + 1) * CHUNK, s * CHUNK:(s + 1) * CHUNK] = block

    h_ref[...] = x_ref[...] * (1.0 + sc_ref[...]) + sh_ref[...]

    steps = [h_ref[pl.ds(r, n_tiles, stride=TILE), :].astype(BF16) for r in range(TILE)]
    tiles = jnp.concatenate(steps, axis=1)
    buf_ref[...] = jnp.dot(tiles, wst_ref[...], preferred_element_type=F32)
    y_tiles = jnp.dot(tiles, taps_ref[...], preferred_element_type=F32)

    row = lax.broadcasted_iota(jnp.int32, (SUBLANES, half), 0)

    def tile_scan(i, carry):
        cr, ci = carry
        r0 = pl.multiple_of(i * SUBLANES, SUBLANES)
        vr = buf_ref[pl.ds(r0, SUBLANES), 0:half]
        vi = buf_ref[pl.ds(r0, SUBLANES), half:2 * half]
        for lvl, shift in enumerate((1, 2, 4)):
            pr, pi = _cmul(tab_ref[2 * lvl], tab_ref[2 * lvl + 1],
                           pltpu.roll(vr, shift, 0), pltpu.roll(vi, shift, 0))
            vr, vi = vr + pr, vi + pi
        pr, pi = _cmul(tab_ref[6], tab_ref[7], cr, ci)
        buf_ref[pl.ds(r0, SUBLANES), 0:half] = jnp.where(row == 0, cr, pltpu.roll(vr + pr, 1, 0))
        buf_ref[pl.ds(r0, SUBLANES), half:2 * half] = jnp.where(row == 0, ci,
                                                                 pltpu.roll(vi + pi, 1, 0))
        pr, pi = _cmul(tab_ref[8], tab_ref[9], cr, ci)
        last = SUBLANES - 1
        return (jnp.broadcast_to(vr[last:, :], vr.shape) + pr,
                jnp.broadcast_to(vi[last:, :], vi.shape) + pi)

    zero = jnp.zeros((SUBLANES, half), F32)
    cr, ci = lax.fori_loop(0, n_tiles // SUBLANES, tile_scan, (zero, zero))
    hre_ref[...] = cr[0:1, :]
    him_ref[...] = ci[0:1, :]

    y_tiles = y_tiles + jnp.dot(buf_ref[...].astype(BF16), call_ref[...],
                                preferred_element_type=F32)
    for s in range(TILE):
        y_ref[pl.ds(s, n_tiles, stride=TILE), :] = y_tiles[:, s * CHUNK:(s + 1) * CHUNK]

    y = y_ref[...] + d_ref[...] * h_ref[...]
    g_ref[...] = jax.nn.gelu(y).astype(g_ref.dtype)


def _ssm_prompt(x, mods, layer, p, d_skip):
    col = pl.BlockSpec((SEQ, CHUNK), lambda k, b: (b, k))
    fin = pl.BlockSpec((None, 1, CHUNK_STATE), lambda k, b: (b, 0, k))
    mod = lambda j: pl.BlockSpec((None, None, 1, CHUNK),
                                 lambda k, b: (layer, DEC_BATCH + b, 0, j * N_CHUNK + k))
    weight = lambda a: pl.BlockSpec((None, None) + a.shape[2:], lambda k, b: (layer, k, 0, 0))
    g, hre, him = pl.pallas_call(
        _ssm_prompt_kernel,
        grid=(N_CHUNK, BATCH),
        in_specs=[col, mod(1), mod(0), weight(p["kcat"]), weight(p["wst"]), weight(p["call"]),
                  pl.BlockSpec((None, 1, CHUNK), lambda k, b: (layer, 0, k)),
                  pl.BlockSpec((None, p["tab"].shape[1], SUBLANES, CHUNK_STATE),
                               lambda k, b: (layer, 0, 0, k))],
        out_specs=[col, fin, fin],
        out_shape=[jax.ShapeDtypeStruct((BATCH * SEQ, D_MODEL), BF16),
                   jax.ShapeDtypeStruct((BATCH, 1, N_STATE), F32),
                   jax.ShapeDtypeStruct((BATCH, 1, N_STATE), F32)],
        scratch_shapes=[pltpu.VMEM((SEQ, CHUNK), F32),
                        pltpu.VMEM((SEQ // TILE, 2 * CHUNK_STATE), F32),
                        pltpu.VMEM((SEQ, CHUNK), F32),
                        pltpu.VMEM((TILE * CHUNK, TILE * CHUNK), BF16)],
        compiler_params=_params(("arbitrary", "arbitrary")),
        name="ssm_prompt",
    )(x, mods.arr, mods.arr, p["kcat"], p["wst"], p["call"], d_skip[:, None, :], p["tab"])
    shape = (BATCH, SSM_GROUPS, SSM_STATE)
    return g, hre.reshape(shape), him.reshape(shape)


def _ssm_sample_kernel(x_ref, sc_ref, sh_ref, bcat_ref, ccat_ref, d_ref, a_ref, h0r_ref, h0i_ref,
                       g_ref, hre_ref, him_ref):
    tile = lambda m: jnp.concatenate([m] * DEC_SEQ, axis=0)
    h = x_ref[...] * (1.0 + tile(sc_ref[...])) + tile(sh_ref[...])
    bu = jnp.dot(h.astype(BF16), bcat_ref[...], preferred_element_type=F32)
    ar = a_ref[0, 0:1, :]
    ai = a_ref[1, 0:1, :]
    sr = h0r_ref[...].T
    si = h0i_ref[...].T
    states = []
    for t in range(DEC_SEQ):
        rows = slice(t * DEC_BATCH, (t + 1) * DEC_BATCH)
        pr, pi = _cmul(ar, ai, sr, si)
        sr = pr + bu[rows, :CHUNK_STATE]
        si = pi + bu[rows, CHUNK_STATE:]
        states.append(jnp.concatenate([sr, si], axis=1))
    xs = jnp.concatenate(states, axis=0).astype(BF16)
    y = jnp.dot(xs, ccat_ref[...], preferred_element_type=F32) + d_ref[...] * h
    g_ref[...] = jax.nn.gelu(y).astype(g_ref.dtype)
    hre_ref[...] = sr.T
    him_ref[...] = si.T


def _ssm_sample(x, mods, layer, p, d_skip, h0_re, h0_im):
    n = DEC_BATCH * DEC_SEQ
    col = pl.BlockSpec((n, CHUNK), lambda k: (0, k))
    st = pl.BlockSpec((None, CHUNK_STATE, DEC_BATCH), lambda k: (layer, k, 0))
    mod = lambda j: pl.BlockSpec((None, DEC_BATCH, CHUNK), lambda k: (layer, 0, j * N_CHUNK + k))
    state_major = lambda a: jnp.transpose(a, (0, 2, 3, 1)).reshape(-1, N_STATE, DEC_BATCH)
    g, hre, him = pl.pallas_call(
        _ssm_sample_kernel,
        grid=(N_CHUNK,),
        in_specs=[col, mod(1), mod(0),
                  pl.BlockSpec((None, None, CHUNK, 2 * CHUNK_STATE), lambda k: (layer, k, 0, 0)),
                  pl.BlockSpec((None, None, 2 * CHUNK_STATE, CHUNK), lambda k: (layer, k, 0, 0)),
                  pl.BlockSpec((None, 1, CHUNK), lambda k: (layer, 0, k)),
                  pl.BlockSpec((None, 2, SUBLANES, CHUNK_STATE), lambda k: (layer, 0, 0, k)),
                  st, st],
        out_specs=[col] + [pl.BlockSpec((CHUNK_STATE, DEC_BATCH), lambda k: (k, 0))] * 2,
        out_shape=[jax.ShapeDtypeStruct((n, D_MODEL), BF16),
                   jax.ShapeDtypeStruct((N_STATE, DEC_BATCH), F32),
                   jax.ShapeDtypeStruct((N_STATE, DEC_BATCH), F32)],
        compiler_params=_params(("arbitrary",)),
        name="ssm_sample",
    )(x, mods.arr, mods.arr, p["bcat"], p["ccat"], d_skip[:, None, :], p["atab"],
      state_major(h0_re), state_major(h0_im))
    batch_major = lambda a: jnp.transpose(a.reshape(SSM_GROUPS, SSM_STATE, DEC_BATCH), (2, 0, 1))
    return g, batch_major(hre), batch_major(him)


def _rel_bucket(rel):
    n = jnp.maximum(rel, 0)
    max_exact = NUM_BUCKETS // 2
    large = max_exact + (jnp.log(jnp.maximum(n, 1).astype(F32) / max_exact)
                         / math.log(MAX_DISTANCE / max_exact) * (NUM_BUCKETS - max_exact)).astype(jnp.int32)
    large = jnp.minimum(large, NUM_BUCKETS - 1)
    return jnp.where(n < max_exact, n, large)


def _attn_scalars(attn_lam, subln_g, layer):
    lam_init = 0.8 - 0.6 * math.exp(-0.3 * layer)
    lam = (jnp.exp(jnp.sum(attn_lam[0] * attn_lam[1])) - jnp.exp(jnp.sum(attn_lam[2] * attn_lam[3]))
           + lam_init)
    lam_tile = jnp.full((SUBLANES, LANES), lam, F32)
    gain = (subln_g * (1.0 - lam_init)).reshape(1, V_DIM)
    return lam_tile, gain


def _bias_by_distance(rel_bias, n):
    bv = rel_bias[_rel_bucket(jnp.arange(n, dtype=jnp.int32))]
    return ((bv - rel_bias[NUM_BUCKETS - 1]) * LOG2E).T


def _softmax_update(s, m_prev):
    m_new = jnp.maximum(m_prev, jnp.max(s, axis=1, keepdims=True))
    p = jnp.exp2(s - jnp.concatenate([m_new] * (s.shape[1] // LANES), axis=1))
    return m_new, p.astype(BF16), jnp.exp2(m_prev - m_new)


def _with_ones(v):
    return jnp.concatenate([v, jnp.ones((v.shape[0], V_DIM), BF16)], axis=1)


def _acc_update(acc, alpha, p, vx):
    return (jnp.concatenate([alpha, alpha], axis=1) * acc
            + jnp.dot(p, vx, preferred_element_type=F32))


_NT = (((1,), (1,)), ((), ()))


def _attn_prompt_kernel(q_ref, k_ref, v_ref, tab_ref, lam_ref, gain_ref, o_ref,
                        q2_ref, m_ref, acc_ref, s_ref):
    i = pl.program_id(2)
    last = pl.num_programs(2) - 1

    def key_rows(j):
        return pl.ds(pl.multiple_of(j * TQ, TQ), TQ)

    def stack_queries(hh, tile):
        q = q_ref[key_rows(tile), hh * V_DIM:(hh + 1) * V_DIM]
        lane = lax.broadcasted_iota(jnp.int32, q.shape, 1)
        zero = jnp.zeros_like(q)
        q2_ref[hh, 0:TQ, :] = jnp.where(lane < HEAD_DIM, q, zero)
        q2_ref[hh, TQ:2 * TQ, :] = jnp.where(lane >= HEAD_DIM, q, zero)

    def logits(hh, j):
        cols = slice(hh * V_DIM, (hh + 1) * V_DIM)
        return lax.dot_general(q2_ref[hh], k_ref[key_rows(j), cols].astype(BF16), _NT,
                               preferred_element_type=F32)

    @pl.when(i == 0)
    def _():
        stack_queries(0, 0)
        s_ref[...] = logits(0, 0)

    for hh in range(1, HEADS_PER_STEP):
        stack_queries(hh, i)
    m_ref[...] = jnp.full(m_ref.shape, NEG_INF, F32)
    acc_ref[...] = jnp.zeros_like(acc_ref)

    def consume(hh, j, s, table):
        cols = slice(hh * V_DIM, (hh + 1) * V_DIM)
        if table is not None:
            bias = tab_ref[hh, table]
            s = s + jnp.concatenate([bias, bias], axis=0)
        m_new, p, alpha = _softmax_update(s, m_ref[hh])
        acc_ref[hh] = _acc_update(acc_ref[hh], alpha, p,
                                  _with_ones(v_ref[key_rows(j), cols].astype(BF16)))
        m_ref[hh] = m_new

    def step(j, table, same_tile):
        consume(0, j, s_ref[...], table)
        for hh in range(1, HEADS_PER_STEP):
            consume(hh, j, logits(hh, j), table)
        if same_tile:
            s_ref[...] = logits(0, j + 1)
        else:
            stack_queries(0, jnp.minimum(i + 1, last))
            s_ref[...] = logits(0, 0)

    def far(j, c):
        step(j, None, True)
        return c

    lax.fori_loop(0, jnp.maximum(i - 1, 0), far, 0)

    @pl.when(i >= 1)
    def _():
        step(i - 1, 1, True)

    step(i, 0, False)

    for hh in range(HEADS_PER_STEP):
        acc = acc_ref[hh]
        o = acc[:, :V_DIM] / acc[:, V_DIM:]
        o = o[0:TQ, :] - lam_ref[0:1, 0:1] * o[TQ:2 * TQ, :]
        o = o * lax.rsqrt(jnp.mean(o * o, axis=-1, keepdims=True) + LN_EPS) * gain_ref[...]
        o_ref[:, hh * V_DIM:(hh + 1) * V_DIM] = o.astype(o_ref.dtype)


def _attn_prompt_tables(rel_bias):
    bv = _bias_by_distance(rel_bias, 2 * TQ)
    masked = jnp.full((N_HEADS, TQ), NEG_INF, F32)
    diag = jnp.concatenate([bv[:, 0:1], masked, bv[:, TQ - 1:0:-1]], axis=1)
    prev = jnp.concatenate([bv[:, TQ:0:-1], bv[:, TQ:TQ + 1], bv[:, 2 * TQ - 1:TQ:-1]], axis=1)
    vec = jnp.broadcast_to(jnp.stack([diag, prev], axis=1)[:, :, None, :],
                           (N_HEADS, 2, SUBLANES, 2 * TQ))
    return pl.pallas_call(
        _toeplitz_kernel,
        grid=(N_HEADS,),
        in_specs=[pl.BlockSpec((None, 2, SUBLANES, 2 * TQ), lambda h: (h, 0, 0, 0))],
        out_specs=pl.BlockSpec((None, 2, TQ, TQ), lambda h: (h, 0, 0, 0)),
        out_shape=jax.ShapeDtypeStruct((N_HEADS, 2, TQ, TQ), F32),
        compiler_params=_params(("arbitrary",)),
        name="bias_tiles",
    )(vec)


def _toeplitz_kernel(vec_ref, o_ref):
    for t in range(o_ref.shape[0]):
        rows = jnp.broadcast_to(vec_ref[t, 0:1, :], (TQ, 2 * TQ))
        o_ref[t] = pltpu.roll(rows, 0, 1, stride=1, stride_axis=0)[:, :TQ]


def _attn_prompt(q, k, v, tab, lam_tile, gain):
    n_q = SEQ // TQ
    width = HEADS_PER_STEP * V_DIM
    shape3 = (BATCH, SEQ, D_MODEL)
    qspec = pl.BlockSpec((None, TQ, width), lambda b, h, i: (b, i, h))
    kvspec = pl.BlockSpec((None, SEQ, width), lambda b, h, i: (b, 0, h))
    o = pl.pallas_call(
        _attn_prompt_kernel,
        grid=(BATCH, N_HEADS // HEADS_PER_STEP, n_q),
        in_specs=[kvspec, kvspec, kvspec,
                  pl.BlockSpec((HEADS_PER_STEP, 2, TQ, TQ), lambda b, h, i: (h, 0, 0, 0)),
                  pl.BlockSpec((SUBLANES, LANES), lambda b, h, i: (0, 0)),
                  pl.BlockSpec((1, V_DIM), lambda b, h, i: (0, 0))],
        out_specs=qspec,
        out_shape=jax.ShapeDtypeStruct(shape3, BF16),
        scratch_shapes=[pltpu.VMEM((HEADS_PER_STEP, 2 * TQ, V_DIM), BF16),
                        pltpu.VMEM((HEADS_PER_STEP, 2 * TQ, LANES), F32),
                        pltpu.VMEM((HEADS_PER_STEP, 2 * TQ, 2 * V_DIM), F32),
                        pltpu.VMEM((2 * TQ, TQ), F32)],
        compiler_params=_params(("arbitrary", "arbitrary", "arbitrary")),
        name="attn_prompt",
    )(q.reshape(shape3), k.reshape(shape3), v.reshape(shape3), tab, lam_tile, gain)
    return o.reshape(BATCH * SEQ, D_MODEL)


N_SROWS = N_HEADS * 2 * DEC_SEQ


def _attn_sample_kernel(pt_ref, q_ref, qmask_ref, knew_ref, vnew_ref, bias_ref, bnew_ref,
                        lam_ref, gain_ref, *rest):
    del pt_ref
    k_refs = rest[:PAGES_PER_STEP]
    v_refs = rest[PAGES_PER_STEP:2 * PAGES_PER_STEP]
    o_ref, qx_ref, m_ref, acc_ref = rest[2 * PAGES_PER_STEP:]
    step = pl.program_id(1)
    rows_per_head = 2 * DEC_SEQ

    @pl.when(step == 0)
    def _():
        q = q_ref[...].astype(F32)
        q = jnp.concatenate([q, q], axis=0)
        qx_ref[...] = (jnp.concatenate([q] * N_HEADS, axis=0) * qmask_ref[...]).astype(BF16)
        m_ref[...] = jnp.full(m_ref.shape, NEG_INF, F32)
        acc_ref[...] = jnp.zeros_like(acc_ref)

    def update(s, values_of_head):
        m_new, p, alpha = _softmax_update(s, m_ref[...])
        for h in range(N_HEADS):
            r = slice(h * rows_per_head, (h + 1) * rows_per_head)
            acc_ref[r, :] = _acc_update(acc_ref[r, :], alpha[r, :], p[r, :],
                                        _with_ones(values_of_head(h)))
        m_ref[...] = m_new

    qx = qx_ref[...]
    kt = jnp.concatenate([r[...].astype(BF16) for r in k_refs], axis=1)
    s = jnp.dot(qx, kt, preferred_element_type=F32) + bias_ref[...]
    update(s, lambda h: jnp.concatenate(
        [r[pl.ds(h, PAGE_SIZE, stride=N_HEADS), :].astype(BF16) for r in v_refs], axis=0))

    @pl.when(step == pl.num_programs(1) - 1)
    def _():
        pad = jnp.zeros((LANES - SUBLANES, D_MODEL), BF16)
        kn = jnp.concatenate([knew_ref[...], pad], axis=0)
        vn = jnp.concatenate([vnew_ref[...], pad], axis=0)
        sn = lax.dot_general(qx, kn, _NT, preferred_element_type=F32) + bnew_ref[...]
        update(sn, lambda h: vn[:, h * V_DIM:(h + 1) * V_DIM])
        acc = acc_ref[...]
        o = acc[:, :V_DIM] / acc[:, V_DIM:]
        heads = []
        for h in range(N_HEADS):
            r0 = h * rows_per_head
            oh = o[r0:r0 + DEC_SEQ, :] - lam_ref[0:1, 0:1] * o[r0 + DEC_SEQ:r0 + rows_per_head, :]
            oh = oh * lax.rsqrt(jnp.mean(oh * oh, axis=-1, keepdims=True) + LN_EPS) * gain_ref[...]
            heads.append(oh)
        o_ref[...] = jnp.concatenate(heads, axis=1)


def _attn_sample_tables(rel_bias):
    bv = _bias_by_distance(rel_bias, PAST_LEN + DEC_SEQ)
    far_first = bv[:, ::-1]
    past = jnp.stack([far_first[:, DEC_SEQ - 1 - t:DEC_SEQ - 1 - t + PAST_LEN]
                      for t in range(DEC_SEQ)], axis=1)
    rel = (jnp.arange(DEC_SEQ, dtype=jnp.int32)[:, None]
           - jnp.arange(LANES, dtype=jnp.int32)[None, :])
    new = jnp.where(rel >= 0, bv[:, jnp.clip(rel, 0, DEC_SEQ - 1)], NEG_INF)
    rows = lambda a: jnp.broadcast_to(a[:, None], (N_HEADS, 2) + a.shape[1:]).reshape(
        N_SROWS, a.shape[-1])
    col = jnp.arange(D_MODEL, dtype=jnp.int32)[None, :]
    row = jnp.arange(N_SROWS, dtype=jnp.int32)[:, None]
    qmask = ((col // V_DIM == row // (2 * DEC_SEQ))
             & ((col % V_DIM) // HEAD_DIM == (row // DEC_SEQ) % 2)).astype(F32)
    return rows(past), rows(new), qmask


def _attn_sample(q, k_new, v_new, cache_k, cache_v, page_table, tables, lam_tile, gain):
    past, new, qmask = tables
    n_phys = cache_k.shape[0]
    ck = jnp.transpose(cache_k, (0, 2, 3, 4, 1)).reshape(n_phys, D_MODEL, PAGE_SIZE)
    cv = cache_v.reshape(n_phys, PAGE_SIZE * N_HEADS, V_DIM)
    n_steps = N_PAGES // PAGES_PER_STEP
    keys_per_step = PAGES_PER_STEP * PAGE_SIZE

    def page_spec(r):
        return pl.BlockSpec((None, D_MODEL, PAGE_SIZE),
                            lambda b, s, pt: (pt[b * N_PAGES + s * PAGES_PER_STEP + r], 0, 0))

    per_seq = lambda rows: pl.BlockSpec((None, rows, D_MODEL), lambda b, s, pt: (b, 0, 0))
    const = lambda shape: pl.BlockSpec(shape, lambda b, s, pt: (0,) * len(shape))
    grid_spec = pltpu.PrefetchScalarGridSpec(
        num_scalar_prefetch=1,
        grid=(DEC_BATCH, n_steps),
        in_specs=[per_seq(DEC_SEQ), const(qmask.shape), per_seq(SUBLANES), per_seq(SUBLANES),
                  pl.BlockSpec((N_SROWS, keys_per_step), lambda b, s, pt: (0, s)),
                  const(new.shape), const(lam_tile.shape), const(gain.shape)]
                 + [page_spec(r) for r in range(PAGES_PER_STEP)] * 2,
        out_specs=per_seq(DEC_SEQ),
        scratch_shapes=[pltpu.VMEM((N_SROWS, D_MODEL), BF16), pltpu.VMEM((N_SROWS, LANES), F32),
                        pltpu.VMEM((N_SROWS, 2 * V_DIM), F32)],
    )
    return pl.pallas_call(
        _attn_sample_kernel,
        grid_spec=grid_spec,
        out_shape=jax.ShapeDtypeStruct((DEC_BATCH, DEC_SEQ, D_MODEL), F32),
        compiler_params=_params(("arbitrary", "arbitrary")),
        name="attn_sample",
    )(page_table.reshape(-1), q, qmask, k_new, v_new, past, new, lam_tile, gain,
      *([ck] * PAGES_PER_STEP), *([cv] * PAGES_PER_STEP))


def _to_batch_major(a):
    return jnp.transpose(a.reshape(DEC_SEQ, DEC_BATCH, D_MODEL), (1, 0, 2))


def _to_time_major(a):
    return jnp.transpose(a, (1, 0, 2)).reshape(DEC_SEQ * DEC_BATCH, D_MODEL)


def _trunk(x, mods, mods_kv, h0_re, h0_im, attend, w):
    ssm_re, ssm_im = [], []
    k = v = None
    for layer in range(DEPTH):
        if layer < N_A_LAYERS:
            if mods.mode == "p":
                g, fr, fi = _ssm_prompt(x, mods, layer, w["ssm"], w["ssm_d"])
            else:
                g, fr, fi = _ssm_sample(x, mods, layer, w["ssm"], w["ssm_d"], h0_re, h0_im)
            ssm_re.append(fr)
            ssm_im.append(fi)
            mixed, w_mix = g, w["glu"]
        else:
            j = layer - N_A_LAYERS
            (q,) = _mod_linear(x, mods, layer, 1, 0, w["wq"][j], [(0, BF16, False)],
                               scale=HEAD_DIM ** -0.5 * LOG2E)
            mixed, w_mix = attend(q, k, v, j), w["wo"]
        x = _block_tail(mixed, x, mods, layer, w_mix, w["w1"], w["w2"], w["ln_g"], w["ln_b"],
                        glu=layer < N_A_LAYERS)
        if layer == N_A_LAYERS - 1:
            outs = [(0, F32, False), (1, F32, False)] + [(0, F32, True)] * (mods.mode == "p")
            k, v, *k_out = _mod_linear(x, mods_kv, 0, 1, 0, w["wkv"], outs)
    return x, jnp.stack(ssm_re), jnp.stack(ssm_im), (k_out or [k])[0], v


def kernel(x_prompt, x_sample, state_ssm_re, state_ssm_im, cache_k, cache_v, page_table, c_prompt, c_sample, rel_bias, w_ada, b_ada, ln_g, ln_b, ssm_lam_re, ssm_lam_im, ssm_log_dt, ssm_b_re, ssm_b_im, ssm_c_re, ssm_c_im, ssm_d, ssm_w_glu_a, ssm_w_glu_b, w_ada_kv, b_ada_kv, w_kv, attn_w_q, attn_lam, attn_subln_g, attn_w_o, mlp_w1, mlp_w2):
    w = {
        "ssm": _ssm_prep(ssm_lam_re, ssm_lam_im, ssm_log_dt, ssm_b_re, ssm_b_im, ssm_c_re, ssm_c_im),
        "ssm_d": ssm_d,
        "glu": jnp.concatenate([ssm_w_glu_a, ssm_w_glu_b], axis=-1).astype(BF16),
        "wq": attn_w_q.astype(BF16),
        "wo": attn_w_o.astype(BF16),
        "wkv": w_kv.astype(BF16),
        "w1": mlp_w1.astype(BF16),
        "w2": mlp_w2.astype(BF16),
        "ln_g": ln_g,
        "ln_b": ln_b,
    }
    c_all = jnp.concatenate([c_sample, c_prompt, jnp.zeros((BATCH, D_MODEL), F32)], axis=0)
    ada = _ada(c_all, w_ada, b_ada)
    ada_kv = _ada(c_all, w_ada_kv[None], b_ada_kv[None])
    scalars = [_attn_scalars(attn_lam[j], attn_subln_g[j], N_A_LAYERS + j)
               for j in range(DEPTH - N_A_LAYERS)]

    prompt_tab = _attn_prompt_tables(rel_bias)

    def attend_prompt(q, k, v, j):
        return _attn_prompt(q, k, v, prompt_tab, *scalars[j])

    y_p, re_p, im_p, k_p, v_p = _trunk(x_prompt.reshape(BATCH * SEQ, D_MODEL), _Mods(ada, "p"),
                                       _Mods(ada_kv, "p"), None, None, attend_prompt, w)

    sample_tab = _attn_sample_tables(rel_bias)

    def attend_sample(q, k, v, j):
        pad = lambda a: jnp.pad(_to_batch_major(a).astype(BF16),
                                ((0, 0), (0, SUBLANES - DEC_SEQ), (0, 0)))
        o = _attn_sample(_to_batch_major(q), pad(k), pad(v), cache_k, cache_v, page_table,
                         sample_tab, *scalars[j])
        return _to_time_major(o).astype(BF16)

    y_s, re_s, im_s, k_s, v_s = _trunk(_to_time_major(x_sample), _Mods(ada, "s"), _Mods(ada_kv, "s"),
                                       state_ssm_re, state_ssm_im, attend_sample, w)

    return (y_p.reshape(BATCH, SEQ, D_MODEL), _to_batch_major(y_s), re_p, im_p,
            jnp.transpose(k_p.reshape(BATCH, N_HEADS, 2, HEAD_DIM, SEQ), (0, 4, 1, 2, 3)),
            v_p.reshape(BATCH, SEQ, N_HEADS, V_DIM),
            re_s, im_s,
            _to_batch_major(k_s).reshape(DEC_BATCH, DEC_SEQ, N_HEADS, 2, HEAD_DIM),
            _to_batch_major(v_s).reshape(DEC_BATCH, DEC_SEQ, N_HEADS, V_DIM))
```

```python
import functools
import math

import jax
import jax.numpy as jnp
from jax import lax
from jax.experimental import pallas as pl
from jax.experimental.pallas import tpu as pltpu

F32 = jnp.float32
BF16 = jnp.bfloat16

D_MODEL = 1024
BATCH = 4
SEQ = 4096
DEPTH = 4
DEC_BATCH = 128
DEC_SEQ = 4
PAST_LEN = 2048
PAGE_SIZE = 128
N_PAGES = PAST_LEN // PAGE_SIZE
N_A_LAYERS = DEPTH // 2
SSM_GROUP = 16
SSM_GROUPS = D_MODEL // SSM_GROUP
SSM_STATE = 64
N_STATE = SSM_GROUPS * SSM_STATE
N_HEADS = 8
HEAD_DIM = D_MODEL // (2 * N_HEADS)
V_DIM = 2 * HEAD_DIM
D_FF = 4 * D_MODEL
NUM_BUCKETS = 32
MAX_DISTANCE = 128
N_MOD = 6
ALPHA = (2.0 * DEPTH) ** 0.25
LN_EPS = 1e-5
NEG_INF = -1e30

SUBLANES = 8
LANES = 128
VMEM_LIMIT = 48 * 1024 * 1024

MOD_ROWS = DEC_BATCH + 2 * BATCH
CHUNK = LANES
N_CHUNK = D_MODEL // CHUNK
CHUNK_STATE = N_STATE // N_CHUNK
TM_PROMPT = 512
TILE = SUBLANES
T_SSM = 512
TQ = 512
HEADS_PER_STEP = 2
PAGES_PER_STEP = 16
LOG2E = math.log2(math.e)
FF_CHUNK = 1024


def _params(sem, vmem=VMEM_LIMIT):
    return pltpu.CompilerParams(dimension_semantics=sem, vmem_limit_bytes=vmem)


def _layer_norm(z, g, b):
    mu = jnp.mean(z, axis=-1, keepdims=True)
    zc = z - mu
    var = jnp.mean(zc * zc, axis=-1, keepdims=True)
    return zc * lax.rsqrt(var + LN_EPS) * g + b


def _ada_kernel(c_ref, w_ref, b_ref, o_ref):
    c = c_ref[...]
    sc = (c * jax.nn.sigmoid(c)).astype(BF16)
    w = w_ref[...].astype(BF16)
    o_ref[...] = jnp.dot(sc, w, preferred_element_type=F32) + b_ref[...]


def _ada(c_all, w, b):
    n_l, _, width = w.shape
    tn = 2048
    return pl.pallas_call(
        _ada_kernel,
        grid=(n_l, width // tn),
        in_specs=[
            pl.BlockSpec((MOD_ROWS, D_MODEL), lambda l, j: (0, 0)),
            pl.BlockSpec((None, D_MODEL, tn), lambda l, j: (l, 0, j)),
            pl.BlockSpec((None, 1, tn), lambda l, j: (l, 0, j)),
        ],
        out_specs=pl.BlockSpec((None, MOD_ROWS, tn), lambda l, j: (l, 0, j)),
        out_shape=jax.ShapeDtypeStruct((n_l, MOD_ROWS, width), F32),
        compiler_params=_params(("arbitrary", "arbitrary")),
        name="ada_mod",
    )(c_all, w, b.reshape(n_l, 1, width))


class _Mods:
    def __init__(self, arr, mode):
        self.mode = mode
        self.arr = arr if mode == "s" else arr.reshape(arr.shape[0], MOD_ROWS, 1, arr.shape[-1])

    def spec(self, layer, j, batch_of):
        if self.mode == "s":
            return pl.BlockSpec((None, DEC_BATCH, D_MODEL), lambda *g: (layer, 0, j))
        return pl.BlockSpec((None, None, 1, D_MODEL),
                            lambda *g: (layer, DEC_BATCH + batch_of(*g), 0, j))


def _row_tiling(mode):
    if mode == "s":
        return DEC_BATCH, lambda i: 0
    tiles_per_batch = SEQ // TM_PROMPT
    return TM_PROMPT, lambda i: i // tiles_per_batch


def _const_spec(shape):
    zeros = (0,) * len(shape)
    return pl.BlockSpec(shape, lambda *g: zeros, pipeline_mode=pl.Buffered(1))


def _mod_linear_kernel(x_ref, sc_ref, sh_ref, w_ref, *o_refs, scale, blocks):
    h = (x_ref[...] * (1.0 + sc_ref[...]) + sh_ref[...]).astype(BF16)
    y = jnp.dot(h, w_ref[...], preferred_element_type=F32)
    if scale != 1.0:
        y = y * scale
    for (n, transposed), o_ref in zip(blocks, o_refs):
        block = y[:, n * D_MODEL:(n + 1) * D_MODEL].astype(o_ref.dtype)
        o_ref[...] = block.T if transposed else block


def _mod_linear(x, mods, layer, j_scale, j_shift, w, outs, scale=1.0):
    n = x.shape[0]
    tm, batch_of = _row_tiling(mods.mode)
    row = pl.BlockSpec((tm, D_MODEL), lambda i: (i, 0))
    tiles_per_batch = SEQ // tm
    col = pl.BlockSpec((None, D_MODEL, tm), lambda i: (i // tiles_per_batch, 0, i % tiles_per_batch))
    return pl.pallas_call(
        functools.partial(_mod_linear_kernel, scale=scale,
                          blocks=tuple((o[0], o[2]) for o in outs)),
        grid=(n // tm,),
        in_specs=[row, mods.spec(layer, j_scale, batch_of), mods.spec(layer, j_shift, batch_of),
                  _const_spec(w.shape)],
        out_specs=[col if o[2] else row for o in outs],
        out_shape=[jax.ShapeDtypeStruct((BATCH, D_MODEL, SEQ) if o[2] else (n, D_MODEL), o[1])
                   for o in outs],
        compiler_params=_params(("arbitrary",)),
        name="mod_linear",
    )(x, mods.arr, mods.arr, w)


def _block_tail_kernel(a_ref, x_ref, gm_ref, sc_ref, sh_ref, gf_ref, wm_ref, w1_ref, w2_ref,
                       g_ref, b_ref, o_ref, *, glu):
    y = jnp.dot(a_ref[...], wm_ref[...], preferred_element_type=F32)
    if glu:
        y = y[:, :D_MODEL] * jax.nn.sigmoid(y[:, D_MODEL:])
    x = _layer_norm(ALPHA * x_ref[...] + (1.0 + gm_ref[...]) * y, g_ref[0:1, :], b_ref[0:1, :])
    h = (x * (1.0 + sc_ref[...]) + sh_ref[...]).astype(BF16)
    acc = jnp.zeros(x.shape, F32)
    for c in range(D_FF // FF_CHUNK):
        cols = slice(c * FF_CHUNK, (c + 1) * FF_CHUNK)
        hid = jnp.dot(h, w1_ref[:, cols], preferred_element_type=F32)
        hid = jnp.square(jnp.maximum(hid, 0.0)).astype(BF16)
        acc = acc + jnp.dot(hid, w2_ref[cols, :], preferred_element_type=F32)
    z = ALPHA * x + (1.0 + gf_ref[...]) * acc
    o_ref[...] = _layer_norm(z, g_ref[1:2, :], b_ref[1:2, :])


def _block_tail(a, x, mods, layer, w_mix, w1, w2, ln_g, ln_b, glu):
    n = x.shape[0]
    tm, batch_of = _row_tiling(mods.mode)
    row = pl.BlockSpec((tm, D_MODEL), lambda i: (i, 0))
    ln = pl.BlockSpec((None, 2, D_MODEL), lambda i: (layer, 0, 0))
    weight = lambda w: pl.BlockSpec((None,) + w.shape[1:], lambda i: (layer % w.shape[0], 0, 0),
                                    pipeline_mode=pl.Buffered(1))
    return pl.pallas_call(
        functools.partial(_block_tail_kernel, glu=glu),
        grid=(n // tm,),
        in_specs=[row, row] + [mods.spec(layer, j, batch_of) for j in (2, 4, 3, 5)]
                 + [weight(w_mix), weight(w1), weight(w2), ln, ln],
        out_specs=row,
        out_shape=jax.ShapeDtypeStruct((n, D_MODEL), F32),
        compiler_params=_params(("arbitrary",)),
        name="block_tail",
    )(a, x, mods.arr, mods.arr, mods.arr, mods.arr, w_mix, w1, w2, ln_g, ln_b)


def _cmul(ar, ai, br, bi):
    return ar * br - ai * bi, ar * bi + ai * br


def _ssm_prep(lam_re, lam_im, log_dt, b_re, b_im, c_re, c_im):
    n_l = lam_re.shape[0]
    dt = jnp.exp(log_dt)[..., None]
    n_scan = TILE + SUBLANES
    ns = (list(range(TILE + 1)) + [TILE * m for m in range(2, SUBLANES + 1)]
          + list(range(TILE - 1, -1, -1)))
    n = jnp.asarray(ns, F32).reshape(-1, 1, 1, 1)
    mag = jnp.exp(n * (lam_re * dt))
    pr = mag * jnp.cos(n * (lam_im * dt))
    pi = mag * jnp.sin(n * (lam_im * dt))
    ar, ai = pr[1], pi[1]
    den = lam_re * lam_re + lam_im * lam_im
    er = ((ar - 1.0) * lam_re + ai * lam_im) / den
    ei = (ai * lam_re - (ar - 1.0) * lam_im) / den
    bbr = er[..., None] * b_re - ei[..., None] * b_im
    bbi = er[..., None] * b_im + ei[..., None] * b_re
    gpc = CHUNK // SSM_GROUP
    exact = lax.Precision.HIGHEST

    def block_diag(m, rows_per_group, cols_per_group, steps=1):
        y = steps * cols_per_group
        c = jnp.arange(steps * gpc * cols_per_group)
        step, group, inner = c // (gpc * cols_per_group), (c // cols_per_group) % gpc, c % cols_per_group
        select = (jnp.arange(y)[:, None] == (step * cols_per_group + inner)[None, :]).astype(BF16)
        wide = jnp.einsum("...ry,yc->...rc", m.astype(BF16), select, preferred_element_type=BF16)
        row_group = (jnp.arange(m.shape[-2]) // rows_per_group) % gpc
        return jnp.where(row_group[:, None] == group[None, :], wide, jnp.zeros((), BF16))

    def per_chunk(m, order):
        x, _, _, a, b = m.shape
        m = m.reshape(x, n_l, N_CHUNK, gpc, a, b)
        return jnp.transpose(m, (1, 2) + tuple({"x": 0, "g": 3, "a": 4, "b": 5}[o] for o in order))

    swap = lambda m: jnp.swapaxes(m, -1, -2)
    b_in = per_chunk(jnp.concatenate([swap(bbr), swap(bbi)], axis=-1)[None], "gaxb")
    bcat = block_diag(b_in.reshape(n_l, N_CHUNK, CHUNK, 2 * SSM_STATE), SSM_GROUP, SSM_STATE,
                      steps=2)
    c_out = per_chunk(jnp.stack([swap(c_re), -swap(c_im)]), "xgab")
    ccat = block_diag(c_out.reshape(n_l, N_CHUNK, 2 * CHUNK_STATE, SSM_GROUP), SSM_STATE, SSM_GROUP)

    abr, abi = _cmul(pr[:TILE, ..., None], pi[:TILE, ..., None], bbr, bbi)
    dot_p = functools.partial(jnp.einsum, "lghp,jlgpi->jlgih", precision=exact)
    taps = per_chunk(dot_p(c_re, abr) - dot_p(c_im, abi), "xgab")
    kcat = block_diag(taps.reshape(n_l, N_CHUNK, TILE * CHUNK, SSM_GROUP), SSM_GROUP, SSM_GROUP)

    wr, wi = _cmul(pr[n_scan:, ..., None], pi[n_scan:, ..., None], bbr, bbi)
    ends = per_chunk(jnp.concatenate([swap(wr), swap(wi)], axis=-1), "xgab")
    wst = block_diag(ends.reshape(n_l, N_CHUNK, TILE * CHUNK, 2 * SSM_STATE), SSM_GROUP, SSM_STATE,
                     steps=2)

    mr, mi = _cmul(pr[1:TILE + 1, :, :, None, :], pi[1:TILE + 1, :, :, None, :], c_re, c_im)
    outs = jnp.stack([per_chunk(m, "gbxa") for m in (mr, -mi)], axis=2)
    call = block_diag(outs.reshape(n_l, N_CHUNK, 2 * CHUNK_STATE, TILE * SSM_GROUP), SSM_STATE,
                      SSM_GROUP, steps=TILE)

    flat = lambda a: jnp.moveaxis(a, 0, 1).reshape(n_l, -1, N_STATE)
    rows = lax.broadcasted_iota(jnp.int32, (SUBLANES, N_STATE), 0)
    parts = []
    for a in (flat(pr[TILE:n_scan]), flat(pi[TILE:n_scan])):
        masked = lambda m, first: jnp.where(rows >= first, a[:, m - 1:m, :], 0.0)
        parts.append([masked(1, 1), masked(2, 2), masked(4, 4), a, masked(SUBLANES, 0)])
    tab = jnp.stack([p for pair in zip(*parts) for p in pair], axis=1)
    atab = jnp.stack([jnp.broadcast_to(flat(a[1:2]), (n_l, SUBLANES, N_STATE)) for a in (pr, pi)],
                     axis=1)
    return dict(bcat=bcat, ccat=ccat, kcat=kcat, wst=wst, call=call, tab=tab, atab=atab)


def _ssm_prompt_kernel(x_ref, sc_ref, sh_ref, kcat_ref, wst_ref, call_ref, d_ref, tab_ref,
                       g_ref, hre_ref, him_ref, h_ref, buf_ref, y_ref, taps_ref):
    n_tiles = SEQ // TILE
    half = CHUNK_STATE

    @pl.when(pl.program_id(1) == 0)
    def _():
        zero = jnp.zeros((CHUNK, CHUNK), BF16)
        for r in range(TILE):
            for s in range(TILE):
                lag = s - r
                block = kcat_ref[lag * CHUNK:(lag + 1) * CHUNK, :] if lag >= 0 else zero
                taps_ref[r * CHUNK:(r + 1) * CHUNK, s * CHUNK:(s + 1) * CHUNK] = block

    h_ref[...] = x_ref[...] * (1.0 + sc_ref[...]) + sh_ref[...]

    steps = [h_ref[pl.ds(r, n_tiles, stride=TILE), :].astype(BF16) for r in range(TILE)]
    tiles = jnp.concatenate(steps, axis=1)
    buf_ref[...] = jnp.dot(tiles, wst_ref[...], preferred_element_type=F32)
    y_tiles = jnp.dot(tiles, taps_ref[...], preferred_element_type=F32)

    row = lax.broadcasted_iota(jnp.int32, (SUBLANES, half), 0)

    def tile_scan(i, carry):
        cr, ci = carry
        r0 = pl.multiple_of(i * SUBLANES, SUBLANES)
        vr = buf_ref[pl.ds(r0, SUBLANES), 0:half]
        vi = buf_ref[pl.ds(r0, SUBLANES), half:2 * half]
        for lvl, shift in enumerate((1, 2, 4)):
            pr, pi = _cmul(tab_ref[2 * lvl], tab_ref[2 * lvl + 1],
                           pltpu.roll(vr, shift, 0), pltpu.roll(vi, shift, 0))
            vr, vi = vr + pr, vi + pi
        pr, pi = _cmul(tab_ref[6], tab_ref[7], cr, ci)
        buf_ref[pl.ds(r0, SUBLANES), 0:half] = jnp.where(row == 0, cr, pltpu.roll(vr + pr, 1, 0))
        buf_ref[pl.ds(r0, SUBLANES), half:2 * half] = jnp.where(row == 0, ci,
                                                                 pltpu.roll(vi + pi, 1, 0))
        pr, pi = _cmul(tab_ref[8], tab_ref[9], cr, ci)
        last = SUBLANES - 1
        return (jnp.broadcast_to(vr[last:, :], vr.shape) + pr,
                jnp.broadcast_to(vi[last:, :], vi.shape) + pi)

    zero = jnp.zeros((SUBLANES, half), F32)
    cr, ci = lax.fori_loop(0, n_tiles // SUBLANES, tile_scan, (zero, zero))
    hre_ref[...] = cr[0:1, :]
    him_ref[...] = ci[0:1, :]

    y_tiles = y_tiles + jnp.dot(buf_ref[...].astype(BF16), call_ref[...],
                                preferred_element_type=F32)
    for s in range(TILE):
        y_ref[pl.ds(s, n_tiles, stride=TILE), :] = y_tiles[:, s * CHUNK:(s + 1) * CHUNK]

    y = y_ref[...] + d_ref[...] * h_ref[...]
    g_ref[...] = jax.nn.gelu(y).astype(g_ref.dtype)


def _ssm_prompt(x, mods, layer, p, d_skip):
    col = pl.BlockSpec((SEQ, CHUNK), lambda k, b: (b, k))
    fin = pl.BlockSpec((None, 1, CHUNK_STATE), lambda k, b: (b, 0, k))
    mod = lambda j: pl.BlockSpec((None, None, 1, CHUNK),
                                 lambda k, b: (layer, DEC_BATCH + b, 0, j * N_CHUNK + k))
    weight = lambda a: pl.BlockSpec((None, None) + a.shape[2:], lambda k, b: (layer, k, 0, 0))
    g, hre, him = pl.pallas_call(
        _ssm_prompt_kernel,
        grid=(N_CHUNK, BATCH),
        in_specs=[col, mod(1), mod(0), weight(p["kcat"]), weight(p["wst"]), weight(p["call"]),
                  pl.BlockSpec((None, 1, CHUNK), lambda k, b: (layer, 0, k)),
                  pl.BlockSpec((None, p["tab"].shape[1], SUBLANES, CHUNK_STATE),
                               lambda k, b: (layer, 0, 0, k))],
        out_specs=[col, fin, fin],
        out_shape=[jax.ShapeDtypeStruct((BATCH * SEQ, D_MODEL), BF16),
                   jax.ShapeDtypeStruct((BATCH, 1, N_STATE), F32),
                   jax.ShapeDtypeStruct((BATCH, 1, N_STATE), F32)],
        scratch_shapes=[pltpu.VMEM((SEQ, CHUNK), F32),
                        pltpu.VMEM((SEQ // TILE, 2 * CHUNK_STATE), F32),
                        pltpu.VMEM((SEQ, CHUNK), F32),
                        pltpu.VMEM((TILE * CHUNK, TILE * CHUNK), BF16)],
        compiler_params=_params(("arbitrary", "arbitrary")),
        name="ssm_prompt",
    )(x, mods.arr, mods.arr, p["kcat"], p["wst"], p["call"], d_skip[:, None, :], p["tab"])
    shape = (BATCH, SSM_GROUPS, SSM_STATE)
    return g, hre.reshape(shape), him.reshape(shape)


def _ssm_sample_kernel(x_ref, sc_ref, sh_ref, bcat_ref, ccat_ref, d_ref, a_ref, h0r_ref, h0i_ref,
                       g_ref, hre_ref, him_ref):
    tile = lambda m: jnp.concatenate([m] * DEC_SEQ, axis=0)
    h = x_ref[...] * (1.0 + tile(sc_ref[...])) + tile(sh_ref[...])
    bu = jnp.dot(h.astype(BF16), bcat_ref[...], preferred_element_type=F32)
    ar = a_ref[0, 0:1, :]
    ai = a_ref[1, 0:1, :]
    sr = h0r_ref[...].T
    si = h0i_ref[...].T
    states = []
    for t in range(DEC_SEQ):
        rows = slice(t * DEC_BATCH, (t + 1) * DEC_BATCH)
        pr, pi = _cmul(ar, ai, sr, si)
        sr = pr + bu[rows, :CHUNK_STATE]
        si = pi + bu[rows, CHUNK_STATE:]
        states.append(jnp.concatenate([sr, si], axis=1))
    xs = jnp.concatenate(states, axis=0).astype(BF16)
    y = jnp.dot(xs, ccat_ref[...], preferred_element_type=F32) + d_ref[...] * h
    g_ref[...] = jax.nn.gelu(y).astype(g_ref.dtype)
    hre_ref[...] = sr.T
    him_ref[...] = si.T


def _ssm_sample(x, mods, layer, p, d_skip, h0_re, h0_im):
    n = DEC_BATCH * DEC_SEQ
    col = pl.BlockSpec((n, CHUNK), lambda k: (0, k))
    st = pl.BlockSpec((None, CHUNK_STATE, DEC_BATCH), lambda k: (layer, k, 0))
    mod = lambda j: pl.BlockSpec((None, DEC_BATCH, CHUNK), lambda k: (layer, 0, j * N_CHUNK + k))
    state_major = lambda a: jnp.transpose(a, (0, 2, 3, 1)).reshape(-1, N_STATE, DEC_BATCH)
    g, hre, him = pl.pallas_call(
        _ssm_sample_kernel,
        grid=(N_CHUNK,),
        in_specs=[col, mod(1), mod(0),
                  pl.BlockSpec((None, None, CHUNK, 2 * CHUNK_STATE), lambda k: (layer, k, 0, 0)),
                  pl.BlockSpec((None, None, 2 * CHUNK_STATE, CHUNK), lambda k: (layer, k, 0, 0)),
                  pl.BlockSpec((None, 1, CHUNK), lambda k: (layer, 0, k)),
                  pl.BlockSpec((None, 2, SUBLANES, CHUNK_STATE), lambda k: (layer, 0, 0, k)),
                  st, st],
        out_specs=[col] + [pl.BlockSpec((CHUNK_STATE, DEC_BATCH), lambda k: (k, 0))] * 2,
        out_shape=[jax.ShapeDtypeStruct((n, D_MODEL), BF16),
                   jax.ShapeDtypeStruct((N_STATE, DEC_BATCH), F32),
                   jax.ShapeDtypeStruct((N_STATE, DEC_BATCH), F32)],
        compiler_params=_params(("arbitrary",)),
        name="ssm_sample",
    )(x, mods.arr, mods.arr, p["bcat"], p["ccat"], d_skip[:, None, :], p["atab"],
      state_major(h0_re), state_major(h0_im))
    batch_major = lambda a: jnp.transpose(a.reshape(SSM_GROUPS, SSM_STATE, DEC_BATCH), (2, 0, 1))
    return g, batch_major(hre), batch_major(him)


def _rel_bucket(rel):
    n = jnp.maximum(rel, 0)
    max_exact = NUM_BUCKETS // 2
    large = max_exact + (jnp.log(jnp.maximum(n, 1).astype(F32) / max_exact)
                         / math.log(MAX_DISTANCE / max_exact) * (NUM_BUCKETS - max_exact)).astype(jnp.int32)
    large = jnp.minimum(large, NUM_BUCKETS - 1)
    return jnp.where(n < max_exact, n, large)


def _attn_scalars(attn_lam, subln_g, layer):
    lam_init = 0.8 - 0.6 * math.exp(-0.3 * layer)
    lam = (jnp.exp(jnp.sum(attn_lam[0] * attn_lam[1])) - jnp.exp(jnp.sum(attn_lam[2] * attn_lam[3]))
           + lam_init)
    lam_tile = jnp.full((SUBLANES, LANES), lam, F32)
    gain = (subln_g * (1.0 - lam_init)).reshape(1, V_DIM)
    return lam_tile, gain


def _bias_by_distance(rel_bias, dist):
    bv = rel_bias[_rel_bucket(dist.astype(jnp.int32))]
    return ((bv - rel_bias[NUM_BUCKETS - 1]) * LOG2E).T


def _softmax_update(s, m_prev):
    m_new = jnp.maximum(m_prev, jnp.max(s, axis=1, keepdims=True))
    p = jnp.exp2(s - jnp.concatenate([m_new] * (s.shape[1] // LANES), axis=1))
    return m_new, p.astype(BF16), jnp.exp2(m_prev - m_new)


def _with_ones(v):
    return jnp.concatenate([v, jnp.ones((v.shape[0], V_DIM), BF16)], axis=1)


def _acc_update(acc, alpha, p, vx):
    return (jnp.concatenate([alpha, alpha], axis=1) * acc
            + jnp.dot(p, vx, preferred_element_type=F32))


_NT = (((1,), (1,)), ((), ()))


def _attn_prompt_kernel(q_ref, k_ref, v_ref, tab_ref, lam_ref, gain_ref, o_ref,
                        q2_ref, m_ref, acc_ref, s_ref):
    i = pl.program_id(2)
    last = pl.num_programs(2) - 1

    def key_rows(j):
        return pl.ds(pl.multiple_of(j * TQ, TQ), TQ)

    def stack_queries(hh, tile):
        q = q_ref[key_rows(tile), hh * V_DIM:(hh + 1) * V_DIM]
        lane = lax.broadcasted_iota(jnp.int32, q.shape, 1)
        zero = jnp.zeros_like(q)
        q2_ref[hh, 0:TQ, :] = jnp.where(lane < HEAD_DIM, q, zero)
        q2_ref[hh, TQ:2 * TQ, :] = jnp.where(lane >= HEAD_DIM, q, zero)

    def logits(hh, j):
        cols = slice(hh * V_DIM, (hh + 1) * V_DIM)
        return lax.dot_general(q2_ref[hh], k_ref[key_rows(j), cols].astype(BF16), _NT,
                               preferred_element_type=F32)

    @pl.when(i == 0)
    def _():
        stack_queries(0, 0)
        s_ref[...] = logits(0, 0)

    for hh in range(1, HEADS_PER_STEP):
        stack_queries(hh, i)
    m_ref[...] = jnp.full(m_ref.shape, NEG_INF, F32)
    acc_ref[...] = jnp.zeros_like(acc_ref)

    def consume(hh, j, s, table):
        cols = slice(hh * V_DIM, (hh + 1) * V_DIM)
        if table is not None:
            bias = tab_ref[hh, table]
            s = s + jnp.concatenate([bias, bias], axis=0)
        m_new, p, alpha = _softmax_update(s, m_ref[hh])
        acc_ref[hh] = _acc_update(acc_ref[hh], alpha, p,
                                  _with_ones(v_ref[key_rows(j), cols].astype(BF16)))
        m_ref[hh] = m_new

    def step(j, table, same_tile):
        consume(0, j, s_ref[...], table)
        for hh in range(1, HEADS_PER_STEP):
            consume(hh, j, logits(hh, j), table)
        if same_tile:
            s_ref[...] = logits(0, j + 1)
        else:
            stack_queries(0, jnp.minimum(i + 1, last))
            s_ref[...] = logits(0, 0)

    def far(j, c):
        step(j, None, True)
        return c

    lax.fori_loop(0, jnp.maximum(i - 1, 0), far, 0)

    @pl.when(i >= 1)
    def _():
        step(i - 1, 1, True)

    step(i, 0, False)

    for hh in range(HEADS_PER_STEP):
        acc = acc_ref[hh]
        o = acc[:, :V_DIM] / acc[:, V_DIM:]
        o = o[0:TQ, :] - lam_ref[0:1, 0:1] * o[TQ:2 * TQ, :]
        o = o * lax.rsqrt(jnp.mean(o * o, axis=-1, keepdims=True) + LN_EPS) * gain_ref[...]
        o_ref[:, hh * V_DIM:(hh + 1) * V_DIM] = o.astype(o_ref.dtype)


def _attn_prompt_tables(rel_bias):
    m = jnp.arange(2 * TQ)
    behind = (2 * TQ - m) % (2 * TQ)
    diag = jnp.where((m >= 1) & (m <= TQ), NEG_INF, _bias_by_distance(rel_bias, behind))
    prev = _bias_by_distance(rel_bias, jnp.where(m <= TQ, TQ - m, 3 * TQ - m))
    vec = jnp.broadcast_to(jnp.stack([diag, prev], axis=1)[:, :, None, :],
                           (N_HEADS, 2, SUBLANES, 2 * TQ))
    return pl.pallas_call(
        _toeplitz_kernel,
        grid=(N_HEADS,),
        in_specs=[pl.BlockSpec((None, 2, SUBLANES, 2 * TQ), lambda h: (h, 0, 0, 0))],
        out_specs=pl.BlockSpec((None, 2, TQ, TQ), lambda h: (h, 0, 0, 0)),
        out_shape=jax.ShapeDtypeStruct((N_HEADS, 2, TQ, TQ), F32),
        compiler_params=_params(("arbitrary",)),
        name="bias_tiles",
    )(vec)


def _toeplitz_kernel(vec_ref, o_ref):
    for t in range(o_ref.shape[0]):
        rows = jnp.broadcast_to(vec_ref[t, 0:1, :], (TQ, 2 * TQ))
        o_ref[t] = pltpu.roll(rows, 0, 1, stride=1, stride_axis=0)[:, :TQ]


def _attn_prompt(q, k, v, tab, lam_tile, gain):
    n_q = SEQ // TQ
    width = HEADS_PER_STEP * V_DIM
    shape3 = (BATCH, SEQ, D_MODEL)
    qspec = pl.BlockSpec((None, TQ, width), lambda b, h, i: (b, i, h))
    kvspec = pl.BlockSpec((None, SEQ, width), lambda b, h, i: (b, 0, h))
    o = pl.pallas_call(
        _attn_prompt_kernel,
        grid=(BATCH, N_HEADS // HEADS_PER_STEP, n_q),
        in_specs=[kvspec, kvspec, kvspec,
                  pl.BlockSpec((HEADS_PER_STEP, 2, TQ, TQ), lambda b, h, i: (h, 0, 0, 0)),
                  pl.BlockSpec((SUBLANES, LANES), lambda b, h, i: (0, 0)),
                  pl.BlockSpec((1, V_DIM), lambda b, h, i: (0, 0))],
        out_specs=qspec,
        out_shape=jax.ShapeDtypeStruct(shape3, BF16),
        scratch_shapes=[pltpu.VMEM((HEADS_PER_STEP, 2 * TQ, V_DIM), BF16),
                        pltpu.VMEM((HEADS_PER_STEP, 2 * TQ, LANES), F32),
                        pltpu.VMEM((HEADS_PER_STEP, 2 * TQ, 2 * V_DIM), F32),
                        pltpu.VMEM((2 * TQ, TQ), F32)],
        compiler_params=_params(("arbitrary", "arbitrary", "arbitrary")),
        name="attn_prompt",
    )(q.reshape(shape3), k.reshape(shape3), v.reshape(shape3), tab, lam_tile, gain)
    return o.reshape(BATCH * SEQ, D_MODEL)


N_SROWS = N_HEADS * 2 * DEC_SEQ


def _attn_sample_kernel(pt_ref, q_ref, qmask_ref, knew_ref, vnew_ref, bias_ref, bnew_ref,
                        lam_ref, gain_ref, *rest):
    del pt_ref
    k_refs = rest[:PAGES_PER_STEP]
    v_refs = rest[PAGES_PER_STEP:2 * PAGES_PER_STEP]
    o_ref, qx_ref, m_ref, acc_ref = rest[2 * PAGES_PER_STEP:]
    step = pl.program_id(1)
    rows_per_head = 2 * DEC_SEQ

    @pl.when(step == 0)
    def _():
        q = q_ref[...].astype(F32)
        q = jnp.concatenate([q, q], axis=0)
        qx_ref[...] = (jnp.concatenate([q] * N_HEADS, axis=0) * qmask_ref[...]).astype(BF16)
        m_ref[...] = jnp.full(m_ref.shape, NEG_INF, F32)
        acc_ref[...] = jnp.zeros_like(acc_ref)

    def update(s, values_of_head):
        m_new, p, alpha = _softmax_update(s, m_ref[...])
        for h in range(N_HEADS):
            r = slice(h * rows_per_head, (h + 1) * rows_per_head)
            acc_ref[r, :] = _acc_update(acc_ref[r, :], alpha[r, :], p[r, :],
                                        _with_ones(values_of_head(h)))
        m_ref[...] = m_new

    qx = qx_ref[...]
    kt = jnp.concatenate([r[...].astype(BF16) for r in k_refs], axis=1)
    s = jnp.dot(qx, kt, preferred_element_type=F32) + bias_ref[...]
    update(s, lambda h: jnp.concatenate(
        [r[pl.ds(h, PAGE_SIZE, stride=N_HEADS), :].astype(BF16) for r in v_refs], axis=0))

    @pl.when(step == pl.num_programs(1) - 1)
    def _():
        pad = jnp.zeros((LANES - SUBLANES, D_MODEL), BF16)
        kn = jnp.concatenate([knew_ref[...], pad], axis=0)
        vn = jnp.concatenate([vnew_ref[...], pad], axis=0)
        sn = lax.dot_general(qx, kn, _NT, preferred_element_type=F32) + bnew_ref[...]
        update(sn, lambda h: vn[:, h * V_DIM:(h + 1) * V_DIM])
        acc = acc_ref[...]
        o = acc[:, :V_DIM] / acc[:, V_DIM:]
        heads = []
        for h in range(N_HEADS):
            r0 = h * rows_per_head
            oh = o[r0:r0 + DEC_SEQ, :] - lam_ref[0:1, 0:1] * o[r0 + DEC_SEQ:r0 + rows_per_head, :]
            oh = oh * lax.rsqrt(jnp.mean(oh * oh, axis=-1, keepdims=True) + LN_EPS) * gain_ref[...]
            heads.append(oh)
        o_ref[...] = jnp.concatenate(heads, axis=1)


def _attn_sample_tables(rel_bias):
    n_dist = PAST_LEN + DEC_SEQ
    far_first = _bias_by_distance(rel_bias, n_dist - 1 - jnp.arange(n_dist))
    past = jnp.stack([far_first[:, DEC_SEQ - 1 - t:DEC_SEQ - 1 - t + PAST_LEN]
                      for t in range(DEC_SEQ)], axis=1)
    rel = (jnp.arange(DEC_SEQ, dtype=jnp.int32)[:, None]
           - jnp.arange(LANES, dtype=jnp.int32)[None, :])
    near = _bias_by_distance(rel_bias, jnp.arange(DEC_SEQ))
    new = jnp.where(rel >= 0, near[:, jnp.clip(rel, 0, DEC_SEQ - 1)], NEG_INF)
    rows = lambda a: jnp.broadcast_to(a[:, None], (N_HEADS, 2) + a.shape[1:]).reshape(
        N_SROWS, a.shape[-1])
    col = jnp.arange(D_MODEL, dtype=jnp.int32)[None, :]
    row = jnp.arange(N_SROWS, dtype=jnp.int32)[:, None]
    qmask = ((col // V_DIM == row // (2 * DEC_SEQ))
             & ((col % V_DIM) // HEAD_DIM == (row // DEC_SEQ) % 2)).astype(F32)
    return rows(past), rows(new), qmask


def _attn_sample(q, k_new, v_new, cache_k, cache_v, page_table, tables, lam_tile, gain):
    past, new, qmask = tables
    n_phys = cache_k.shape[0]
    ck = jnp.transpose(cache_k, (0, 2, 3, 4, 1)).reshape(n_phys, D_MODEL, PAGE_SIZE)
    cv = cache_v.reshape(n_phys, PAGE_SIZE * N_HEADS, V_DIM)
    n_steps = N_PAGES // PAGES_PER_STEP
    keys_per_step = PAGES_PER_STEP * PAGE_SIZE

    def page_spec(r):
        return pl.BlockSpec((None, D_MODEL, PAGE_SIZE),
                            lambda b, s, pt: (pt[b * N_PAGES + s * PAGES_PER_STEP + r], 0, 0))

    per_seq = lambda rows: pl.BlockSpec((None, rows, D_MODEL), lambda b, s, pt: (b, 0, 0))
    const = lambda shape: pl.BlockSpec(shape, lambda b, s, pt: (0,) * len(shape))
    grid_spec = pltpu.PrefetchScalarGridSpec(
        num_scalar_prefetch=1,
        grid=(DEC_BATCH, n_steps),
        in_specs=[per_seq(DEC_SEQ), const(qmask.shape), per_seq(SUBLANES), per_seq(SUBLANES),
                  pl.BlockSpec((N_SROWS, keys_per_step), lambda b, s, pt: (0, s)),
                  const(new.shape), const(lam_tile.shape), const(gain.shape)]
                 + [page_spec(r) for r in range(PAGES_PER_STEP)] * 2,
        out_specs=per_seq(DEC_SEQ),
        scratch_shapes=[pltpu.VMEM((N_SROWS, D_MODEL), BF16), pltpu.VMEM((N_SROWS, LANES), F32),
                        pltpu.VMEM((N_SROWS, 2 * V_DIM), F32)],
    )
    return pl.pallas_call(
        _attn_sample_kernel,
        grid_spec=grid_spec,
        out_shape=jax.ShapeDtypeStruct((DEC_BATCH, DEC_SEQ, D_MODEL), F32),
        compiler_params=_params(("arbitrary", "arbitrary")),
        name="attn_sample",
    )(page_table.reshape(-1), q, qmask, k_new, v_new, past, new, lam_tile, gain,
      *([ck] * PAGES_PER_STEP), *([cv] * PAGES_PER_STEP))


def _to_batch_major(a):
    return jnp.transpose(a.reshape(DEC_SEQ, DEC_BATCH, D_MODEL), (1, 0, 2))


def _to_time_major(a):
    return jnp.transpose(a, (1, 0, 2)).reshape(DEC_SEQ * DEC_BATCH, D_MODEL)


def _trunk(x, mods, mods_kv, h0_re, h0_im, attend, w):
    ssm_re, ssm_im = [], []
    k = v = None
    for layer in range(DEPTH):
        if layer < N_A_LAYERS:
            if mods.mode == "p":
                g, fr, fi = _ssm_prompt(x, mods, layer, w["ssm"], w["ssm_d"])
            else:
                g, fr, fi = _ssm_sample(x, mods, layer, w["ssm"], w["ssm_d"], h0_re, h0_im)
            ssm_re.append(fr)
            ssm_im.append(fi)
            mixed, w_mix = g, w["glu"]
        else:
            j = layer - N_A_LAYERS
            (q,) = _mod_linear(x, mods, layer, 1, 0, w["wq"][j], [(0, BF16, False)],
                               scale=HEAD_DIM ** -0.5 * LOG2E)
            mixed, w_mix = attend(q, k, v, j), w["wo"]
        x = _block_tail(mixed, x, mods, layer, w_mix, w["w1"], w["w2"], w["ln_g"], w["ln_b"],
                        glu=layer < N_A_LAYERS)
        if layer == N_A_LAYERS - 1:
            outs = [(0, F32, False), (1, F32, False)] + [(0, F32, True)] * (mods.mode == "p")
            k, v, *k_out = _mod_linear(x, mods_kv, 0, 1, 0, w["wkv"], outs)
    return x, jnp.stack(ssm_re), jnp.stack(ssm_im), (k_out or [k])[0], v


def kernel(x_prompt, x_sample, state_ssm_re, state_ssm_im, cache_k, cache_v, page_table, c_prompt, c_sample, rel_bias, w_ada, b_ada, ln_g, ln_b, ssm_lam_re, ssm_lam_im, ssm_log_dt, ssm_b_re, ssm_b_im, ssm_c_re, ssm_c_im, ssm_d, ssm_w_glu_a, ssm_w_glu_b, w_ada_kv, b_ada_kv, w_kv, attn_w_q, attn_lam, attn_subln_g, attn_w_o, mlp_w1, mlp_w2):
    w = {
        "ssm": _ssm_prep(ssm_lam_re, ssm_lam_im, ssm_log_dt, ssm_b_re, ssm_b_im, ssm_c_re, ssm_c_im),
        "ssm_d": ssm_d,
        "glu": jnp.concatenate([ssm_w_glu_a, ssm_w_glu_b], axis=-1).astype(BF16),
        "wq": attn_w_q.astype(BF16),
        "wo": attn_w_o.astype(BF16),
        "wkv": w_kv.astype(BF16),
        "w1": mlp_w1.astype(BF16),
        "w2": mlp_w2.astype(BF16),
        "ln_g": ln_g,
        "ln_b": ln_b,
    }
    c_all = jnp.concatenate([c_sample, c_prompt, jnp.zeros((BATCH, D_MODEL), F32)], axis=0)
    ada = _ada(c_all, w_ada, b_ada)
    ada_kv = _ada(c_all, w_ada_kv[None], b_ada_kv[None])
    scalars = [_attn_scalars(attn_lam[j], attn_subln_g[j], N_A_LAYERS + j)
               for j in range(DEPTH - N_A_LAYERS)]

    prompt_tab = _attn_prompt_tables(rel_bias)

    def attend_prompt(q, k, v, j):
        return _attn_prompt(q, k, v, prompt_tab, *scalars[j])

    y_p, re_p, im_p, k_p, v_p = _trunk(x_prompt.reshape(BATCH * SEQ, D_MODEL), _Mods(ada, "p"),
                                       _Mods(ada_kv, "p"), None, None, attend_prompt, w)

    sample_tab = _attn_sample_tables(rel_bias)

    def attend_sample(q, k, v, j):
        pad = lambda a: jnp.pad(_to_batch_major(a).astype(BF16),
                                ((0, 0), (0, SUBLANES - DEC_SEQ), (0, 0)))
        o = _attn_sample(_to_batch_major(q), pad(k), pad(v), cache_k, cache_v, page_table,
                         sample_tab, *scalars[j])
        return _to_time_major(o).astype(BF16)

    y_s, re_s, im_s, k_s, v_s = _trunk(_to_time_major(x_sample), _Mods(ada, "s"), _Mods(ada_kv, "s"),
                                       state_ssm_re, state_ssm_im, attend_sample, w)

    return (y_p.reshape(BATCH, SEQ, D_MODEL), _to_batch_major(y_s), re_p, im_p,
            jnp.transpose(k_p.reshape(BATCH, N_HEADS, 2, HEAD_DIM, SEQ), (0, 4, 1, 2, 3)),
            v_p.reshape(BATCH, SEQ, N_HEADS, V_DIM),
            re_s, im_s,
            _to_batch_major(k_s).reshape(DEC_BATCH, DEC_SEQ, N_HEADS, 2, HEAD_DIM),
            _to_batch_major(v_s).reshape(DEC_BATCH, DEC_SEQ, N_HEADS, V_DIM))
```

```python
import functools
import math

import jax
import jax.numpy as jnp
from jax import lax
from jax.experimental import pallas as pl
from jax.experimental.pallas import tpu as pltpu

F32 = jnp.float32
BF16 = jnp.bfloat16

D_MODEL = 1024
BATCH = 4
SEQ = 4096
DEPTH = 4
DEC_BATCH = 128
DEC_SEQ = 4
PAST_LEN = 2048
PAGE_SIZE = 128
N_PAGES = PAST_LEN // PAGE_SIZE
N_A_LAYERS = DEPTH // 2
SSM_GROUP = 16
SSM_GROUPS = D_MODEL // SSM_GROUP
SSM_STATE = 64
N_STATE = SSM_GROUPS * SSM_STATE
N_HEADS = 8
HEAD_DIM = D_MODEL // (2 * N_HEADS)
V_DIM = 2 * HEAD_DIM
D_FF = 4 * D_MODEL
NUM_BUCKETS = 32
MAX_DISTANCE = 128
N_MOD = 6
ALPHA = (2.0 * DEPTH) ** 0.25
LN_EPS = 1e-5
NEG_INF = -1e30

SUBLANES = 8
LANES = 128
VMEM_LIMIT = 48 * 1024 * 1024

MOD_ROWS = DEC_BATCH + 2 * BATCH
CHUNK = LANES
N_CHUNK = D_MODEL // CHUNK
CHUNK_STATE = N_STATE // N_CHUNK
TM_PROMPT = 512
TILE = SUBLANES
T_SSM = 512
TQ = 512
HEADS_PER_STEP = 2
PAGES_PER_STEP = 16
LOG2E = math.log2(math.e)
FF_CHUNK = 1024


def _params(sem, vmem=VMEM_LIMIT):
    return pltpu.CompilerParams(dimension_semantics=sem, vmem_limit_bytes=vmem)


def _layer_norm(z, g, b):
    mu = jnp.mean(z, axis=-1, keepdims=True)
    zc = z - mu
    var = jnp.mean(zc * zc, axis=-1, keepdims=True)
    return zc * lax.rsqrt(var + LN_EPS) * g + b


def _ada_kernel(c_ref, w_ref, b_ref, o_ref):
    c = c_ref[...]
    sc = (c * jax.nn.sigmoid(c)).astype(BF16)
    w = w_ref[...].astype(BF16)
    o_ref[...] = jnp.dot(sc, w, preferred_element_type=F32) + b_ref[...]


def _ada(c_all, w, b):
    n_l, _, width = w.shape
    tn = 2048
    return pl.pallas_call(
        _ada_kernel,
        grid=(n_l, width // tn),
        in_specs=[
            pl.BlockSpec((MOD_ROWS, D_MODEL), lambda l, j: (0, 0)),
            pl.BlockSpec((None, D_MODEL, tn), lambda l, j: (l, 0, j)),
            pl.BlockSpec((None, 1, tn), lambda l, j: (l, 0, j)),
        ],
        out_specs=pl.BlockSpec((None, MOD_ROWS, tn), lambda l, j: (l, 0, j)),
        out_shape=jax.ShapeDtypeStruct((n_l, MOD_ROWS, width), F32),
        compiler_params=_params(("arbitrary", "arbitrary")),
        name="ada_mod",
    )(c_all, w, b.reshape(n_l, 1, width))


class _Mods:
    def __init__(self, arr, mode):
        self.mode = mode
        self.arr = arr if mode == "s" else arr.reshape(arr.shape[0], MOD_ROWS, 1, arr.shape[-1])

    def spec(self, layer, j, batch_of):
        if self.mode == "s":
            return pl.BlockSpec((None, DEC_BATCH, D_MODEL), lambda *g: (layer, 0, j))
        return pl.BlockSpec((None, None, 1, D_MODEL),
                            lambda *g: (layer, DEC_BATCH + batch_of(*g), 0, j))


def _row_tiling(mode):
    if mode == "s":
        return DEC_BATCH, lambda i: 0
    tiles_per_batch = SEQ // TM_PROMPT
    return TM_PROMPT, lambda i: i // tiles_per_batch


def _const_spec(shape):
    zeros = (0,) * len(shape)
    return pl.BlockSpec(shape, lambda *g: zeros, pipeline_mode=pl.Buffered(1))


def _mod_linear_kernel(x_ref, sc_ref, sh_ref, w_ref, *o_refs, blocks):
    h = (x_ref[...] * (1.0 + sc_ref[...]) + sh_ref[...]).astype(BF16)
    y = jnp.dot(h, w_ref[...], preferred_element_type=F32)
    for (n, transposed), o_ref in zip(blocks, o_refs):
        block = y[:, n * D_MODEL:(n + 1) * D_MODEL].astype(o_ref.dtype)
        o_ref[...] = block.T if transposed else block


def _mod_linear(x, mods, layer, j_scale, j_shift, w, outs):
    n = x.shape[0]
    tm, batch_of = _row_tiling(mods.mode)
    row = pl.BlockSpec((tm, D_MODEL), lambda i: (i, 0))
    tiles_per_batch = SEQ // tm
    col = pl.BlockSpec((None, D_MODEL, tm), lambda i: (i // tiles_per_batch, 0, i % tiles_per_batch))
    return pl.pallas_call(
        functools.partial(_mod_linear_kernel, blocks=tuple((o[0], o[2]) for o in outs)),
        grid=(n // tm,),
        in_specs=[row, mods.spec(layer, j_scale, batch_of), mods.spec(layer, j_shift, batch_of),
                  _const_spec(w.shape)],
        out_specs=[col if o[2] else row for o in outs],
        out_shape=[jax.ShapeDtypeStruct((BATCH, D_MODEL, SEQ) if o[2] else (n, D_MODEL), o[1])
                   for o in outs],
        compiler_params=_params(("arbitrary",)),
        name="mod_linear",
    )(x, mods.arr, mods.arr, w)


def _block_tail_kernel(a_ref, x_ref, gm_ref, sc_ref, sh_ref, gf_ref, wm_ref, w1_ref, w2_ref,
                       g_ref, b_ref, *rest, glu, q_scale):
    o_ref = rest[-2] if q_scale else rest[-1]
    y = jnp.dot(a_ref[...], wm_ref[...], preferred_element_type=F32)
    if glu:
        y = y[:, :D_MODEL] * jax.nn.sigmoid(y[:, D_MODEL:])
    x = _layer_norm(ALPHA * x_ref[...] + (1.0 + gm_ref[...]) * y, g_ref[0:1, :], b_ref[0:1, :])
    h = (x * (1.0 + sc_ref[...]) + sh_ref[...]).astype(BF16)
    acc = jnp.zeros(x.shape, F32)
    for c in range(D_FF // FF_CHUNK):
        cols = slice(c * FF_CHUNK, (c + 1) * FF_CHUNK)
        hid = jnp.dot(h, w1_ref[:, cols], preferred_element_type=F32)
        hid = jnp.square(jnp.maximum(hid, 0.0)).astype(BF16)
        acc = acc + jnp.dot(hid, w2_ref[cols, :], preferred_element_type=F32)
    z = ALPHA * x + (1.0 + gf_ref[...]) * acc
    out = _layer_norm(z, g_ref[1:2, :], b_ref[1:2, :])
    o_ref[...] = out
    if q_scale:
        qsc_ref, qsh_ref, wq_ref, _, q_ref = rest
        hq = (out * (1.0 + qsc_ref[...]) + qsh_ref[...]).astype(BF16)
        q = jnp.dot(hq, wq_ref[...], preferred_element_type=F32) * q_scale
        q_ref[...] = q.astype(q_ref.dtype)


def _block_tail(a, x, mods, layer, w_mix, w1, w2, ln_g, ln_b, glu, w_q=None, q_scale=None):
    n = x.shape[0]
    tm, batch_of = _row_tiling(mods.mode)
    row = pl.BlockSpec((tm, D_MODEL), lambda i: (i, 0))
    ln = pl.BlockSpec((None, 2, D_MODEL), lambda i: (layer, 0, 0))
    weight = lambda w, l=layer: pl.BlockSpec((None,) + w.shape[1:],
                                             lambda i: (l % w.shape[0], 0, 0),
                                             pipeline_mode=pl.Buffered(1))
    in_specs = ([row, row] + [mods.spec(layer, j, batch_of) for j in (2, 4, 3, 5)]
                + [weight(w_mix), weight(w1), weight(w2), ln, ln])
    args = [a, x, mods.arr, mods.arr, mods.arr, mods.arr, w_mix, w1, w2, ln_g, ln_b]
    out_specs, out_shape = [row], [jax.ShapeDtypeStruct((n, D_MODEL), F32)]
    if w_q is not None:
        nxt = layer + 1
        in_specs += [mods.spec(nxt, 1, batch_of), mods.spec(nxt, 0, batch_of),
                     weight(w_q, nxt - N_A_LAYERS)]
        args += [mods.arr, mods.arr, w_q]
        out_specs.append(row)
        out_shape.append(jax.ShapeDtypeStruct((n, D_MODEL), BF16))
    return pl.pallas_call(
        functools.partial(_block_tail_kernel, glu=glu, q_scale=q_scale),
        grid=(n // tm,),
        in_specs=in_specs,
        out_specs=out_specs,
        out_shape=out_shape,
        compiler_params=_params(("arbitrary",)),
        name="block_tail",
    )(*args)


def _cmul(ar, ai, br, bi):
    return ar * br - ai * bi, ar * bi + ai * br


def _ssm_prep(lam_re, lam_im, log_dt, b_re, b_im, c_re, c_im):
    n_l = lam_re.shape[0]
    dt = jnp.exp(log_dt)[..., None]
    n_scan = TILE + SUBLANES
    ns = (list(range(TILE + 1)) + [TILE * m for m in range(2, SUBLANES + 1)]
          + list(range(TILE - 1, -1, -1)))
    n = jnp.asarray(ns, F32).reshape(-1, 1, 1, 1)
    mag = jnp.exp(n * (lam_re * dt))
    pr = mag * jnp.cos(n * (lam_im * dt))
    pi = mag * jnp.sin(n * (lam_im * dt))
    ar, ai = pr[1], pi[1]
    den = lam_re * lam_re + lam_im * lam_im
    er = ((ar - 1.0) * lam_re + ai * lam_im) / den
    ei = (ai * lam_re - (ar - 1.0) * lam_im) / den
    bbr = er[..., None] * b_re - ei[..., None] * b_im
    bbi = er[..., None] * b_im + ei[..., None] * b_re
    gpc = CHUNK // SSM_GROUP
    exact = lax.Precision.HIGHEST

    def block_diag(m, rows_per_group, cols_per_group, steps=1):
        y = steps * cols_per_group
        c = jnp.arange(steps * gpc * cols_per_group)
        step, group, inner = c // (gpc * cols_per_group), (c // cols_per_group) % gpc, c % cols_per_group
        select = (jnp.arange(y)[:, None] == (step * cols_per_group + inner)[None, :]).astype(BF16)
        wide = jnp.einsum("...ry,yc->...rc", m.astype(BF16), select, preferred_element_type=BF16)
        row_group = (jnp.arange(m.shape[-2]) // rows_per_group) % gpc
        return jnp.where(row_group[:, None] == group[None, :], wide, jnp.zeros((), BF16))

    def per_chunk(m, order):
        x, _, _, a, b = m.shape
        m = m.reshape(x, n_l, N_CHUNK, gpc, a, b)
        return jnp.transpose(m, (1, 2) + tuple({"x": 0, "g": 3, "a": 4, "b": 5}[o] for o in order))

    swap = lambda m: jnp.swapaxes(m, -1, -2)
    b_in = per_chunk(jnp.concatenate([swap(bbr), swap(bbi)], axis=-1)[None], "gaxb")
    bcat = block_diag(b_in.reshape(n_l, N_CHUNK, CHUNK, 2 * SSM_STATE), SSM_GROUP, SSM_STATE,
                      steps=2)
    c_out = per_chunk(jnp.stack([swap(c_re), -swap(c_im)]), "xgab")
    ccat = block_diag(c_out.reshape(n_l, N_CHUNK, 2 * CHUNK_STATE, SSM_GROUP), SSM_STATE, SSM_GROUP)

    abr, abi = _cmul(pr[:TILE, ..., None], pi[:TILE, ..., None], bbr, bbi)
    dot_p = functools.partial(jnp.einsum, "lghp,jlgpi->jlgih", precision=exact)
    taps = per_chunk(dot_p(c_re, abr) - dot_p(c_im, abi), "xgab")
    kcat = block_diag(taps.reshape(n_l, N_CHUNK, TILE * CHUNK, SSM_GROUP), SSM_GROUP, SSM_GROUP)

    wr, wi = _cmul(pr[n_scan:, ..., None], pi[n_scan:, ..., None], bbr, bbi)
    ends = per_chunk(jnp.concatenate([swap(wr), swap(wi)], axis=-1), "xgab")
    wst = block_diag(ends.reshape(n_l, N_CHUNK, TILE * CHUNK, 2 * SSM_STATE), SSM_GROUP, SSM_STATE,
                     steps=2)

    mr, mi = _cmul(pr[1:TILE + 1, :, :, None, :], pi[1:TILE + 1, :, :, None, :], c_re, c_im)
    outs = jnp.stack([per_chunk(m, "gbxa") for m in (mr, -mi)], axis=2)
    call = block_diag(outs.reshape(n_l, N_CHUNK, 2 * CHUNK_STATE, TILE * SSM_GROUP), SSM_STATE,
                      SSM_GROUP, steps=TILE)

    flat = lambda a: jnp.moveaxis(a, 0, 1).reshape(n_l, -1, N_STATE)
    rows = lax.broadcasted_iota(jnp.int32, (SUBLANES, N_STATE), 0)
    parts = []
    for a in (flat(pr[TILE:n_scan]), flat(pi[TILE:n_scan])):
        masked = lambda m, first: jnp.where(rows >= first, a[:, m - 1:m, :], 0.0)
        parts.append([masked(1, 1), masked(2, 2), masked(4, 4), a, masked(SUBLANES, 0)])
    tab = jnp.stack([p for pair in zip(*parts) for p in pair], axis=1)
    atab = jnp.stack([jnp.broadcast_to(flat(a[1:2]), (n_l, SUBLANES, N_STATE)) for a in (pr, pi)],
                     axis=1)
    return dict(bcat=bcat, ccat=ccat, kcat=kcat, wst=wst, call=call, tab=tab, atab=atab)


def _ssm_prompt_kernel(x_ref, sc_ref, sh_ref, kcat_ref, wst_ref, call_ref, d_ref, tab_ref,
                       g_ref, hre_ref, him_ref, h_ref, buf_ref, y_ref, taps_ref):
    n_tiles = SEQ // TILE
    half = CHUNK_STATE

    @pl.when(pl.program_id(1) == 0)
    def _():
        zero = jnp.zeros((CHUNK, CHUNK), BF16)
        for r in range(TILE):
            for s in range(TILE):
                lag = s - r
                block = kcat_ref[lag * CHUNK:(lag + 1) * CHUNK, :] if lag >= 0 else zero
                taps_ref[r * CHUNK:(r + 1) * CHUNK, s * CHUNK:(s + 1) * CHUNK] = block

    h_ref[...] = x_ref[...] * (1.0 + sc_ref[...]) + sh_ref[...]

    steps = [h_ref[pl.ds(r, n_tiles, stride=TILE), :].astype(BF16) for r in range(TILE)]
    tiles = jnp.concatenate(steps, axis=1)
    buf_ref[...] = jnp.dot(tiles, wst_ref[...], preferred_element_type=F32)
    y_tiles = jnp.dot(tiles, taps_ref[...], preferred_element_type=F32)

    row = lax.broadcasted_iota(jnp.int32, (SUBLANES, half), 0)

    def tile_scan(i, carry):
        cr, ci = carry
        r0 = pl.multiple_of(i * SUBLANES, SUBLANES)
        vr = buf_ref[pl.ds(r0, SUBLANES), 0:half]
        vi = buf_ref[pl.ds(r0, SUBLANES), half:2 * half]
        for lvl, shift in enumerate((1, 2, 4)):
            pr, pi = _cmul(tab_ref[2 * lvl], tab_ref[2 * lvl + 1],
                           pltpu.roll(vr, shift, 0), pltpu.roll(vi, shift, 0))
            vr, vi = vr + pr, vi + pi
        pr, pi = _cmul(tab_ref[6], tab_ref[7], cr, ci)
        buf_ref[pl.ds(r0, SUBLANES), 0:half] = jnp.where(row == 0, cr, pltpu.roll(vr + pr, 1, 0))
        buf_ref[pl.ds(r0, SUBLANES), half:2 * half] = jnp.where(row == 0, ci,
                                                                 pltpu.roll(vi + pi, 1, 0))
        pr, pi = _cmul(tab_ref[8], tab_ref[9], cr, ci)
        last = SUBLANES - 1
        return (jnp.broadcast_to(vr[last:, :], vr.shape) + pr,
                jnp.broadcast_to(vi[last:, :], vi.shape) + pi)

    zero = jnp.zeros((SUBLANES, half), F32)
    cr, ci = lax.fori_loop(0, n_tiles // SUBLANES, tile_scan, (zero, zero))
    hre_ref[...] = cr[0:1, :]
    him_ref[...] = ci[0:1, :]

    y_tiles = y_tiles + jnp.dot(buf_ref[...].astype(BF16), call_ref[...],
                                preferred_element_type=F32)
    for s in range(TILE):
        y_ref[pl.ds(s, n_tiles, stride=TILE), :] = y_tiles[:, s * CHUNK:(s + 1) * CHUNK]

    y = y_ref[...] + d_ref[...] * h_ref[...]
    g_ref[...] = jax.nn.gelu(y).astype(g_ref.dtype)


def _ssm_prompt(x, mods, layer, p, d_skip):
    col = pl.BlockSpec((SEQ, CHUNK), lambda k, b: (b, k))
    fin = pl.BlockSpec((None, 1, CHUNK_STATE), lambda k, b: (b, 0, k))
    mod = lambda j: pl.BlockSpec((None, None, 1, CHUNK),
                                 lambda k, b: (layer, DEC_BATCH + b, 0, j * N_CHUNK + k))
    weight = lambda a: pl.BlockSpec((None, None) + a.shape[2:], lambda k, b: (layer, k, 0, 0))
    g, hre, him = pl.pallas_call(
        _ssm_prompt_kernel,
        grid=(N_CHUNK, BATCH),
        in_specs=[col, mod(1), mod(0), weight(p["kcat"]), weight(p["wst"]), weight(p["call"]),
                  pl.BlockSpec((None, 1, CHUNK), lambda k, b: (layer, 0, k)),
                  pl.BlockSpec((None, p["tab"].shape[1], SUBLANES, CHUNK_STATE),
                               lambda k, b: (layer, 0, 0, k))],
        out_specs=[col, fin, fin],
        out_shape=[jax.ShapeDtypeStruct((BATCH * SEQ, D_MODEL), BF16),
                   jax.ShapeDtypeStruct((BATCH, 1, N_STATE), F32),
                   jax.ShapeDtypeStruct((BATCH, 1, N_STATE), F32)],
        scratch_shapes=[pltpu.VMEM((SEQ, CHUNK), F32),
                        pltpu.VMEM((SEQ // TILE, 2 * CHUNK_STATE), F32),
                        pltpu.VMEM((SEQ, CHUNK), F32),
                        pltpu.VMEM((TILE * CHUNK, TILE * CHUNK), BF16)],
        compiler_params=_params(("arbitrary", "arbitrary")),
        name="ssm_prompt",
    )(x, mods.arr, mods.arr, p["kcat"], p["wst"], p["call"], d_skip[:, None, :], p["tab"])
    shape = (BATCH, SSM_GROUPS, SSM_STATE)
    return g, hre.reshape(shape), him.reshape(shape)


def _ssm_sample_kernel(x_ref, sc_ref, sh_ref, bcat_ref, ccat_ref, d_ref, a_ref, h0r_ref, h0i_ref,
                       g_ref, hre_ref, him_ref):
    tile = lambda m: jnp.concatenate([m] * DEC_SEQ, axis=0)
    h = x_ref[...] * (1.0 + tile(sc_ref[...])) + tile(sh_ref[...])
    bu = jnp.dot(h.astype(BF16), bcat_ref[...], preferred_element_type=F32)
    ar = a_ref[0, 0:1, :]
    ai = a_ref[1, 0:1, :]
    sr = h0r_ref[...].T
    si = h0i_ref[...].T
    states = []
    for t in range(DEC_SEQ):
        rows = slice(t * DEC_BATCH, (t + 1) * DEC_BATCH)
        pr, pi = _cmul(ar, ai, sr, si)
        sr = pr + bu[rows, :CHUNK_STATE]
        si = pi + bu[rows, CHUNK_STATE:]
        states.append(jnp.concatenate([sr, si], axis=1))
    xs = jnp.concatenate(states, axis=0).astype(BF16)
    y = jnp.dot(xs, ccat_ref[...], preferred_element_type=F32) + d_ref[...] * h
    g_ref[...] = jax.nn.gelu(y).astype(g_ref.dtype)
    hre_ref[...] = sr.T
    him_ref[...] = si.T


def _ssm_sample(x, mods, layer, p, d_skip, h0_re, h0_im):
    n = DEC_BATCH * DEC_SEQ
    col = pl.BlockSpec((n, CHUNK), lambda k: (0, k))
    st = pl.BlockSpec((None, CHUNK_STATE, DEC_BATCH), lambda k: (layer, k, 0))
    mod = lambda j: pl.BlockSpec((None, DEC_BATCH, CHUNK), lambda k: (layer, 0, j * N_CHUNK + k))
    state_major = lambda a: jnp.transpose(a, (0, 2, 3, 1)).reshape(-1, N_STATE, DEC_BATCH)
    g, hre, him = pl.pallas_call(
        _ssm_sample_kernel,
        grid=(N_CHUNK,),
        in_specs=[col, mod(1), mod(0),
                  pl.BlockSpec((None, None, CHUNK, 2 * CHUNK_STATE), lambda k: (layer, k, 0, 0)),
                  pl.BlockSpec((None, None, 2 * CHUNK_STATE, CHUNK), lambda k: (layer, k, 0, 0)),
                  pl.BlockSpec((None, 1, CHUNK), lambda k: (layer, 0, k)),
                  pl.BlockSpec((None, 2, SUBLANES, CHUNK_STATE), lambda k: (layer, 0, 0, k)),
                  st, st],
        out_specs=[col] + [pl.BlockSpec((CHUNK_STATE, DEC_BATCH), lambda k: (k, 0))] * 2,
        out_shape=[jax.ShapeDtypeStruct((n, D_MODEL), BF16),
                   jax.ShapeDtypeStruct((N_STATE, DEC_BATCH), F32),
                   jax.ShapeDtypeStruct((N_STATE, DEC_BATCH), F32)],
        compiler_params=_params(("arbitrary",)),
        name="ssm_sample",
    )(x, mods.arr, mods.arr, p["bcat"], p["ccat"], d_skip[:, None, :], p["atab"],
      state_major(h0_re), state_major(h0_im))
    batch_major = lambda a: jnp.transpose(a.reshape(SSM_GROUPS, SSM_STATE, DEC_BATCH), (2, 0, 1))
    return g, batch_major(hre), batch_major(him)


def _rel_bucket(rel):
    n = jnp.maximum(rel, 0)
    max_exact = NUM_BUCKETS // 2
    large = max_exact + (jnp.log(jnp.maximum(n, 1).astype(F32) / max_exact)
                         / math.log(MAX_DISTANCE / max_exact) * (NUM_BUCKETS - max_exact)).astype(jnp.int32)
    large = jnp.minimum(large, NUM_BUCKETS - 1)
    return jnp.where(n < max_exact, n, large)


def _attn_scalars(attn_lam, subln_g, layer):
    lam_init = 0.8 - 0.6 * math.exp(-0.3 * layer)
    lam = (jnp.exp(jnp.sum(attn_lam[0] * attn_lam[1])) - jnp.exp(jnp.sum(attn_lam[2] * attn_lam[3]))
           + lam_init)
    lam_tile = jnp.full((SUBLANES, LANES), lam, F32)
    gain = (subln_g * (1.0 - lam_init)).reshape(1, V_DIM)
    return lam_tile, gain


def _bias_by_distance(rel_bias, dist):
    bv = rel_bias[_rel_bucket(dist.astype(jnp.int32))]
    return ((bv - rel_bias[NUM_BUCKETS - 1]) * LOG2E).T


def _softmax_update(s, m_prev):
    m_new = jnp.maximum(m_prev, jnp.max(s, axis=1, keepdims=True))
    p = jnp.exp2(s - jnp.concatenate([m_new] * (s.shape[1] // LANES), axis=1))
    return m_new, p.astype(BF16), jnp.exp2(m_prev - m_new)


def _with_ones(v):
    return jnp.concatenate([v, jnp.ones((v.shape[0], V_DIM), BF16)], axis=1)


def _acc_update(acc, alpha, p, vx):
    return (jnp.concatenate([alpha, alpha], axis=1) * acc
            + jnp.dot(p, vx, preferred_element_type=F32))


_NT = (((1,), (1,)), ((), ()))


def _attn_prompt_kernel(q_ref, k_ref, v_ref, tab_ref, lam_ref, gain_ref, o_ref,
                        q2_ref, m_ref, acc_ref, s_ref):
    i = pl.program_id(2)
    last = pl.num_programs(2) - 1

    def key_rows(j):
        return pl.ds(pl.multiple_of(j * TQ, TQ), TQ)

    def stack_queries(hh, tile):
        q = q_ref[key_rows(tile), hh * V_DIM:(hh + 1) * V_DIM]
        lane = lax.broadcasted_iota(jnp.int32, q.shape, 1)
        zero = jnp.zeros_like(q)
        q2_ref[hh, 0:TQ, :] = jnp.where(lane < HEAD_DIM, q, zero)
        q2_ref[hh, TQ:2 * TQ, :] = jnp.where(lane >= HEAD_DIM, q, zero)

    def logits(hh, j):
        cols = slice(hh * V_DIM, (hh + 1) * V_DIM)
        return lax.dot_general(q2_ref[hh], k_ref[key_rows(j), cols].astype(BF16), _NT,
                               preferred_element_type=F32)

    @pl.when(i == 0)
    def _():
        stack_queries(0, 0)
        s_ref[...] = logits(0, 0)

    for hh in range(1, HEADS_PER_STEP):
        stack_queries(hh, i)
    m_ref[...] = jnp.full(m_ref.shape, NEG_INF, F32)
    acc_ref[...] = jnp.zeros_like(acc_ref)

    def consume(hh, j, s, table):
        cols = slice(hh * V_DIM, (hh + 1) * V_DIM)
        if table is not None:
            bias = tab_ref[hh, table]
            s = s + jnp.concatenate([bias, bias], axis=0)
        m_new, p, alpha = _softmax_update(s, m_ref[hh])
        acc_ref[hh] = _acc_update(acc_ref[hh], alpha, p,
                                  _with_ones(v_ref[key_rows(j), cols].astype(BF16)))
        m_ref[hh] = m_new

    def step(j, table, same_tile):
        consume(0, j, s_ref[...], table)
        for hh in range(1, HEADS_PER_STEP):
            consume(hh, j, logits(hh, j), table)
        if same_tile:
            s_ref[...] = logits(0, j + 1)
        else:
            stack_queries(0, jnp.minimum(i + 1, last))
            s_ref[...] = logits(0, 0)

    def far(j, c):
        step(j, None, True)
        return c

    lax.fori_loop(0, jnp.maximum(i - 1, 0), far, 0)

    @pl.when(i >= 1)
    def _():
        step(i - 1, 1, True)

    step(i, 0, False)

    for hh in range(HEADS_PER_STEP):
        acc = acc_ref[hh]
        o = acc[:, :V_DIM] / acc[:, V_DIM:]
        o = o[0:TQ, :] - lam_ref[0:1, 0:1] * o[TQ:2 * TQ, :]
        o = o * lax.rsqrt(jnp.mean(o * o, axis=-1, keepdims=True) + LN_EPS) * gain_ref[...]
        o_ref[:, hh * V_DIM:(hh + 1) * V_DIM] = o.astype(o_ref.dtype)


def _attn_prompt_tables(rel_bias):
    m = jnp.arange(2 * TQ)
    behind = (2 * TQ - m) % (2 * TQ)
    diag = jnp.where((m >= 1) & (m <= TQ), NEG_INF, _bias_by_distance(rel_bias, behind))
    prev = _bias_by_distance(rel_bias, jnp.where(m <= TQ, TQ - m, 3 * TQ - m))
    vec = jnp.broadcast_to(jnp.stack([diag, prev], axis=1)[:, :, None, :],
                           (N_HEADS, 2, SUBLANES, 2 * TQ))
    return pl.pallas_call(
        _toeplitz_kernel,
        grid=(N_HEADS,),
        in_specs=[pl.BlockSpec((None, 2, SUBLANES, 2 * TQ), lambda h: (h, 0, 0, 0))],
        out_specs=pl.BlockSpec((None, 2, TQ, TQ), lambda h: (h, 0, 0, 0)),
        out_shape=jax.ShapeDtypeStruct((N_HEADS, 2, TQ, TQ), F32),
        compiler_params=_params(("arbitrary",)),
        name="bias_tiles",
    )(vec)


def _toeplitz_kernel(vec_ref, o_ref):
    for t in range(o_ref.shape[0]):
        rows = jnp.broadcast_to(vec_ref[t, 0:1, :], (TQ, 2 * TQ))
        o_ref[t] = pltpu.roll(rows, 0, 1, stride=1, stride_axis=0)[:, :TQ]


def _attn_prompt(q, k, v, tab, lam_tile, gain):
    n_q = SEQ // TQ
    width = HEADS_PER_STEP * V_DIM
    shape3 = (BATCH, SEQ, D_MODEL)
    qspec = pl.BlockSpec((None, TQ, width), lambda b, h, i: (b, i, h))
    kvspec = pl.BlockSpec((None, SEQ, width), lambda b, h, i: (b, 0, h))
    o = pl.pallas_call(
        _attn_prompt_kernel,
        grid=(BATCH, N_HEADS // HEADS_PER_STEP, n_q),
        in_specs=[kvspec, kvspec, kvspec,
                  pl.BlockSpec((HEADS_PER_STEP, 2, TQ, TQ), lambda b, h, i: (h, 0, 0, 0)),
                  pl.BlockSpec((SUBLANES, LANES), lambda b, h, i: (0, 0)),
                  pl.BlockSpec((1, V_DIM), lambda b, h, i: (0, 0))],
        out_specs=qspec,
        out_shape=jax.ShapeDtypeStruct(shape3, BF16),
        scratch_shapes=[pltpu.VMEM((HEADS_PER_STEP, 2 * TQ, V_DIM), BF16),
                        pltpu.VMEM((HEADS_PER_STEP, 2 * TQ, LANES), F32),
                        pltpu.VMEM((HEADS_PER_STEP, 2 * TQ, 2 * V_DIM), F32),
                        pltpu.VMEM((2 * TQ, TQ), F32)],
        compiler_params=_params(("arbitrary", "arbitrary", "arbitrary")),
        name="attn_prompt",
    )(q.reshape(shape3), k.reshape(shape3), v.reshape(shape3), tab, lam_tile, gain)
    return o.reshape(BATCH * SEQ, D_MODEL)


N_SROWS = N_HEADS * 2 * DEC_SEQ


def _attn_sample_kernel(pt_ref, q_ref, qmask_ref, knew_ref, vnew_ref, bias_ref, bnew_ref,
                        lam_ref, gain_ref, *rest):
    del pt_ref
    k_refs = rest[:PAGES_PER_STEP]
    v_refs = rest[PAGES_PER_STEP:2 * PAGES_PER_STEP]
    o_ref, qx_ref, m_ref, acc_ref = rest[2 * PAGES_PER_STEP:]
    step = pl.program_id(1)
    rows_per_head = 2 * DEC_SEQ

    @pl.when(step == 0)
    def _():
        q = q_ref[...].astype(F32)
        q = jnp.concatenate([q, q], axis=0)
        qx_ref[...] = (jnp.concatenate([q] * N_HEADS, axis=0) * qmask_ref[...]).astype(BF16)
        m_ref[...] = jnp.full(m_ref.shape, NEG_INF, F32)
        acc_ref[...] = jnp.zeros_like(acc_ref)

    def update(s, values_of_head):
        m_new, p, alpha = _softmax_update(s, m_ref[...])
        for h in range(N_HEADS):
            r = slice(h * rows_per_head, (h + 1) * rows_per_head)
            acc_ref[r, :] = _acc_update(acc_ref[r, :], alpha[r, :], p[r, :],
                                        _with_ones(values_of_head(h)))
        m_ref[...] = m_new

    qx = qx_ref[...]
    kt = jnp.concatenate([r[...].astype(BF16) for r in k_refs], axis=1)
    s = jnp.dot(qx, kt, preferred_element_type=F32) + bias_ref[...]
    update(s, lambda h: jnp.concatenate(
        [r[pl.ds(h, PAGE_SIZE, stride=N_HEADS), :].astype(BF16) for r in v_refs], axis=0))

    @pl.when(step == pl.num_programs(1) - 1)
    def _():
        pad = jnp.zeros((LANES - SUBLANES, D_MODEL), BF16)
        kn = jnp.concatenate([knew_ref[...], pad], axis=0)
        vn = jnp.concatenate([vnew_ref[...], pad], axis=0)
        sn = lax.dot_general(qx, kn, _NT, preferred_element_type=F32) + bnew_ref[...]
        update(sn, lambda h: vn[:, h * V_DIM:(h + 1) * V_DIM])
        acc = acc_ref[...]
        o = acc[:, :V_DIM] / acc[:, V_DIM:]
        heads = []
        for h in range(N_HEADS):
            r0 = h * rows_per_head
            oh = o[r0:r0 + DEC_SEQ, :] - lam_ref[0:1, 0:1] * o[r0 + DEC_SEQ:r0 + rows_per_head, :]
            oh = oh * lax.rsqrt(jnp.mean(oh * oh, axis=-1, keepdims=True) + LN_EPS) * gain_ref[...]
            heads.append(oh)
        o_ref[...] = jnp.concatenate(heads, axis=1)


def _attn_sample_tables(rel_bias):
    n_dist = PAST_LEN + DEC_SEQ
    far_first = _bias_by_distance(rel_bias, n_dist - 1 - jnp.arange(n_dist))
    past = jnp.stack([far_first[:, DEC_SEQ - 1 - t:DEC_SEQ - 1 - t + PAST_LEN]
                      for t in range(DEC_SEQ)], axis=1)
    rel = (jnp.arange(DEC_SEQ, dtype=jnp.int32)[:, None]
           - jnp.arange(LANES, dtype=jnp.int32)[None, :])
    near = _bias_by_distance(rel_bias, jnp.arange(DEC_SEQ))
    new = jnp.where(rel >= 0, near[:, jnp.clip(rel, 0, DEC_SEQ - 1)], NEG_INF)
    rows = lambda a: jnp.broadcast_to(a[:, None], (N_HEADS, 2) + a.shape[1:]).reshape(
        N_SROWS, a.shape[-1])
    col = jnp.arange(D_MODEL, dtype=jnp.int32)[None, :]
    row = jnp.arange(N_SROWS, dtype=jnp.int32)[:, None]
    qmask = ((col // V_DIM == row // (2 * DEC_SEQ))
             & ((col % V_DIM) // HEAD_DIM == (row // DEC_SEQ) % 2)).astype(F32)
    return rows(past), rows(new), qmask


def _attn_sample(q, k_new, v_new, cache_k, cache_v, page_table, tables, lam_tile, gain):
    past, new, qmask = tables
    n_phys = cache_k.shape[0]
    ck = jnp.transpose(cache_k, (0, 2, 3, 4, 1)).reshape(n_phys, D_MODEL, PAGE_SIZE)
    cv = cache_v.reshape(n_phys, PAGE_SIZE * N_HEADS, V_DIM)
    n_steps = N_PAGES // PAGES_PER_STEP
    keys_per_step = PAGES_PER_STEP * PAGE_SIZE

    def page_spec(r):
        return pl.BlockSpec((None, D_MODEL, PAGE_SIZE),
                            lambda b, s, pt: (pt[b * N_PAGES + s * PAGES_PER_STEP + r], 0, 0))

    per_seq = lambda rows: pl.BlockSpec((None, rows, D_MODEL), lambda b, s, pt: (b, 0, 0))
    const = lambda shape: pl.BlockSpec(shape, lambda b, s, pt: (0,) * len(shape))
    grid_spec = pltpu.PrefetchScalarGridSpec(
        num_scalar_prefetch=1,
        grid=(DEC_BATCH, n_steps),
        in_specs=[per_seq(DEC_SEQ), const(qmask.shape), per_seq(SUBLANES), per_seq(SUBLANES),
                  pl.BlockSpec((N_SROWS, keys_per_step), lambda b, s, pt: (0, s)),
                  const(new.shape), const(lam_tile.shape), const(gain.shape)]
                 + [page_spec(r) for r in range(PAGES_PER_STEP)] * 2,
        out_specs=per_seq(DEC_SEQ),
        scratch_shapes=[pltpu.VMEM((N_SROWS, D_MODEL), BF16), pltpu.VMEM((N_SROWS, LANES), F32),
                        pltpu.VMEM((N_SROWS, 2 * V_DIM), F32)],
    )
    return pl.pallas_call(
        _attn_sample_kernel,
        grid_spec=grid_spec,
        out_shape=jax.ShapeDtypeStruct((DEC_BATCH, DEC_SEQ, D_MODEL), F32),
        compiler_params=_params(("arbitrary", "arbitrary")),
        name="attn_sample",
    )(page_table.reshape(-1), q, qmask, k_new, v_new, past, new, lam_tile, gain,
      *([ck] * PAGES_PER_STEP), *([cv] * PAGES_PER_STEP))


def _to_batch_major(a):
    return jnp.transpose(a.reshape(DEC_SEQ, DEC_BATCH, D_MODEL), (1, 0, 2))


def _to_time_major(a):
    return jnp.transpose(a, (1, 0, 2)).reshape(DEC_SEQ * DEC_BATCH, D_MODEL)


def _trunk(x, mods, mods_kv, h0_re, h0_im, attend, w):
    ssm_re, ssm_im = [], []
    k = v = q = None
    for layer in range(DEPTH):
        if layer < N_A_LAYERS:
            if mods.mode == "p":
                g, fr, fi = _ssm_prompt(x, mods, layer, w["ssm"], w["ssm_d"])
            else:
                g, fr, fi = _ssm_sample(x, mods, layer, w["ssm"], w["ssm_d"], h0_re, h0_im)
            ssm_re.append(fr)
            ssm_im.append(fi)
            mixed, w_mix = g, w["glu"]
        else:
            mixed, w_mix = attend(q, k, v, layer - N_A_LAYERS), w["wo"]
        feeds_attention = N_A_LAYERS <= layer + 1 < DEPTH
        x, *q_next = _block_tail(mixed, x, mods, layer, w_mix, w["w1"], w["w2"], w["ln_g"],
                                 w["ln_b"], glu=layer < N_A_LAYERS,
                                 w_q=w["wq"] if feeds_attention else None,
                                 q_scale=HEAD_DIM ** -0.5 * LOG2E if feeds_attention else None)
        q = q_next[0] if q_next else None
        if layer == N_A_LAYERS - 1:
            outs = [(0, F32, False), (1, F32, False)] + [(0, F32, True)] * (mods.mode == "p")
            k, v, *k_out = _mod_linear(x, mods_kv, 0, 1, 0, w["wkv"], outs)
    return x, jnp.stack(ssm_re), jnp.stack(ssm_im), (k_out or [k])[0], v


def kernel(x_prompt, x_sample, state_ssm_re, state_ssm_im, cache_k, cache_v, page_table, c_prompt, c_sample, rel_bias, w_ada, b_ada, ln_g, ln_b, ssm_lam_re, ssm_lam_im, ssm_log_dt, ssm_b_re, ssm_b_im, ssm_c_re, ssm_c_im, ssm_d, ssm_w_glu_a, ssm_w_glu_b, w_ada_kv, b_ada_kv, w_kv, attn_w_q, attn_lam, attn_subln_g, attn_w_o, mlp_w1, mlp_w2):
    w = {
        "ssm": _ssm_prep(ssm_lam_re, ssm_lam_im, ssm_log_dt, ssm_b_re, ssm_b_im, ssm_c_re, ssm_c_im),
        "ssm_d": ssm_d,
        "glu": jnp.concatenate([ssm_w_glu_a, ssm_w_glu_b], axis=-1).astype(BF16),
        "wq": attn_w_q.astype(BF16),
        "wo": attn_w_o.astype(BF16),
        "wkv": w_kv.astype(BF16),
        "w1": mlp_w1.astype(BF16),
        "w2": mlp_w2.astype(BF16),
        "ln_g": ln_g,
        "ln_b": ln_b,
    }
    c_all = jnp.concatenate([c_sample, c_prompt, jnp.zeros((BATCH, D_MODEL), F32)], axis=0)
    ada = _ada(c_all, w_ada, b_ada)
    ada_kv = _ada(c_all, w_ada_kv[None], b_ada_kv[None])
    scalars = [_attn_scalars(attn_lam[j], attn_subln_g[j], N_A_LAYERS + j)
               for j in range(DEPTH - N_A_LAYERS)]

    prompt_tab = _attn_prompt_tables(rel_bias)

    def attend_prompt(q, k, v, j):
        return _attn_prompt(q, k, v, prompt_tab, *scalars[j])

    y_p, re_p, im_p, k_p, v_p = _trunk(x_prompt.reshape(BATCH * SEQ, D_MODEL), _Mods(ada, "p"),
                                       _Mods(ada_kv, "p"), None, None, attend_prompt, w)

    sample_tab = _attn_sample_tables(rel_bias)

    def attend_sample(q, k, v, j):
        pad = lambda a: jnp.pad(_to_batch_major(a).astype(BF16),
                                ((0, 0), (0, SUBLANES - DEC_SEQ), (0, 0)))
        o = _attn_sample(_to_batch_major(q), pad(k), pad(v), cache_k, cache_v, page_table,
                         sample_tab, *scalars[j])
        return _to_time_major(o).astype(BF16)

    y_s, re_s, im_s, k_s, v_s = _trunk(_to_time_major(x_sample), _Mods(ada, "s"), _Mods(ada_kv, "s"),
                                       state_ssm_re, state_ssm_im, attend_sample, w)

    return (y_p.reshape(BATCH, SEQ, D_MODEL), _to_batch_major(y_s), re_p, im_p,
            jnp.transpose(k_p.reshape(BATCH, N_HEADS, 2, HEAD_DIM, SEQ), (0, 4, 1, 2, 3)),
            v_p.reshape(BATCH, SEQ, N_HEADS, V_DIM),
            re_s, im_s,
            _to_batch_major(k_s).reshape(DEC_BATCH, DEC_SEQ, N_HEADS, 2, HEAD_DIM),
            _to_batch_major(v_s).reshape(DEC_BATCH, DEC_SEQ, N_HEADS, V_DIM))
```

```python
import functools
import math

import jax
import jax.numpy as jnp
from jax import lax
from jax.experimental import pallas as pl
from jax.experimental.pallas import tpu as pltpu

F32 = jnp.float32
BF16 = jnp.bfloat16

D_MODEL = 1024
BATCH = 4
SEQ = 4096
DEPTH = 4
DEC_BATCH = 128
DEC_SEQ = 4
PAST_LEN = 2048
PAGE_SIZE = 128
N_PAGES = PAST_LEN // PAGE_SIZE
N_A_LAYERS = DEPTH // 2
SSM_GROUP = 16
SSM_GROUPS = D_MODEL // SSM_GROUP
SSM_STATE = 64
N_STATE = SSM_GROUPS * SSM_STATE
N_HEADS = 8
HEAD_DIM = D_MODEL // (2 * N_HEADS)
V_DIM = 2 * HEAD_DIM
D_FF = 4 * D_MODEL
NUM_BUCKETS = 32
MAX_DISTANCE = 128
N_MOD = 6
ALPHA = (2.0 * DEPTH) ** 0.25
LN_EPS = 1e-5
NEG_INF = -1e30

SUBLANES = 8
LANES = 128
VMEM_LIMIT = 48 * 1024 * 1024

MOD_ROWS = DEC_BATCH + 2 * BATCH
CHUNK = LANES
N_CHUNK = D_MODEL // CHUNK
CHUNK_STATE = N_STATE // N_CHUNK
TM_PROMPT = 512
TILE = SUBLANES
T_SSM = 512
TQ = 512
HEADS_PER_STEP = 2
PAGES_PER_STEP = 16
LOG2E = math.log2(math.e)
FF_CHUNK = 1024


def _params(sem, vmem=VMEM_LIMIT):
    return pltpu.CompilerParams(dimension_semantics=sem, vmem_limit_bytes=vmem)


def _mod(ref, n_rows):
    m = ref[...]
    reps = n_rows // m.shape[0]
    return m if m.shape[0] == 1 or reps == 1 else jnp.concatenate([m] * reps, axis=0)


def _layer_norm(z, g, b):
    mu = jnp.mean(z, axis=-1, keepdims=True)
    zc = z - mu
    var = jnp.mean(zc * zc, axis=-1, keepdims=True)
    return zc * lax.rsqrt(var + LN_EPS) * g + b


def _ada_kernel(c_ref, w_ref, b_ref, o_ref):
    c = c_ref[...]
    sc = (c * jax.nn.sigmoid(c)).astype(BF16)
    w = w_ref[...].astype(BF16)
    o_ref[...] = jnp.dot(sc, w, preferred_element_type=F32) + b_ref[...]


def _ada(c_all, w, b):
    n_l, _, width = w.shape
    tn = 2048
    return pl.pallas_call(
        _ada_kernel,
        grid=(n_l, width // tn),
        in_specs=[
            pl.BlockSpec((MOD_ROWS, D_MODEL), lambda l, j: (0, 0)),
            pl.BlockSpec((None, D_MODEL, tn), lambda l, j: (l, 0, j)),
            pl.BlockSpec((None, 1, tn), lambda l, j: (l, 0, j)),
        ],
        out_specs=pl.BlockSpec((None, MOD_ROWS, tn), lambda l, j: (l, 0, j)),
        out_shape=jax.ShapeDtypeStruct((n_l, MOD_ROWS, width), F32),
        compiler_params=_params(("arbitrary", "arbitrary")),
        name="ada_mod",
    )(c_all, w, b.reshape(n_l, 1, width))


class _Mods:
    def __init__(self, arr, mode):
        self.mode = mode
        self.arr = arr if mode == "s" else arr.reshape(arr.shape[0], MOD_ROWS, 1, arr.shape[-1])

    def spec(self, layer, j, batch_of):
        if self.mode == "s":
            return pl.BlockSpec((None, DEC_BATCH, D_MODEL), lambda *g: (layer, 0, j))
        return pl.BlockSpec((None, None, 1, D_MODEL),
                            lambda *g: (layer, DEC_BATCH + batch_of(*g), 0, j))


def _row_tiling(mode):
    if mode == "s":
        return DEC_BATCH * DEC_SEQ, lambda i: 0
    tiles_per_batch = SEQ // TM_PROMPT
    return TM_PROMPT, lambda i: i // tiles_per_batch


def _const_spec(shape):
    zeros = (0,) * len(shape)
    return pl.BlockSpec(shape, lambda *g: zeros, pipeline_mode=pl.Buffered(1))


def _mod_linear_kernel(x_ref, sc_ref, sh_ref, w_ref, *o_refs, blocks):
    n_rows = x_ref.shape[0]
    h = (x_ref[...] * (1.0 + _mod(sc_ref, n_rows)) + _mod(sh_ref, n_rows)).astype(BF16)
    y = jnp.dot(h, w_ref[...], preferred_element_type=F32)
    for (n, transposed), o_ref in zip(blocks, o_refs):
        block = y[:, n * D_MODEL:(n + 1) * D_MODEL].astype(o_ref.dtype)
        o_ref[...] = block.T if transposed else block


def _mod_linear(x, mods, layer, j_scale, j_shift, w, outs):
    n = x.shape[0]
    tm, batch_of = _row_tiling(mods.mode)
    row = pl.BlockSpec((tm, D_MODEL), lambda i: (i, 0))
    tiles_per_batch = SEQ // tm
    col = pl.BlockSpec((None, D_MODEL, tm), lambda i: (i // tiles_per_batch, 0, i % tiles_per_batch))
    return pl.pallas_call(
        functools.partial(_mod_linear_kernel, blocks=tuple((o[0], o[2]) for o in outs)),
        grid=(n // tm,),
        in_specs=[row, mods.spec(layer, j_scale, batch_of), mods.spec(layer, j_shift, batch_of),
                  _const_spec(w.shape)],
        out_specs=[col if o[2] else row for o in outs],
        out_shape=[jax.ShapeDtypeStruct((BATCH, D_MODEL, SEQ) if o[2] else (n, D_MODEL), o[1])
                   for o in outs],
        compiler_params=_params(("arbitrary",)),
        name="mod_linear",
    )(x, mods.arr, mods.arr, w)


def _block_tail_kernel(a_ref, x_ref, gm_ref, sc_ref, sh_ref, gf_ref, wm_ref, w1_ref, w2_ref,
                       g_ref, b_ref, *rest, glu, q_scale):
    o_ref = rest[-2] if q_scale else rest[-1]
    y = jnp.dot(a_ref[...], wm_ref[...], preferred_element_type=F32)
    if glu:
        y = y[:, :D_MODEL] * jax.nn.sigmoid(y[:, D_MODEL:])
    n_rows = x_ref.shape[0]
    x = _layer_norm(ALPHA * x_ref[...] + (1.0 + _mod(gm_ref, n_rows)) * y,
                    g_ref[0:1, :], b_ref[0:1, :])
    h = (x * (1.0 + _mod(sc_ref, n_rows)) + _mod(sh_ref, n_rows)).astype(BF16)
    acc = jnp.zeros(x.shape, F32)
    for c in range(D_FF // FF_CHUNK):
        cols = slice(c * FF_CHUNK, (c + 1) * FF_CHUNK)
        hid = jnp.dot(h, w1_ref[:, cols], preferred_element_type=F32)
        hid = jnp.square(jnp.maximum(hid, 0.0)).astype(BF16)
        acc = acc + jnp.dot(hid, w2_ref[cols, :], preferred_element_type=F32)
    z = ALPHA * x + (1.0 + _mod(gf_ref, n_rows)) * acc
    out = _layer_norm(z, g_ref[1:2, :], b_ref[1:2, :])
    o_ref[...] = out
    if q_scale:
        qsc_ref, qsh_ref, wq_ref, _, q_ref = rest
        hq = (out * (1.0 + _mod(qsc_ref, n_rows)) + _mod(qsh_ref, n_rows)).astype(BF16)
        q = jnp.dot(hq, wq_ref[...], preferred_element_type=F32) * q_scale
        q_ref[...] = q.astype(q_ref.dtype)


def _block_tail(a, x, mods, layer, w_mix, w1, w2, ln_g, ln_b, glu, w_q=None, q_scale=None):
    n = x.shape[0]
    tm, batch_of = _row_tiling(mods.mode)
    row = pl.BlockSpec((tm, D_MODEL), lambda i: (i, 0))
    ln = pl.BlockSpec((None, 2, D_MODEL), lambda i: (layer, 0, 0))
    weight = lambda w, l=layer: pl.BlockSpec((None,) + w.shape[1:],
                                             lambda i: (l % w.shape[0], 0, 0),
                                             pipeline_mode=pl.Buffered(1))
    in_specs = ([row, row] + [mods.spec(layer, j, batch_of) for j in (2, 4, 3, 5)]
                + [weight(w_mix), weight(w1), weight(w2), ln, ln])
    args = [a, x, mods.arr, mods.arr, mods.arr, mods.arr, w_mix, w1, w2, ln_g, ln_b]
    out_specs, out_shape = [row], [jax.ShapeDtypeStruct((n, D_MODEL), F32)]
    if w_q is not None:
        nxt = layer + 1
        in_specs += [mods.spec(nxt, 1, batch_of), mods.spec(nxt, 0, batch_of),
                     weight(w_q, nxt - N_A_LAYERS)]
        args += [mods.arr, mods.arr, w_q]
        out_specs.append(row)
        out_shape.append(jax.ShapeDtypeStruct((n, D_MODEL), BF16))
    return pl.pallas_call(
        functools.partial(_block_tail_kernel, glu=glu, q_scale=q_scale),
        grid=(n // tm,),
        in_specs=in_specs,
        out_specs=out_specs,
        out_shape=out_shape,
        compiler_params=_params(("arbitrary",)),
        name="block_tail",
    )(*args)


def _cmul(ar, ai, br, bi):
    return ar * br - ai * bi, ar * bi + ai * br


def _ssm_prep(lam_re, lam_im, log_dt, b_re, b_im, c_re, c_im):
    n_l = lam_re.shape[0]
    dt = jnp.exp(log_dt)[..., None]
    n_scan = TILE + SUBLANES
    ns = (list(range(TILE + 1)) + [TILE * m for m in range(2, SUBLANES + 1)]
          + list(range(TILE - 1, -1, -1)))
    n = jnp.asarray(ns, F32).reshape(-1, 1, 1, 1)
    mag = jnp.exp(n * (lam_re * dt))
    pr = mag * jnp.cos(n * (lam_im * dt))
    pi = mag * jnp.sin(n * (lam_im * dt))
    ar, ai = pr[1], pi[1]
    den = lam_re * lam_re + lam_im * lam_im
    er = ((ar - 1.0) * lam_re + ai * lam_im) / den
    ei = (ai * lam_re - (ar - 1.0) * lam_im) / den
    bbr = er[..., None] * b_re - ei[..., None] * b_im
    bbi = er[..., None] * b_im + ei[..., None] * b_re
    gpc = CHUNK // SSM_GROUP
    exact = lax.Precision.HIGHEST

    def block_diag(m, rows_per_group, cols_per_group, steps=1):
        y = steps * cols_per_group
        c = jnp.arange(steps * gpc * cols_per_group)
        step, group, inner = c // (gpc * cols_per_group), (c // cols_per_group) % gpc, c % cols_per_group
        select = (jnp.arange(y)[:, None] == (step * cols_per_group + inner)[None, :]).astype(BF16)
        wide = jnp.einsum("...ry,yc->...rc", m.astype(BF16), select, preferred_element_type=BF16)
        row_group = (jnp.arange(m.shape[-2]) // rows_per_group) % gpc
        return jnp.where(row_group[:, None] == group[None, :], wide, jnp.zeros((), BF16))

    def per_chunk(m, order):
        x, _, _, a, b = m.shape
        m = m.reshape(x, n_l, N_CHUNK, gpc, a, b)
        return jnp.transpose(m, (1, 2) + tuple({"x": 0, "g": 3, "a": 4, "b": 5}[o] for o in order))

    swap = lambda m: jnp.swapaxes(m, -1, -2)
    b_in = per_chunk(jnp.concatenate([swap(bbr), swap(bbi)], axis=-1)[None], "gaxb")
    bcat = block_diag(b_in.reshape(n_l, N_CHUNK, CHUNK, 2 * SSM_STATE), SSM_GROUP, SSM_STATE,
                      steps=2)
    c_out = per_chunk(jnp.stack([swap(c_re), -swap(c_im)]), "xgab")
    ccat = block_diag(c_out.reshape(n_l, N_CHUNK, 2 * CHUNK_STATE, SSM_GROUP), SSM_STATE, SSM_GROUP)

    abr, abi = _cmul(pr[:TILE, ..., None], pi[:TILE, ..., None], bbr, bbi)
    dot_p = functools.partial(jnp.einsum, "lghp,jlgpi->jlgih", precision=exact)
    taps = per_chunk(dot_p(c_re, abr) - dot_p(c_im, abi), "xgab")
    kcat = block_diag(taps.reshape(n_l, N_CHUNK, TILE * CHUNK, SSM_GROUP), SSM_GROUP, SSM_GROUP)

    wr, wi = _cmul(pr[n_scan:, ..., None], pi[n_scan:, ..., None], bbr, bbi)
    ends = per_chunk(jnp.concatenate([swap(wr), swap(wi)], axis=-1), "xgab")
    wst = block_diag(ends.reshape(n_l, N_CHUNK, TILE * CHUNK, 2 * SSM_STATE), SSM_GROUP, SSM_STATE,
                     steps=2)

    mr, mi = _cmul(pr[1:TILE + 1, :, :, None, :], pi[1:TILE + 1, :, :, None, :], c_re, c_im)
    outs = jnp.stack([per_chunk(m, "gbxa") for m in (mr, -mi)], axis=2)
    call = block_diag(outs.reshape(n_l, N_CHUNK, 2 * CHUNK_STATE, TILE * SSM_GROUP), SSM_STATE,
                      SSM_GROUP, steps=TILE)

    flat = lambda a: jnp.moveaxis(a, 0, 1).reshape(n_l, -1, N_STATE)
    rows = lax.broadcasted_iota(jnp.int32, (SUBLANES, N_STATE), 0)
    parts = []
    for a in (flat(pr[TILE:n_scan]), flat(pi[TILE:n_scan])):
        masked = lambda m, first: jnp.where(rows >= first, a[:, m - 1:m, :], 0.0)
        parts.append([masked(1, 1), masked(2, 2), masked(4, 4), a, masked(SUBLANES, 0)])
    tab = jnp.stack([p for pair in zip(*parts) for p in pair], axis=1)
    atab = jnp.stack([jnp.broadcast_to(flat(a[1:2]), (n_l, SUBLANES, N_STATE)) for a in (pr, pi)],
                     axis=1)
    return dict(bcat=bcat, ccat=ccat, kcat=kcat, wst=wst, call=call, tab=tab, atab=atab)


def _ssm_prompt_kernel(x_ref, sc_ref, sh_ref, kcat_ref, wst_ref, call_ref, d_ref, tab_ref,
                       g_ref, hre_ref, him_ref, h_ref, buf_ref, y_ref, taps_ref):
    n_tiles = SEQ // TILE
    half = CHUNK_STATE

    @pl.when(pl.program_id(1) == 0)
    def _():
        zero = jnp.zeros((CHUNK, CHUNK), BF16)
        for r in range(TILE):
            for s in range(TILE):
                lag = s - r
                block = kcat_ref[lag * CHUNK:(lag + 1) * CHUNK, :] if lag >= 0 else zero
                taps_ref[r * CHUNK:(r + 1) * CHUNK, s * CHUNK:(s + 1) * CHUNK] = block

    h_ref[...] = x_ref[...] * (1.0 + sc_ref[...]) + sh_ref[...]

    steps = [h_ref[pl.ds(r, n_tiles, stride=TILE), :].astype(BF16) for r in range(TILE)]
    tiles = jnp.concatenate(steps, axis=1)
    buf_ref[...] = jnp.dot(tiles, wst_ref[...], preferred_element_type=F32)
    y_tiles = jnp.dot(tiles, taps_ref[...], preferred_element_type=F32)

    row = lax.broadcasted_iota(jnp.int32, (SUBLANES, half), 0)

    def tile_scan(i, carry):
        cr, ci = carry
        r0 = pl.multiple_of(i * SUBLANES, SUBLANES)
        vr = buf_ref[pl.ds(r0, SUBLANES), 0:half]
        vi = buf_ref[pl.ds(r0, SUBLANES), half:2 * half]
        for lvl, shift in enumerate((1, 2, 4)):
            pr, pi = _cmul(tab_ref[2 * lvl], tab_ref[2 * lvl + 1],
                           pltpu.roll(vr, shift, 0), pltpu.roll(vi, shift, 0))
            vr, vi = vr + pr, vi + pi
        pr, pi = _cmul(tab_ref[6], tab_ref[7], cr, ci)
        buf_ref[pl.ds(r0, SUBLANES), 0:half] = jnp.where(row == 0, cr, pltpu.roll(vr + pr, 1, 0))
        buf_ref[pl.ds(r0, SUBLANES), half:2 * half] = jnp.where(row == 0, ci,
                                                                 pltpu.roll(vi + pi, 1, 0))
        pr, pi = _cmul(tab_ref[8], tab_ref[9], cr, ci)
        last = SUBLANES - 1
        return (jnp.broadcast_to(vr[last:, :], vr.shape) + pr,
                jnp.broadcast_to(vi[last:, :], vi.shape) + pi)

    zero = jnp.zeros((SUBLANES, half), F32)
    cr, ci = lax.fori_loop(0, n_tiles // SUBLANES, tile_scan, (zero, zero))
    hre_ref[...] = cr[0:1, :]
    him_ref[...] = ci[0:1, :]

    y_tiles = y_tiles + jnp.dot(buf_ref[...].astype(BF16), call_ref[...],
                                preferred_element_type=F32)
    for s in range(TILE):
        y_ref[pl.ds(s, n_tiles, stride=TILE), :] = y_tiles[:, s * CHUNK:(s + 1) * CHUNK]

    y = y_ref[...] + d_ref[...] * h_ref[...]
    g_ref[...] = jax.nn.gelu(y).astype(g_ref.dtype)


def _ssm_prompt(x, mods, layer, p, d_skip):
    col = pl.BlockSpec((SEQ, CHUNK), lambda k, b: (b, k))
    fin = pl.BlockSpec((None, 1, CHUNK_STATE), lambda k, b: (b, 0, k))
    mod = lambda j: pl.BlockSpec((None, None, 1, CHUNK),
                                 lambda k, b: (layer, DEC_BATCH + b, 0, j * N_CHUNK + k))
    weight = lambda a: pl.BlockSpec((None, None) + a.shape[2:], lambda k, b: (layer, k, 0, 0))
    g, hre, him = pl.pallas_call(
        _ssm_prompt_kernel,
        grid=(N_CHUNK, BATCH),
        in_specs=[col, mod(1), mod(0), weight(p["kcat"]), weight(p["wst"]), weight(p["call"]),
                  pl.BlockSpec((None, 1, CHUNK), lambda k, b: (layer, 0, k)),
                  pl.BlockSpec((None, p["tab"].shape[1], SUBLANES, CHUNK_STATE),
                               lambda k, b: (layer, 0, 0, k))],
        out_specs=[col, fin, fin],
        out_shape=[jax.ShapeDtypeStruct((BATCH * SEQ, D_MODEL), BF16),
                   jax.ShapeDtypeStruct((BATCH, 1, N_STATE), F32),
                   jax.ShapeDtypeStruct((BATCH, 1, N_STATE), F32)],
        scratch_shapes=[pltpu.VMEM((SEQ, CHUNK), F32),
                        pltpu.VMEM((SEQ // TILE, 2 * CHUNK_STATE), F32),
                        pltpu.VMEM((SEQ, CHUNK), F32),
                        pltpu.VMEM((TILE * CHUNK, TILE * CHUNK), BF16)],
        compiler_params=_params(("arbitrary", "arbitrary")),
        name="ssm_prompt",
    )(x, mods.arr, mods.arr, p["kcat"], p["wst"], p["call"], d_skip[:, None, :], p["tab"])
    shape = (BATCH, SSM_GROUPS, SSM_STATE)
    return g, hre.reshape(shape), him.reshape(shape)


def _ssm_sample_kernel(x_ref, sc_ref, sh_ref, bcat_ref, ccat_ref, d_ref, a_ref, h0r_ref, h0i_ref,
                       g_ref, hre_ref, him_ref):
    tile = lambda m: jnp.concatenate([m] * DEC_SEQ, axis=0)
    h = x_ref[...] * (1.0 + tile(sc_ref[...])) + tile(sh_ref[...])
    bu = jnp.dot(h.astype(BF16), bcat_ref[...], preferred_element_type=F32)
    ar = a_ref[0, 0:1, :]
    ai = a_ref[1, 0:1, :]
    sr = h0r_ref[...].T
    si = h0i_ref[...].T
    states = []
    for t in range(DEC_SEQ):
        rows = slice(t * DEC_BATCH, (t + 1) * DEC_BATCH)
        pr, pi = _cmul(ar, ai, sr, si)
        sr = pr + bu[rows, :CHUNK_STATE]
        si = pi + bu[rows, CHUNK_STATE:]
        states.append(jnp.concatenate([sr, si], axis=1))
    xs = jnp.concatenate(states, axis=0).astype(BF16)
    y = jnp.dot(xs, ccat_ref[...], preferred_element_type=F32) + d_ref[...] * h
    g_ref[...] = jax.nn.gelu(y).astype(g_ref.dtype)
    hre_ref[...] = sr.T
    him_ref[...] = si.T


def _ssm_sample(x, mods, layer, p, d_skip, h0_re, h0_im):
    n = DEC_BATCH * DEC_SEQ
    col = pl.BlockSpec((n, CHUNK), lambda k: (0, k))
    st = pl.BlockSpec((None, CHUNK_STATE, DEC_BATCH), lambda k: (layer, k, 0))
    mod = lambda j: pl.BlockSpec((None, DEC_BATCH, CHUNK), lambda k: (layer, 0, j * N_CHUNK + k))
    state_major = lambda a: jnp.transpose(a, (0, 2, 3, 1)).reshape(-1, N_STATE, DEC_BATCH)
    g, hre, him = pl.pallas_call(
        _ssm_sample_kernel,
        grid=(N_CHUNK,),
        in_specs=[col, mod(1), mod(0),
                  pl.BlockSpec((None, None, CHUNK, 2 * CHUNK_STATE), lambda k: (layer, k, 0, 0)),
                  pl.BlockSpec((None, None, 2 * CHUNK_STATE, CHUNK), lambda k: (layer, k, 0, 0)),
                  pl.BlockSpec((None, 1, CHUNK), lambda k: (layer, 0, k)),
                  pl.BlockSpec((None, 2, SUBLANES, CHUNK_STATE), lambda k: (layer, 0, 0, k)),
                  st, st],
        out_specs=[col] + [pl.BlockSpec((CHUNK_STATE, DEC_BATCH), lambda k: (k, 0))] * 2,
        out_shape=[jax.ShapeDtypeStruct((n, D_MODEL), BF16),
                   jax.ShapeDtypeStruct((N_STATE, DEC_BATCH), F32),
                   jax.ShapeDtypeStruct((N_STATE, DEC_BATCH), F32)],
        compiler_params=_params(("arbitrary",)),
        name="ssm_sample",
    )(x, mods.arr, mods.arr, p["bcat"], p["ccat"], d_skip[:, None, :], p["atab"],
      state_major(h0_re), state_major(h0_im))
    batch_major = lambda a: jnp.transpose(a.reshape(SSM_GROUPS, SSM_STATE, DEC_BATCH), (2, 0, 1))
    return g, batch_major(hre), batch_major(him)


def _rel_bucket(rel):
    n = jnp.maximum(rel, 0)
    max_exact = NUM_BUCKETS // 2
    large = max_exact + (jnp.log(jnp.maximum(n, 1).astype(F32) / max_exact)
                         / math.log(MAX_DISTANCE / max_exact) * (NUM_BUCKETS - max_exact)).astype(jnp.int32)
    large = jnp.minimum(large, NUM_BUCKETS - 1)
    return jnp.where(n < max_exact, n, large)


def _attn_scalars(attn_lam, subln_g, layer):
    lam_init = 0.8 - 0.6 * math.exp(-0.3 * layer)
    lam = (jnp.exp(jnp.sum(attn_lam[0] * attn_lam[1])) - jnp.exp(jnp.sum(attn_lam[2] * attn_lam[3]))
           + lam_init)
    lam_tile = jnp.full((SUBLANES, LANES), lam, F32)
    gain = (subln_g * (1.0 - lam_init)).reshape(1, V_DIM)
    return lam_tile, gain


def _bias_by_distance(rel_bias, dist):
    bv = rel_bias[_rel_bucket(dist.astype(jnp.int32))]
    return ((bv - rel_bias[NUM_BUCKETS - 1]) * LOG2E).T


def _softmax_update(s, m_prev):
    m_new = jnp.maximum(m_prev, jnp.max(s, axis=1, keepdims=True))
    p = jnp.exp2(s - jnp.concatenate([m_new] * (s.shape[1] // LANES), axis=1))
    return m_new, p.astype(BF16), jnp.exp2(m_prev - m_new)


def _with_ones(v):
    return jnp.concatenate([v, jnp.ones((v.shape[0], V_DIM), BF16)], axis=1)


def _acc_update(acc, alpha, p, vx):
    return (jnp.concatenate([alpha, alpha], axis=1) * acc
            + jnp.dot(p, vx, preferred_element_type=F32))


_NT = (((1,), (1,)), ((), ()))


def _attn_prompt_kernel(q_ref, k_ref, v_ref, tab_ref, lam_ref, gain_ref, o_ref,
                        q2_ref, m_ref, acc_ref, s_ref):
    i = pl.program_id(2)
    last = pl.num_programs(2) - 1

    def key_rows(j):
        return pl.ds(pl.multiple_of(j * TQ, TQ), TQ)

    def stack_queries(hh, tile):
        q = q_ref[key_rows(tile), hh * V_DIM:(hh + 1) * V_DIM]
        lane = lax.broadcasted_iota(jnp.int32, q.shape, 1)
        zero = jnp.zeros_like(q)
        q2_ref[hh, 0:TQ, :] = jnp.where(lane < HEAD_DIM, q, zero)
        q2_ref[hh, TQ:2 * TQ, :] = jnp.where(lane >= HEAD_DIM, q, zero)

    def logits(hh, j):
        cols = slice(hh * V_DIM, (hh + 1) * V_DIM)
        return lax.dot_general(q2_ref[hh], k_ref[key_rows(j), cols].astype(BF16), _NT,
                               preferred_element_type=F32)

    @pl.when(i == 0)
    def _():
        stack_queries(0, 0)
        s_ref[...] = logits(0, 0)

    for hh in range(1, HEADS_PER_STEP):
        stack_queries(hh, i)
    m_ref[...] = jnp.full(m_ref.shape, NEG_INF, F32)
    acc_ref[...] = jnp.zeros_like(acc_ref)

    def consume(hh, j, s, table):
        cols = slice(hh * V_DIM, (hh + 1) * V_DIM)
        if table is not None:
            bias = tab_ref[hh, table]
            s = s + jnp.concatenate([bias, bias], axis=0)
        m_new, p, alpha = _softmax_update(s, m_ref[hh])
        acc_ref[hh] = _acc_update(acc_ref[hh], alpha, p,
                                  _with_ones(v_ref[key_rows(j), cols].astype(BF16)))
        m_ref[hh] = m_new

    def step(j, table, same_tile):
        consume(0, j, s_ref[...], table)
        for hh in range(1, HEADS_PER_STEP):
            consume(hh, j, logits(hh, j), table)
        if same_tile:
            s_ref[...] = logits(0, j + 1)
        else:
            stack_queries(0, jnp.minimum(i + 1, last))
            s_ref[...] = logits(0, 0)

    def far(j, c):
        step(j, None, True)
        return c

    lax.fori_loop(0, jnp.maximum(i - 1, 0), far, 0)

    @pl.when(i >= 1)
    def _():
        step(i - 1, 1, True)

    step(i, 0, False)

    for hh in range(HEADS_PER_STEP):
        acc = acc_ref[hh]
        o = acc[:, :V_DIM] / acc[:, V_DIM:]
        o = o[0:TQ, :] - lam_ref[0:1, 0:1] * o[TQ:2 * TQ, :]
        o = o * lax.rsqrt(jnp.mean(o * o, axis=-1, keepdims=True) + LN_EPS) * gain_ref[...]
        o_ref[:, hh * V_DIM:(hh + 1) * V_DIM] = o.astype(o_ref.dtype)


def _attn_prompt_tables(rel_bias):
    m = jnp.arange(2 * TQ)
    behind = (2 * TQ - m) % (2 * TQ)
    diag = jnp.where((m >= 1) & (m <= TQ), NEG_INF, _bias_by_distance(rel_bias, behind))
    prev = _bias_by_distance(rel_bias, jnp.where(m <= TQ, TQ - m, 3 * TQ - m))
    vec = jnp.broadcast_to(jnp.stack([diag, prev], axis=1)[:, :, None, :],
                           (N_HEADS, 2, SUBLANES, 2 * TQ))
    return pl.pallas_call(
        _toeplitz_kernel,
        grid=(N_HEADS,),
        in_specs=[pl.BlockSpec((None, 2, SUBLANES, 2 * TQ), lambda h: (h, 0, 0, 0))],
        out_specs=pl.BlockSpec((None, 2, TQ, TQ), lambda h: (h, 0, 0, 0)),
        out_shape=jax.ShapeDtypeStruct((N_HEADS, 2, TQ, TQ), F32),
        compiler_params=_params(("arbitrary",)),
        name="bias_tiles",
    )(vec)


def _toeplitz_kernel(vec_ref, o_ref):
    for t in range(o_ref.shape[0]):
        rows = jnp.broadcast_to(vec_ref[t, 0:1, :], (TQ, 2 * TQ))
        o_ref[t] = pltpu.roll(rows, 0, 1, stride=1, stride_axis=0)[:, :TQ]


def _attn_prompt(q, k, v, tab, lam_tile, gain):
    n_q = SEQ // TQ
    width = HEADS_PER_STEP * V_DIM
    shape3 = (BATCH, SEQ, D_MODEL)
    qspec = pl.BlockSpec((None, TQ, width), lambda b, h, i: (b, i, h))
    kvspec = pl.BlockSpec((None, SEQ, width), lambda b, h, i: (b, 0, h))
    o = pl.pallas_call(
        _attn_prompt_kernel,
        grid=(BATCH, N_HEADS // HEADS_PER_STEP, n_q),
        in_specs=[kvspec, kvspec, kvspec,
                  pl.BlockSpec((HEADS_PER_STEP, 2, TQ, TQ), lambda b, h, i: (h, 0, 0, 0)),
                  pl.BlockSpec((SUBLANES, LANES), lambda b, h, i: (0, 0)),
                  pl.BlockSpec((1, V_DIM), lambda b, h, i: (0, 0))],
        out_specs=qspec,
        out_shape=jax.ShapeDtypeStruct(shape3, BF16),
        scratch_shapes=[pltpu.VMEM((HEADS_PER_STEP, 2 * TQ, V_DIM), BF16),
                        pltpu.VMEM((HEADS_PER_STEP, 2 * TQ, LANES), F32),
                        pltpu.VMEM((HEADS_PER_STEP, 2 * TQ, 2 * V_DIM), F32),
                        pltpu.VMEM((2 * TQ, TQ), F32)],
        compiler_params=_params(("arbitrary", "arbitrary", "arbitrary")),
        name="attn_prompt",
    )(q.reshape(shape3), k.reshape(shape3), v.reshape(shape3), tab, lam_tile, gain)
    return o.reshape(BATCH * SEQ, D_MODEL)


N_SROWS = N_HEADS * 2 * DEC_SEQ


def _attn_sample_kernel(pt_ref, q_ref, qmask_ref, knew_ref, vnew_ref, bias_ref, bnew_ref,
                        lam_ref, gain_ref, *rest):
    del pt_ref
    k_refs = rest[:PAGES_PER_STEP]
    v_refs = rest[PAGES_PER_STEP:2 * PAGES_PER_STEP]
    o_ref, qx_ref, m_ref, acc_ref = rest[2 * PAGES_PER_STEP:]
    step = pl.program_id(1)
    rows_per_head = 2 * DEC_SEQ

    @pl.when(step == 0)
    def _():
        q = q_ref[...].astype(F32)
        q = jnp.concatenate([q, q], axis=0)
        qx_ref[...] = (jnp.concatenate([q] * N_HEADS, axis=0) * qmask_ref[...]).astype(BF16)
        m_ref[...] = jnp.full(m_ref.shape, NEG_INF, F32)
        acc_ref[...] = jnp.zeros_like(acc_ref)

    def update(s, values_of_head):
        m_new, p, alpha = _softmax_update(s, m_ref[...])
        for h in range(N_HEADS):
            r = slice(h * rows_per_head, (h + 1) * rows_per_head)
            acc_ref[r, :] = _acc_update(acc_ref[r, :], alpha[r, :], p[r, :],
                                        _with_ones(values_of_head(h)))
        m_ref[...] = m_new

    qx = qx_ref[...]
    kt = jnp.concatenate([r[...].astype(BF16) for r in k_refs], axis=1)
    s = jnp.dot(qx, kt, preferred_element_type=F32) + bias_ref[...]
    update(s, lambda h: jnp.concatenate(
        [r[pl.ds(h, PAGE_SIZE, stride=N_HEADS), :].astype(BF16) for r in v_refs], axis=0))

    @pl.when(step == pl.num_programs(1) - 1)
    def _():
        pad = jnp.zeros((LANES - SUBLANES, D_MODEL), BF16)
        kn = jnp.concatenate([knew_ref[...], pad], axis=0)
        vn = jnp.concatenate([vnew_ref[...], pad], axis=0)
        sn = lax.dot_general(qx, kn, _NT, preferred_element_type=F32) + bnew_ref[...]
        update(sn, lambda h: vn[:, h * V_DIM:(h + 1) * V_DIM])
        acc = acc_ref[...]
        o = acc[:, :V_DIM] / acc[:, V_DIM:]
        heads = []
        for h in range(N_HEADS):
            r0 = h * rows_per_head
            oh = o[r0:r0 + DEC_SEQ, :] - lam_ref[0:1, 0:1] * o[r0 + DEC_SEQ:r0 + rows_per_head, :]
            oh = oh * lax.rsqrt(jnp.mean(oh * oh, axis=-1, keepdims=True) + LN_EPS) * gain_ref[...]
            heads.append(oh)
        o_ref[...] = jnp.concatenate(heads, axis=1)


def _attn_sample_tables(rel_bias):
    n_dist = PAST_LEN + DEC_SEQ
    far_first = _bias_by_distance(rel_bias, n_dist - 1 - jnp.arange(n_dist))
    past = jnp.stack([far_first[:, DEC_SEQ - 1 - t:DEC_SEQ - 1 - t + PAST_LEN]
                      for t in range(DEC_SEQ)], axis=1)
    rel = (jnp.arange(DEC_SEQ, dtype=jnp.int32)[:, None]
           - jnp.arange(LANES, dtype=jnp.int32)[None, :])
    near = _bias_by_distance(rel_bias, jnp.arange(DEC_SEQ))
    new = jnp.where(rel >= 0, near[:, jnp.clip(rel, 0, DEC_SEQ - 1)], NEG_INF)
    rows = lambda a: jnp.broadcast_to(a[:, None], (N_HEADS, 2) + a.shape[1:]).reshape(
        N_SROWS, a.shape[-1])
    col = jnp.arange(D_MODEL, dtype=jnp.int32)[None, :]
    row = jnp.arange(N_SROWS, dtype=jnp.int32)[:, None]
    qmask = ((col // V_DIM == row // (2 * DEC_SEQ))
             & ((col % V_DIM) // HEAD_DIM == (row // DEC_SEQ) % 2)).astype(F32)
    return rows(past), rows(new), qmask


def _attn_sample(q, k_new, v_new, cache_k, cache_v, page_table, tables, lam_tile, gain):
    past, new, qmask = tables
    n_phys = cache_k.shape[0]
    ck = jnp.transpose(cache_k, (0, 2, 3, 4, 1)).reshape(n_phys, D_MODEL, PAGE_SIZE)
    cv = cache_v.reshape(n_phys, PAGE_SIZE * N_HEADS, V_DIM)
    n_steps = N_PAGES // PAGES_PER_STEP
    keys_per_step = PAGES_PER_STEP * PAGE_SIZE

    def page_spec(r):
        return pl.BlockSpec((None, D_MODEL, PAGE_SIZE),
                            lambda b, s, pt: (pt[b * N_PAGES + s * PAGES_PER_STEP + r], 0, 0))

    per_seq = lambda rows: pl.BlockSpec((None, rows, D_MODEL), lambda b, s, pt: (b, 0, 0))
    const = lambda shape: pl.BlockSpec(shape, lambda b, s, pt: (0,) * len(shape))
    grid_spec = pltpu.PrefetchScalarGridSpec(
        num_scalar_prefetch=1,
        grid=(DEC_BATCH, n_steps),
        in_specs=[per_seq(DEC_SEQ), const(qmask.shape), per_seq(SUBLANES), per_seq(SUBLANES),
                  pl.BlockSpec((N_SROWS, keys_per_step), lambda b, s, pt: (0, s)),
                  const(new.shape), const(lam_tile.shape), const(gain.shape)]
                 + [page_spec(r) for r in range(PAGES_PER_STEP)] * 2,
        out_specs=per_seq(DEC_SEQ),
        scratch_shapes=[pltpu.VMEM((N_SROWS, D_MODEL), BF16), pltpu.VMEM((N_SROWS, LANES), F32),
                        pltpu.VMEM((N_SROWS, 2 * V_DIM), F32)],
    )
    return pl.pallas_call(
        _attn_sample_kernel,
        grid_spec=grid_spec,
        out_shape=jax.ShapeDtypeStruct((DEC_BATCH, DEC_SEQ, D_MODEL), F32),
        compiler_params=_params(("arbitrary", "arbitrary")),
        name="attn_sample",
    )(page_table.reshape(-1), q, qmask, k_new, v_new, past, new, lam_tile, gain,
      *([ck] * PAGES_PER_STEP), *([cv] * PAGES_PER_STEP))


def _to_batch_major(a):
    return jnp.transpose(a.reshape(DEC_SEQ, DEC_BATCH, D_MODEL), (1, 0, 2))


def _to_time_major(a):
    return jnp.transpose(a, (1, 0, 2)).reshape(DEC_SEQ * DEC_BATCH, D_MODEL)


def _trunk(x, mods, mods_kv, h0_re, h0_im, attend, w):
    ssm_re, ssm_im = [], []
    k = v = q = None
    for layer in range(DEPTH):
        if layer < N_A_LAYERS:
            if mods.mode == "p":
                g, fr, fi = _ssm_prompt(x, mods, layer, w["ssm"], w["ssm_d"])
            else:
                g, fr, fi = _ssm_sample(x, mods, layer, w["ssm"], w["ssm_d"], h0_re, h0_im)
            ssm_re.append(fr)
            ssm_im.append(fi)
            mixed, w_mix = g, w["glu"]
        else:
            mixed, w_mix = attend(q, k, v, layer - N_A_LAYERS), w["wo"]
        feeds_attention = N_A_LAYERS <= layer + 1 < DEPTH
        x, *q_next = _block_tail(mixed, x, mods, layer, w_mix, w["w1"], w["w2"], w["ln_g"],
                                 w["ln_b"], glu=layer < N_A_LAYERS,
                                 w_q=w["wq"] if feeds_attention else None,
                                 q_scale=HEAD_DIM ** -0.5 * LOG2E if feeds_attention else None)
        q = q_next[0] if q_next else None
        if layer == N_A_LAYERS - 1:
            outs = [(0, F32, False), (1, F32, False)] + [(0, F32, True)] * (mods.mode == "p")
            k, v, *k_out = _mod_linear(x, mods_kv, 0, 1, 0, w["wkv"], outs)
    return x, jnp.stack(ssm_re), jnp.stack(ssm_im), (k_out or [k])[0], v


def kernel(x_prompt, x_sample, state_ssm_re, state_ssm_im, cache_k, cache_v, page_table, c_prompt, c_sample, rel_bias, w_ada, b_ada, ln_g, ln_b, ssm_lam_re, ssm_lam_im, ssm_log_dt, ssm_b_re, ssm_b_im, ssm_c_re, ssm_c_im, ssm_d, ssm_w_glu_a, ssm_w_glu_b, w_ada_kv, b_ada_kv, w_kv, attn_w_q, attn_lam, attn_subln_g, attn_w_o, mlp_w1, mlp_w2):
    w = {
        "ssm": _ssm_prep(ssm_lam_re, ssm_lam_im, ssm_log_dt, ssm_b_re, ssm_b_im, ssm_c_re, ssm_c_im),
        "ssm_d": ssm_d,
        "glu": jnp.concatenate([ssm_w_glu_a, ssm_w_glu_b], axis=-1).astype(BF16),
        "wq": attn_w_q.astype(BF16),
        "wo": attn_w_o.astype(BF16),
        "wkv": w_kv.astype(BF16),
        "w1": mlp_w1.astype(BF16),
        "w2": mlp_w2.astype(BF16),
        "ln_g": ln_g,
        "ln_b": ln_b,
    }
    c_all = jnp.concatenate([c_sample, c_prompt, jnp.zeros((BATCH, D_MODEL), F32)], axis=0)
    ada = _ada(c_all, w_ada, b_ada)
    ada_kv = _ada(c_all, w_ada_kv[None], b_ada_kv[None])
    scalars = [_attn_scalars(attn_lam[j], attn_subln_g[j], N_A_LAYERS + j)
               for j in range(DEPTH - N_A_LAYERS)]

    prompt_tab = _attn_prompt_tables(rel_bias)

    def attend_prompt(q, k, v, j):
        return _attn_prompt(q, k, v, prompt_tab, *scalars[j])

    y_p, re_p, im_p, k_p, v_p = _trunk(x_prompt.reshape(BATCH * SEQ, D_MODEL), _Mods(ada, "p"),
                                       _Mods(ada_kv, "p"), None, None, attend_prompt, w)

    sample_tab = _attn_sample_tables(rel_bias)

    def attend_sample(q, k, v, j):
        pad = lambda a: jnp.pad(_to_batch_major(a).astype(BF16),
                                ((0, 0), (0, SUBLANES - DEC_SEQ), (0, 0)))
        o = _attn_sample(_to_batch_major(q), pad(k), pad(v), cache_k, cache_v, page_table,
                         sample_tab, *scalars[j])
        return _to_time_major(o).astype(BF16)

    y_s, re_s, im_s, k_s, v_s = _trunk(_to_time_major(x_sample), _Mods(ada, "s"), _Mods(ada_kv, "s"),
                                       state_ssm_re, state_ssm_im, attend_sample, w)

    return (y_p.reshape(BATCH, SEQ, D_MODEL), _to_batch_major(y_s), re_p, im_p,
            jnp.transpose(k_p.reshape(BATCH, N_HEADS, 2, HEAD_DIM, SEQ), (0, 4, 1, 2, 3)),
            v_p.reshape(BATCH, SEQ, N_HEADS, V_DIM),
            re_s, im_s,
            _to_batch_major(k_s).reshape(DEC_BATCH, DEC_SEQ, N_HEADS, 2, HEAD_DIM),
            _to_batch_major(v_s).reshape(DEC_BATCH, DEC_SEQ, N_HEADS, V_DIM))
```

```python
import functools
import math

import jax
import jax.numpy as jnp
from jax import lax
from jax.experimental import pallas as pl
from jax.experimental.pallas import tpu as pltpu

F32 = jnp.float32
BF16 = jnp.bfloat16

D_MODEL = 1024
BATCH = 4
SEQ = 4096
DEPTH = 4
DEC_BATCH = 128
DEC_SEQ = 4
PAST_LEN = 2048
PAGE_SIZE = 128
N_PAGES = PAST_LEN // PAGE_SIZE
N_A_LAYERS = DEPTH // 2
SSM_GROUP = 16
SSM_GROUPS = D_MODEL // SSM_GROUP
SSM_STATE = 64
N_STATE = SSM_GROUPS * SSM_STATE
N_HEADS = 8
HEAD_DIM = D_MODEL // (2 * N_HEADS)
V_DIM = 2 * HEAD_DIM
D_FF = 4 * D_MODEL
NUM_BUCKETS = 32
MAX_DISTANCE = 128
N_MOD = 6
ALPHA = (2.0 * DEPTH) ** 0.25
LN_EPS = 1e-5
NEG_INF = -1e30

SUBLANES = 8
LANES = 128
VMEM_LIMIT = 48 * 1024 * 1024

MOD_ROWS = DEC_BATCH + 2 * BATCH
CHUNK = LANES
N_CHUNK = D_MODEL // CHUNK
CHUNK_STATE = N_STATE // N_CHUNK
TM_PROMPT = 512
TILE = SUBLANES
T_SSM = 512
TQ = 512
HEADS_PER_STEP = 2
PAGES_PER_STEP = 16
LOG2E = math.log2(math.e)
FF_CHUNK = 1024


def _params(sem, vmem=VMEM_LIMIT):
    return pltpu.CompilerParams(dimension_semantics=sem, vmem_limit_bytes=vmem)


def _mod(ref, n_rows):
    m = ref[...]
    reps = n_rows // m.shape[0]
    return m if m.shape[0] == 1 or reps == 1 else jnp.concatenate([m] * reps, axis=0)


def _layer_norm(z, g, b):
    mu = jnp.mean(z, axis=-1, keepdims=True)
    zc = z - mu
    var = jnp.mean(zc * zc, axis=-1, keepdims=True)
    return zc * lax.rsqrt(var + LN_EPS) * g + b


def _ada_kernel(c_ref, w_ref, b_ref, o_ref):
    c = c_ref[...]
    sc = (c * jax.nn.sigmoid(c)).astype(BF16)
    w = w_ref[...].astype(BF16)
    o_ref[...] = jnp.dot(sc, w, preferred_element_type=F32) + b_ref[...]


def _ada(c_all, w, b):
    n_l, _, width = w.shape
    tn = 2048
    return pl.pallas_call(
        _ada_kernel,
        grid=(n_l, width // tn),
        in_specs=[
            pl.BlockSpec((MOD_ROWS, D_MODEL), lambda l, j: (0, 0)),
            pl.BlockSpec((None, D_MODEL, tn), lambda l, j: (l, 0, j)),
            pl.BlockSpec((None, 1, tn), lambda l, j: (l, 0, j)),
        ],
        out_specs=pl.BlockSpec((None, MOD_ROWS, tn), lambda l, j: (l, 0, j)),
        out_shape=jax.ShapeDtypeStruct((n_l, MOD_ROWS, width), F32),
        compiler_params=_params(("arbitrary", "arbitrary")),
        name="ada_mod",
    )(c_all, w, b.reshape(n_l, 1, width))


class _Mods:
    def __init__(self, arr, mode):
        self.mode = mode
        self.arr = arr if mode == "s" else arr.reshape(arr.shape[0], MOD_ROWS, 1, arr.shape[-1])

    def spec(self, layer, j, batch_of):
        if self.mode == "s":
            return pl.BlockSpec((None, DEC_BATCH, D_MODEL), lambda *g: (layer, 0, j))
        return pl.BlockSpec((None, None, 1, D_MODEL),
                            lambda *g: (layer, DEC_BATCH + batch_of(*g), 0, j))


def _row_tiling(mode):
    if mode == "s":
        return DEC_BATCH * DEC_SEQ, lambda i: 0
    tiles_per_batch = SEQ // TM_PROMPT
    return TM_PROMPT, lambda i: i // tiles_per_batch


def _const_spec(shape):
    zeros = (0,) * len(shape)
    return pl.BlockSpec(shape, lambda *g: zeros, pipeline_mode=pl.Buffered(1))


def _mod_linear_kernel(x_ref, sc_ref, sh_ref, w_ref, *o_refs, blocks):
    n_rows = x_ref.shape[0]
    h = (x_ref[...] * (1.0 + _mod(sc_ref, n_rows)) + _mod(sh_ref, n_rows)).astype(BF16)
    y = jnp.dot(h, w_ref[...], preferred_element_type=F32)
    for (n, transposed), o_ref in zip(blocks, o_refs):
        block = y[:, n * D_MODEL:(n + 1) * D_MODEL].astype(o_ref.dtype)
        o_ref[...] = block.T if transposed else block


def _mod_linear(x, mods, layer, j_scale, j_shift, w, outs):
    n = x.shape[0]
    tm, batch_of = _row_tiling(mods.mode)
    row = pl.BlockSpec((tm, D_MODEL), lambda i: (i, 0))
    tiles_per_batch = SEQ // tm
    col = pl.BlockSpec((None, D_MODEL, tm), lambda i: (i // tiles_per_batch, 0, i % tiles_per_batch))
    return pl.pallas_call(
        functools.partial(_mod_linear_kernel, blocks=tuple((o[0], o[2]) for o in outs)),
        grid=(n // tm,),
        in_specs=[row, mods.spec(layer, j_scale, batch_of), mods.spec(layer, j_shift, batch_of),
                  _const_spec(w.shape)],
        out_specs=[col if o[2] else row for o in outs],
        out_shape=[jax.ShapeDtypeStruct((BATCH, D_MODEL, SEQ) if o[2] else (n, D_MODEL), o[1])
                   for o in outs],
        compiler_params=_params(("arbitrary",)),
        name="mod_linear",
    )(x, mods.arr, mods.arr, w)


def _block_tail_kernel(a_ref, x_ref, gm_ref, sc_ref, sh_ref, gf_ref, wm_ref, w1_ref, w2_ref,
                       g_ref, b_ref, *rest, glu, q_scale):
    o_ref = rest[-2] if q_scale else rest[-1]
    y = jnp.dot(a_ref[...], wm_ref[...], preferred_element_type=F32)
    if glu:
        y = y[:, :D_MODEL] * jax.nn.sigmoid(y[:, D_MODEL:])
    n_rows = x_ref.shape[0]
    x = _layer_norm(ALPHA * x_ref[...] + (1.0 + _mod(gm_ref, n_rows)) * y,
                    g_ref[0:1, :], b_ref[0:1, :])
    h = (x * (1.0 + _mod(sc_ref, n_rows)) + _mod(sh_ref, n_rows)).astype(BF16)
    acc = jnp.zeros(x.shape, F32)
    for c in range(D_FF // FF_CHUNK):
        cols = slice(c * FF_CHUNK, (c + 1) * FF_CHUNK)
        hid = jnp.dot(h, w1_ref[:, cols], preferred_element_type=F32)
        hid = jnp.square(jnp.maximum(hid, 0.0)).astype(BF16)
        acc = acc + jnp.dot(hid, w2_ref[cols, :], preferred_element_type=F32)
    z = ALPHA * x + (1.0 + _mod(gf_ref, n_rows)) * acc
    out = _layer_norm(z, g_ref[1:2, :], b_ref[1:2, :])
    o_ref[...] = out
    if q_scale:
        qsc_ref, qsh_ref, wq_ref, _, q_ref = rest
        hq = (out * (1.0 + _mod(qsc_ref, n_rows)) + _mod(qsh_ref, n_rows)).astype(BF16)
        q = jnp.dot(hq, wq_ref[...], preferred_element_type=F32) * q_scale
        q_ref[...] = q.astype(q_ref.dtype)


def _block_tail(a, x, mods, layer, w_mix, w1, w2, ln_g, ln_b, glu, w_q=None, q_scale=None):
    n = x.shape[0]
    tm, batch_of = _row_tiling(mods.mode)
    row = pl.BlockSpec((tm, D_MODEL), lambda i: (i, 0))
    ln = pl.BlockSpec((None, 2, D_MODEL), lambda i: (layer, 0, 0))
    weight = lambda w, l=layer: pl.BlockSpec((None,) + w.shape[1:],
                                             lambda i: (l % w.shape[0], 0, 0),
                                             pipeline_mode=pl.Buffered(1))
    in_specs = ([row, row] + [mods.spec(layer, j, batch_of) for j in (2, 4, 3, 5)]
                + [weight(w_mix), weight(w1), weight(w2), ln, ln])
    args = [a, x, mods.arr, mods.arr, mods.arr, mods.arr, w_mix, w1, w2, ln_g, ln_b]
    out_specs, out_shape = [row], [jax.ShapeDtypeStruct((n, D_MODEL), F32)]
    if w_q is not None:
        nxt = layer + 1
        in_specs += [mods.spec(nxt, 1, batch_of), mods.spec(nxt, 0, batch_of),
                     weight(w_q, nxt - N_A_LAYERS)]
        args += [mods.arr, mods.arr, w_q]
        out_specs.append(row)
        out_shape.append(jax.ShapeDtypeStruct((n, D_MODEL), BF16))
    return pl.pallas_call(
        functools.partial(_block_tail_kernel, glu=glu, q_scale=q_scale),
        grid=(n // tm,),
        in_specs=in_specs,
        out_specs=out_specs,
        out_shape=out_shape,
        compiler_params=_params(("arbitrary",)),
        name="block_tail",
    )(*args)


def _cmul(ar, ai, br, bi):
    return ar * br - ai * bi, ar * bi + ai * br


def _ssm_prep(lam_re, lam_im, log_dt, b_re, b_im, c_re, c_im):
    n_l = lam_re.shape[0]
    dt = jnp.exp(log_dt)[..., None]
    n_scan = TILE + SUBLANES
    ns = (list(range(TILE + 1)) + [TILE * m for m in range(2, SUBLANES + 1)]
          + list(range(TILE - 1, -1, -1)))
    n = jnp.asarray(ns, F32).reshape(-1, 1, 1, 1)
    mag = jnp.exp(n * (lam_re * dt))
    pr = mag * jnp.cos(n * (lam_im * dt))
    pi = mag * jnp.sin(n * (lam_im * dt))
    ar, ai = pr[1], pi[1]
    den = lam_re * lam_re + lam_im * lam_im
    er = ((ar - 1.0) * lam_re + ai * lam_im) / den
    ei = (ai * lam_re - (ar - 1.0) * lam_im) / den
    bbr = er[..., None] * b_re - ei[..., None] * b_im
    bbi = er[..., None] * b_im + ei[..., None] * b_re
    gpc = CHUNK // SSM_GROUP
    exact = lax.Precision.HIGHEST

    def block_diag(m, rows_per_group, cols_per_group, steps=1):
        y = steps * cols_per_group
        c = jnp.arange(steps * gpc * cols_per_group)
        step, group, inner = c // (gpc * cols_per_group), (c // cols_per_group) % gpc, c % cols_per_group
        select = (jnp.arange(y)[:, None] == (step * cols_per_group + inner)[None, :]).astype(BF16)
        wide = jnp.einsum("...ry,yc->...rc", m.astype(BF16), select, preferred_element_type=BF16)
        row_group = (jnp.arange(m.shape[-2]) // rows_per_group) % gpc
        return jnp.where(row_group[:, None] == group[None, :], wide, jnp.zeros((), BF16))

    def per_chunk(m, order):
        x, _, _, a, b = m.shape
        m = m.reshape(x, n_l, N_CHUNK, gpc, a, b)
        return jnp.transpose(m, (1, 2) + tuple({"x": 0, "g": 3, "a": 4, "b": 5}[o] for o in order))

    swap = lambda m: jnp.swapaxes(m, -1, -2)
    b_in = per_chunk(jnp.concatenate([swap(bbr), swap(bbi)], axis=-1)[None], "gaxb")
    bcat = block_diag(b_in.reshape(n_l, N_CHUNK, CHUNK, 2 * SSM_STATE), SSM_GROUP, SSM_STATE,
                      steps=2)
    c_out = per_chunk(jnp.stack([swap(c_re), -swap(c_im)]), "xgab")
    ccat = block_diag(c_out.reshape(n_l, N_CHUNK, 2 * CHUNK_STATE, SSM_GROUP), SSM_STATE, SSM_GROUP)

    abr, abi = _cmul(pr[:TILE, ..., None], pi[:TILE, ..., None], bbr, bbi)
    dot_p = functools.partial(jnp.einsum, "lghp,jlgpi->jlgih", precision=exact)
    taps = per_chunk(dot_p(c_re, abr) - dot_p(c_im, abi), "xgab")
    kcat = block_diag(taps.reshape(n_l, N_CHUNK, TILE * CHUNK, SSM_GROUP), SSM_GROUP, SSM_GROUP)

    wr, wi = _cmul(pr[n_scan:, ..., None], pi[n_scan:, ..., None], bbr, bbi)
    ends = per_chunk(jnp.concatenate([swap(wr), swap(wi)], axis=-1), "xgab")
    wst = block_diag(ends.reshape(n_l, N_CHUNK, TILE * CHUNK, 2 * SSM_STATE), SSM_GROUP, SSM_STATE,
                     steps=2)

    mr, mi = _cmul(pr[1:TILE + 1, :, :, None, :], pi[1:TILE + 1, :, :, None, :], c_re, c_im)
    outs = jnp.stack([per_chunk(m, "gbxa") for m in (mr, -mi)], axis=2)
    call = block_diag(outs.reshape(n_l, N_CHUNK, 2 * CHUNK_STATE, TILE * SSM_GROUP), SSM_STATE,
                      SSM_GROUP, steps=TILE)

    flat = lambda a: jnp.moveaxis(a, 0, 1).reshape(n_l, -1, N_STATE)
    rows = lax.broadcasted_iota(jnp.int32, (SUBLANES, N_STATE), 0)
    parts = []
    for a in (flat(pr[TILE:n_scan]), flat(pi[TILE:n_scan])):
        masked = lambda m, first: jnp.where(rows >= first, a[:, m - 1:m, :], 0.0)
        parts.append([masked(1, 1), masked(2, 2), masked(4, 4), a])
    tab = jnp.stack([p for pair in zip(*parts) for p in pair], axis=1)
    atab = jnp.stack([jnp.broadcast_to(flat(a[1:2]), (n_l, SUBLANES, N_STATE)) for a in (pr, pi)],
                     axis=1)
    return dict(bcat=bcat, ccat=ccat, kcat=kcat, wst=wst, call=call, tab=tab, atab=atab)


def _ssm_prompt_kernel(x_ref, sc_ref, sh_ref, kcat_ref, wst_ref, call_ref, d_ref, tab_ref,
                       g_ref, hre_ref, him_ref, h_ref, buf_ref, y_ref, taps_ref):
    n_tiles = SEQ // TILE
    half = CHUNK_STATE

    @pl.when(pl.program_id(1) == 0)
    def _():
        zero = jnp.zeros((CHUNK, CHUNK), BF16)
        for r in range(TILE):
            for s in range(TILE):
                lag = s - r
                block = kcat_ref[lag * CHUNK:(lag + 1) * CHUNK, :] if lag >= 0 else zero
                taps_ref[r * CHUNK:(r + 1) * CHUNK, s * CHUNK:(s + 1) * CHUNK] = block

    h_ref[...] = x_ref[...] * (1.0 + sc_ref[...]) + sh_ref[...]

    steps = [h_ref[pl.ds(r, n_tiles, stride=TILE), :].astype(BF16) for r in range(TILE)]
    tiles = jnp.concatenate(steps, axis=1)
    buf_ref[...] = jnp.dot(tiles, wst_ref[...], preferred_element_type=F32)
    y_tiles = jnp.dot(tiles, taps_ref[...], preferred_element_type=F32)

    row = lax.broadcasted_iota(jnp.int32, (SUBLANES, half), 0)

    def tile_scan(i, carry):
        cr, ci = carry
        r0 = pl.multiple_of(i * SUBLANES, SUBLANES)
        vr = buf_ref[pl.ds(r0, SUBLANES), 0:half]
        vi = buf_ref[pl.ds(r0, SUBLANES), half:2 * half]
        for lvl, shift in enumerate((1, 2, 4)):
            pr, pi = _cmul(tab_ref[2 * lvl], tab_ref[2 * lvl + 1],
                           pltpu.roll(vr, shift, 0), pltpu.roll(vi, shift, 0))
            vr, vi = vr + pr, vi + pi
        pr, pi = _cmul(tab_ref[6], tab_ref[7], cr, ci)
        xr, xi = vr + pr, vi + pi
        buf_ref[pl.ds(r0, SUBLANES), 0:half] = jnp.where(row == 0, cr, pltpu.roll(xr, 1, 0))
        buf_ref[pl.ds(r0, SUBLANES), half:2 * half] = jnp.where(row == 0, ci, pltpu.roll(xi, 1, 0))
        last = SUBLANES - 1
        return (jnp.broadcast_to(xr[last:, :], xr.shape), jnp.broadcast_to(xi[last:, :], xi.shape))

    zero = jnp.zeros((SUBLANES, half), F32)
    cr, ci = lax.fori_loop(0, n_tiles // SUBLANES, tile_scan, (zero, zero))
    hre_ref[...] = cr[0:1, :]
    him_ref[...] = ci[0:1, :]

    y_tiles = y_tiles + jnp.dot(buf_ref[...].astype(BF16), call_ref[...],
                                preferred_element_type=F32)
    for s in range(TILE):
        y_ref[pl.ds(s, n_tiles, stride=TILE), :] = y_tiles[:, s * CHUNK:(s + 1) * CHUNK]

    y = y_ref[...] + d_ref[...] * h_ref[...]
    g_ref[...] = jax.nn.gelu(y).astype(g_ref.dtype)


def _ssm_prompt(x, mods, layer, p, d_skip):
    col = pl.BlockSpec((SEQ, CHUNK), lambda k, b: (b, k))
    fin = pl.BlockSpec((None, 1, CHUNK_STATE), lambda k, b: (b, 0, k))
    mod = lambda j: pl.BlockSpec((None, None, 1, CHUNK),
                                 lambda k, b: (layer, DEC_BATCH + b, 0, j * N_CHUNK + k))
    weight = lambda a: pl.BlockSpec((None, None) + a.shape[2:], lambda k, b: (layer, k, 0, 0))
    g, hre, him = pl.pallas_call(
        _ssm_prompt_kernel,
        grid=(N_CHUNK, BATCH),
        in_specs=[col, mod(1), mod(0), weight(p["kcat"]), weight(p["wst"]), weight(p["call"]),
                  pl.BlockSpec((None, 1, CHUNK), lambda k, b: (layer, 0, k)),
                  pl.BlockSpec((None, p["tab"].shape[1], SUBLANES, CHUNK_STATE),
                               lambda k, b: (layer, 0, 0, k))],
        out_specs=[col, fin, fin],
        out_shape=[jax.ShapeDtypeStruct((BATCH * SEQ, D_MODEL), BF16),
                   jax.ShapeDtypeStruct((BATCH, 1, N_STATE), F32),
                   jax.ShapeDtypeStruct((BATCH, 1, N_STATE), F32)],
        scratch_shapes=[pltpu.VMEM((SEQ, CHUNK), F32),
                        pltpu.VMEM((SEQ // TILE, 2 * CHUNK_STATE), F32),
                        pltpu.VMEM((SEQ, CHUNK), F32),
                        pltpu.VMEM((TILE * CHUNK, TILE * CHUNK), BF16)],
        compiler_params=_params(("arbitrary", "arbitrary")),
        name="ssm_prompt",
    )(x, mods.arr, mods.arr, p["kcat"], p["wst"], p["call"], d_skip[:, None, :], p["tab"])
    shape = (BATCH, SSM_GROUPS, SSM_STATE)
    return g, hre.reshape(shape), him.reshape(shape)


def _ssm_sample_kernel(x_ref, sc_ref, sh_ref, bcat_ref, ccat_ref, d_ref, a_ref, h0r_ref, h0i_ref,
                       g_ref, hre_ref, him_ref):
    tile = lambda m: jnp.concatenate([m] * DEC_SEQ, axis=0)
    h = x_ref[...] * (1.0 + tile(sc_ref[...])) + tile(sh_ref[...])
    bu = jnp.dot(h.astype(BF16), bcat_ref[...], preferred_element_type=F32)
    ar = a_ref[0, 0:1, :]
    ai = a_ref[1, 0:1, :]
    sr = h0r_ref[...].T
    si = h0i_ref[...].T
    states = []
    for t in range(DEC_SEQ):
        rows = slice(t * DEC_BATCH, (t + 1) * DEC_BATCH)
        pr, pi = _cmul(ar, ai, sr, si)
        sr = pr + bu[rows, :CHUNK_STATE]
        si = pi + bu[rows, CHUNK_STATE:]
        states.append(jnp.concatenate([sr, si], axis=1))
    xs = jnp.concatenate(states, axis=0).astype(BF16)
    y = jnp.dot(xs, ccat_ref[...], preferred_element_type=F32) + d_ref[...] * h
    g_ref[...] = jax.nn.gelu(y).astype(g_ref.dtype)
    hre_ref[...] = sr.T
    him_ref[...] = si.T


def _ssm_sample(x, mods, layer, p, d_skip, h0_re, h0_im):
    n = DEC_BATCH * DEC_SEQ
    col = pl.BlockSpec((n, CHUNK), lambda k: (0, k))
    st = pl.BlockSpec((None, CHUNK_STATE, DEC_BATCH), lambda k: (layer, k, 0))
    mod = lambda j: pl.BlockSpec((None, DEC_BATCH, CHUNK), lambda k: (layer, 0, j * N_CHUNK + k))
    state_major = lambda a: jnp.transpose(a, (0, 2, 3, 1)).reshape(-1, N_STATE, DEC_BATCH)
    g, hre, him = pl.pallas_call(
        _ssm_sample_kernel,
        grid=(N_CHUNK,),
        in_specs=[col, mod(1), mod(0),
                  pl.BlockSpec((None, None, CHUNK, 2 * CHUNK_STATE), lambda k: (layer, k, 0, 0)),
                  pl.BlockSpec((None, None, 2 * CHUNK_STATE, CHUNK), lambda k: (layer, k, 0, 0)),
                  pl.BlockSpec((None, 1, CHUNK), lambda k: (layer, 0, k)),
                  pl.BlockSpec((None, 2, SUBLANES, CHUNK_STATE), lambda k: (layer, 0, 0, k)),
                  st, st],
        out_specs=[col] + [pl.BlockSpec((CHUNK_STATE, DEC_BATCH), lambda k: (k, 0))] * 2,
        out_shape=[jax.ShapeDtypeStruct((n, D_MODEL), BF16),
                   jax.ShapeDtypeStruct((N_STATE, DEC_BATCH), F32),
                   jax.ShapeDtypeStruct((N_STATE, DEC_BATCH), F32)],
        compiler_params=_params(("arbitrary",)),
        name="ssm_sample",
    )(x, mods.arr, mods.arr, p["bcat"], p["ccat"], d_skip[:, None, :], p["atab"],
      state_major(h0_re), state_major(h0_im))
    batch_major = lambda a: jnp.transpose(a.reshape(SSM_GROUPS, SSM_STATE, DEC_BATCH), (2, 0, 1))
    return g, batch_major(hre), batch_major(him)


def _rel_bucket(rel):
    n = jnp.maximum(rel, 0)
    max_exact = NUM_BUCKETS // 2
    large = max_exact + (jnp.log(jnp.maximum(n, 1).astype(F32) / max_exact)
                         / math.log(MAX_DISTANCE / max_exact) * (NUM_BUCKETS - max_exact)).astype(jnp.int32)
    large = jnp.minimum(large, NUM_BUCKETS - 1)
    return jnp.where(n < max_exact, n, large)


def _attn_scalars(attn_lam, subln_g, layer):
    lam_init = 0.8 - 0.6 * math.exp(-0.3 * layer)
    lam = (jnp.exp(jnp.sum(attn_lam[0] * attn_lam[1])) - jnp.exp(jnp.sum(attn_lam[2] * attn_lam[3]))
           + lam_init)
    lam_tile = jnp.full((SUBLANES, LANES), lam, F32)
    gain = (subln_g * (1.0 - lam_init)).reshape(1, V_DIM)
    return lam_tile, gain


def _bias_by_distance(rel_bias, dist):
    bv = rel_bias[_rel_bucket(dist.astype(jnp.int32))]
    return ((bv - rel_bias[NUM_BUCKETS - 1]) * LOG2E).T


def _softmax_update(s, m_prev):
    m_new = jnp.maximum(m_prev, jnp.max(s, axis=1, keepdims=True))
    p = jnp.exp2(s - jnp.concatenate([m_new] * (s.shape[1] // LANES), axis=1))
    return m_new, p.astype(BF16), jnp.exp2(m_prev - m_new)


def _with_ones(v):
    return jnp.concatenate([v, jnp.ones((v.shape[0], V_DIM), BF16)], axis=1)


def _acc_update(acc, alpha, p, vx):
    return (jnp.concatenate([alpha, alpha], axis=1) * acc
            + jnp.dot(p, vx, preferred_element_type=F32))


_NT = (((1,), (1,)), ((), ()))


def _attn_prompt_kernel(q_ref, k_ref, v_ref, tab_ref, lam_ref, gain_ref, o_ref,
                        q2_ref, m_ref, acc_ref, s_ref):
    i = pl.program_id(2)
    last = pl.num_programs(2) - 1

    def key_rows(j):
        return pl.ds(pl.multiple_of(j * TQ, TQ), TQ)

    def stack_queries(hh, tile):
        q = q_ref[key_rows(tile), hh * V_DIM:(hh + 1) * V_DIM]
        lane = lax.broadcasted_iota(jnp.int32, q.shape, 1)
        zero = jnp.zeros_like(q)
        q2_ref[hh, 0:TQ, :] = jnp.where(lane < HEAD_DIM, q, zero)
        q2_ref[hh, TQ:2 * TQ, :] = jnp.where(lane >= HEAD_DIM, q, zero)

    def logits(hh, j):
        cols = slice(hh * V_DIM, (hh + 1) * V_DIM)
        return lax.dot_general(q2_ref[hh], k_ref[key_rows(j), cols].astype(BF16), _NT,
                               preferred_element_type=F32)

    @pl.when(i == 0)
    def _():
        stack_queries(0, 0)
        s_ref[...] = logits(0, 0)

    for hh in range(1, HEADS_PER_STEP):
        stack_queries(hh, i)
    m_ref[...] = jnp.full(m_ref.shape, NEG_INF, F32)
    acc_ref[...] = jnp.zeros_like(acc_ref)

    def consume(hh, j, s, table):
        cols = slice(hh * V_DIM, (hh + 1) * V_DIM)
        if table is not None:
            bias = tab_ref[hh, table]
            s = s + jnp.concatenate([bias, bias], axis=0)
        m_new, p, alpha = _softmax_update(s, m_ref[hh])
        acc_ref[hh] = _acc_update(acc_ref[hh], alpha, p,
                                  _with_ones(v_ref[key_rows(j), cols].astype(BF16)))
        m_ref[hh] = m_new

    def step(j, table, same_tile):
        consume(0, j, s_ref[...], table)
        for hh in range(1, HEADS_PER_STEP):
            consume(hh, j, logits(hh, j), table)
        if same_tile:
            s_ref[...] = logits(0, j + 1)
        else:
            stack_queries(0, jnp.minimum(i + 1, last))
            s_ref[...] = logits(0, 0)

    def far(j, c):
        step(j, None, True)
        return c

    lax.fori_loop(0, jnp.maximum(i - 1, 0), far, 0)

    @pl.when(i >= 1)
    def _():
        step(i - 1, 1, True)

    step(i, 0, False)

    for hh in range(HEADS_PER_STEP):
        acc = acc_ref[hh]
        o = acc[:, :V_DIM] / acc[:, V_DIM:]
        o = o[0:TQ, :] - lam_ref[0:1, 0:1] * o[TQ:2 * TQ, :]
        o = o * lax.rsqrt(jnp.mean(o * o, axis=-1, keepdims=True) + LN_EPS) * gain_ref[...]
        o_ref[:, hh * V_DIM:(hh + 1) * V_DIM] = o.astype(o_ref.dtype)


def _attn_prompt_tables(rel_bias):
    m = jnp.arange(2 * TQ)
    behind = (2 * TQ - m) % (2 * TQ)
    diag = jnp.where((m >= 1) & (m <= TQ), NEG_INF, _bias_by_distance(rel_bias, behind))
    prev = _bias_by_distance(rel_bias, jnp.where(m <= TQ, TQ - m, 3 * TQ - m))
    vec = jnp.broadcast_to(jnp.stack([diag, prev], axis=1)[:, :, None, :],
                           (N_HEADS, 2, SUBLANES, 2 * TQ))
    return pl.pallas_call(
        _toeplitz_kernel,
        grid=(N_HEADS,),
        in_specs=[pl.BlockSpec((None, 2, SUBLANES, 2 * TQ), lambda h: (h, 0, 0, 0))],
        out_specs=pl.BlockSpec((None, 2, TQ, TQ), lambda h: (h, 0, 0, 0)),
        out_shape=jax.ShapeDtypeStruct((N_HEADS, 2, TQ, TQ), F32),
        compiler_params=_params(("arbitrary",)),
        name="bias_tiles",
    )(vec)


def _toeplitz_kernel(vec_ref, o_ref):
    for t in range(o_ref.shape[0]):
        rows = jnp.broadcast_to(vec_ref[t, 0:1, :], (TQ, 2 * TQ))
        o_ref[t] = pltpu.roll(rows, 0, 1, stride=1, stride_axis=0)[:, :TQ]


def _attn_prompt(q, k, v, tab, lam_tile, gain):
    n_q = SEQ // TQ
    width = HEADS_PER_STEP * V_DIM
    shape3 = (BATCH, SEQ, D_MODEL)
    qspec = pl.BlockSpec((None, TQ, width), lambda b, h, i: (b, i, h))
    kvspec = pl.BlockSpec((None, SEQ, width), lambda b, h, i: (b, 0, h))
    o = pl.pallas_call(
        _attn_prompt_kernel,
        grid=(BATCH, N_HEADS // HEADS_PER_STEP, n_q),
        in_specs=[kvspec, kvspec, kvspec,
                  pl.BlockSpec((HEADS_PER_STEP, 2, TQ, TQ), lambda b, h, i: (h, 0, 0, 0)),
                  pl.BlockSpec((SUBLANES, LANES), lambda b, h, i: (0, 0)),
                  pl.BlockSpec((1, V_DIM), lambda b, h, i: (0, 0))],
        out_specs=qspec,
        out_shape=jax.ShapeDtypeStruct(shape3, BF16),
        scratch_shapes=[pltpu.VMEM((HEADS_PER_STEP, 2 * TQ, V_DIM), BF16),
                        pltpu.VMEM((HEADS_PER_STEP, 2 * TQ, LANES), F32),
                        pltpu.VMEM((HEADS_PER_STEP, 2 * TQ, 2 * V_DIM), F32),
                        pltpu.VMEM((2 * TQ, TQ), F32)],
        compiler_params=_params(("arbitrary", "arbitrary", "arbitrary")),
        name="attn_prompt",
    )(q.reshape(shape3), k.reshape(shape3), v.reshape(shape3), tab, lam_tile, gain)
    return o.reshape(BATCH * SEQ, D_MODEL)


N_SROWS = N_HEADS * 2 * DEC_SEQ


def _attn_sample_kernel(pt_ref, q_ref, qmask_ref, knew_ref, vnew_ref, bias_ref, bnew_ref,
                        lam_ref, gain_ref, *rest):
    del pt_ref
    k_refs = rest[:PAGES_PER_STEP]
    v_refs = rest[PAGES_PER_STEP:2 * PAGES_PER_STEP]
    o_ref, qx_ref, m_ref, acc_ref = rest[2 * PAGES_PER_STEP:]
    step = pl.program_id(1)
    rows_per_head = 2 * DEC_SEQ

    @pl.when(step == 0)
    def _():
        q = q_ref[...].astype(F32)
        q = jnp.concatenate([q, q], axis=0)
        qx_ref[...] = (jnp.concatenate([q] * N_HEADS, axis=0) * qmask_ref[...]).astype(BF16)
        m_ref[...] = jnp.full(m_ref.shape, NEG_INF, F32)
        acc_ref[...] = jnp.zeros_like(acc_ref)

    def update(s, values_of_head):
        m_new, p, alpha = _softmax_update(s, m_ref[...])
        for h in range(N_HEADS):
            r = slice(h * rows_per_head, (h + 1) * rows_per_head)
            acc_ref[r, :] = _acc_update(acc_ref[r, :], alpha[r, :], p[r, :],
                                        _with_ones(values_of_head(h)))
        m_ref[...] = m_new

    qx = qx_ref[...]
    kt = jnp.concatenate([r[...].astype(BF16) for r in k_refs], axis=1)
    s = jnp.dot(qx, kt, preferred_element_type=F32) + bias_ref[...]
    update(s, lambda h: jnp.concatenate(
        [r[pl.ds(h, PAGE_SIZE, stride=N_HEADS), :].astype(BF16) for r in v_refs], axis=0))

    @pl.when(step == pl.num_programs(1) - 1)
    def _():
        pad = jnp.zeros((LANES - SUBLANES, D_MODEL), BF16)
        kn = jnp.concatenate([knew_ref[...], pad], axis=0)
        vn = jnp.concatenate([vnew_ref[...], pad], axis=0)
        sn = lax.dot_general(qx, kn, _NT, preferred_element_type=F32) + bnew_ref[...]
        update(sn, lambda h: vn[:, h * V_DIM:(h + 1) * V_DIM])
        acc = acc_ref[...]
        o = acc[:, :V_DIM] / acc[:, V_DIM:]
        heads = []
        for h in range(N_HEADS):
            r0 = h * rows_per_head
            oh = o[r0:r0 + DEC_SEQ, :] - lam_ref[0:1, 0:1] * o[r0 + DEC_SEQ:r0 + rows_per_head, :]
            oh = oh * lax.rsqrt(jnp.mean(oh * oh, axis=-1, keepdims=True) + LN_EPS) * gain_ref[...]
            heads.append(oh)
        o_ref[...] = jnp.concatenate(heads, axis=1)


def _attn_sample_tables(rel_bias):
    n_dist = PAST_LEN + DEC_SEQ
    far_first = _bias_by_distance(rel_bias, n_dist - 1 - jnp.arange(n_dist))
    past = jnp.stack([far_first[:, DEC_SEQ - 1 - t:DEC_SEQ - 1 - t + PAST_LEN]
                      for t in range(DEC_SEQ)], axis=1)
    rel = (jnp.arange(DEC_SEQ, dtype=jnp.int32)[:, None]
           - jnp.arange(LANES, dtype=jnp.int32)[None, :])
    near = _bias_by_distance(rel_bias, jnp.arange(DEC_SEQ))
    new = jnp.where(rel >= 0, near[:, jnp.clip(rel, 0, DEC_SEQ - 1)], NEG_INF)
    rows = lambda a: jnp.broadcast_to(a[:, None], (N_HEADS, 2) + a.shape[1:]).reshape(
        N_SROWS, a.shape[-1])
    col = jnp.arange(D_MODEL, dtype=jnp.int32)[None, :]
    row = jnp.arange(N_SROWS, dtype=jnp.int32)[:, None]
    qmask = ((col // V_DIM == row // (2 * DEC_SEQ))
             & ((col % V_DIM) // HEAD_DIM == (row // DEC_SEQ) % 2)).astype(F32)
    return rows(past), rows(new), qmask


def _attn_sample(q, k_new, v_new, cache_k, cache_v, page_table, tables, lam_tile, gain):
    past, new, qmask = tables
    n_phys = cache_k.shape[0]
    ck = jnp.transpose(cache_k, (0, 2, 3, 4, 1)).reshape(n_phys, D_MODEL, PAGE_SIZE)
    cv = cache_v.reshape(n_phys, PAGE_SIZE * N_HEADS, V_DIM)
    n_steps = N_PAGES // PAGES_PER_STEP
    keys_per_step = PAGES_PER_STEP * PAGE_SIZE

    def page_spec(r):
        return pl.BlockSpec((None, D_MODEL, PAGE_SIZE),
                            lambda b, s, pt: (pt[b * N_PAGES + s * PAGES_PER_STEP + r], 0, 0))

    per_seq = lambda rows: pl.BlockSpec((None, rows, D_MODEL), lambda b, s, pt: (b, 0, 0))
    const = lambda shape: pl.BlockSpec(shape, lambda b, s, pt: (0,) * len(shape))
    grid_spec = pltpu.PrefetchScalarGridSpec(
        num_scalar_prefetch=1,
        grid=(DEC_BATCH, n_steps),
        in_specs=[per_seq(DEC_SEQ), const(qmask.shape), per_seq(SUBLANES), per_seq(SUBLANES),
                  pl.BlockSpec((N_SROWS, keys_per_step), lambda b, s, pt: (0, s)),
                  const(new.shape), const(lam_tile.shape), const(gain.shape)]
                 + [page_spec(r) for r in range(PAGES_PER_STEP)] * 2,
        out_specs=per_seq(DEC_SEQ),
        scratch_shapes=[pltpu.VMEM((N_SROWS, D_MODEL), BF16), pltpu.VMEM((N_SROWS, LANES), F32),
                        pltpu.VMEM((N_SROWS, 2 * V_DIM), F32)],
    )
    return pl.pallas_call(
        _attn_sample_kernel,
        grid_spec=grid_spec,
        out_shape=jax.ShapeDtypeStruct((DEC_BATCH, DEC_SEQ, D_MODEL), F32),
        compiler_params=_params(("arbitrary", "arbitrary")),
        name="attn_sample",
    )(page_table.reshape(-1), q, qmask, k_new, v_new, past, new, lam_tile, gain,
      *([ck] * PAGES_PER_STEP), *([cv] * PAGES_PER_STEP))


def _to_batch_major(a):
    return jnp.transpose(a.reshape(DEC_SEQ, DEC_BATCH, D_MODEL), (1, 0, 2))


def _to_time_major(a):
    return jnp.transpose(a, (1, 0, 2)).reshape(DEC_SEQ * DEC_BATCH, D_MODEL)


def _trunk(x, mods, mods_kv, h0_re, h0_im, attend, w):
    ssm_re, ssm_im = [], []
    k = v = q = None
    for layer in range(DEPTH):
        if layer < N_A_LAYERS:
            if mods.mode == "p":
                g, fr, fi = _ssm_prompt(x, mods, layer, w["ssm"], w["ssm_d"])
            else:
                g, fr, fi = _ssm_sample(x, mods, layer, w["ssm"], w["ssm_d"], h0_re, h0_im)
            ssm_re.append(fr)
            ssm_im.append(fi)
            mixed, w_mix = g, w["glu"]
        else:
            mixed, w_mix = attend(q, k, v, layer - N_A_LAYERS), w["wo"]
        feeds_attention = N_A_LAYERS <= layer + 1 < DEPTH
        x, *q_next = _block_tail(mixed, x, mods, layer, w_mix, w["w1"], w["w2"], w["ln_g"],
                                 w["ln_b"], glu=layer < N_A_LAYERS,
                                 w_q=w["wq"] if feeds_attention else None,
                                 q_scale=HEAD_DIM ** -0.5 * LOG2E if feeds_attention else None)
        q = q_next[0] if q_next else None
        if layer == N_A_LAYERS - 1:
            outs = [(0, F32, False), (1, F32, False)] + [(0, F32, True)] * (mods.mode == "p")
            k, v, *k_out = _mod_linear(x, mods_kv, 0, 1, 0, w["wkv"], outs)
    return x, jnp.stack(ssm_re), jnp.stack(ssm_im), (k_out or [k])[0], v


def kernel(x_prompt, x_sample, state_ssm_re, state_ssm_im, cache_k, cache_v, page_table, c_prompt, c_sample, rel_bias, w_ada, b_ada, ln_g, ln_b, ssm_lam_re, ssm_lam_im, ssm_log_dt, ssm_b_re, ssm_b_im, ssm_c_re, ssm_c_im, ssm_d, ssm_w_glu_a, ssm_w_glu_b, w_ada_kv, b_ada_kv, w_kv, attn_w_q, attn_lam, attn_subln_g, attn_w_o, mlp_w1, mlp_w2):
    w = {
        "ssm": _ssm_prep(ssm_lam_re, ssm_lam_im, ssm_log_dt, ssm_b_re, ssm_b_im, ssm_c_re, ssm_c_im),
        "ssm_d": ssm_d,
        "glu": jnp.concatenate([ssm_w_glu_a, ssm_w_glu_b], axis=-1).astype(BF16),
        "wq": attn_w_q.astype(BF16),
        "wo": attn_w_o.astype(BF16),
        "wkv": w_kv.astype(BF16),
        "w1": mlp_w1.astype(BF16),
        "w2": mlp_w2.astype(BF16),
        "ln_g": ln_g,
        "ln_b": ln_b,
    }
    c_all = jnp.concatenate([c_sample, c_prompt, jnp.zeros((BATCH, D_MODEL), F32)], axis=0)
    ada = _ada(c_all, w_ada, b_ada)
    ada_kv = _ada(c_all, w_ada_kv[None], b_ada_kv[None])
    scalars = [_attn_scalars(attn_lam[j], attn_subln_g[j], N_A_LAYERS + j)
               for j in range(DEPTH - N_A_LAYERS)]

    prompt_tab = _attn_prompt_tables(rel_bias)

    def attend_prompt(q, k, v, j):
        return _attn_prompt(q, k, v, prompt_tab, *scalars[j])

    y_p, re_p, im_p, k_p, v_p = _trunk(x_prompt.reshape(BATCH * SEQ, D_MODEL), _Mods(ada, "p"),
                                       _Mods(ada_kv, "p"), None, None, attend_prompt, w)

    sample_tab = _attn_sample_tables(rel_bias)

    def attend_sample(q, k, v, j):
        pad = lambda a: jnp.pad(_to_batch_major(a).astype(BF16),
                                ((0, 0), (0, SUBLANES - DEC_SEQ), (0, 0)))
        o = _attn_sample(_to_batch_major(q), pad(k), pad(v), cache_k, cache_v, page_table,
                         sample_tab, *scalars[j])
        return _to_time_major(o).astype(BF16)

    y_s, re_s, im_s, k_s, v_s = _trunk(_to_time_major(x_sample), _Mods(ada, "s"), _Mods(ada_kv, "s"),
                                       state_ssm_re, state_ssm_im, attend_sample, w)

    return (y_p.reshape(BATCH, SEQ, D_MODEL), _to_batch_major(y_s), re_p, im_p,
            jnp.transpose(k_p.reshape(BATCH, N_HEADS, 2, HEAD_DIM, SEQ), (0, 4, 1, 2, 3)),
            v_p.reshape(BATCH, SEQ, N_HEADS, V_DIM),
            re_s, im_s,
            _to_batch_major(k_s).reshape(DEC_BATCH, DEC_SEQ, N_HEADS, 2, HEAD_DIM),
            _to_batch_major(v_s).reshape(DEC_BATCH, DEC_SEQ, N_HEADS, V_DIM))
```

```python
import functools
import math

import jax
import jax.numpy as jnp
from jax import lax
from jax.experimental import pallas as pl
from jax.experimental.pallas import tpu as pltpu

F32 = jnp.float32
BF16 = jnp.bfloat16

D_MODEL = 1024
BATCH = 4
SEQ = 4096
DEPTH = 4
DEC_BATCH = 128
DEC_SEQ = 4
PAST_LEN = 2048
PAGE_SIZE = 128
N_PAGES = PAST_LEN // PAGE_SIZE
N_A_LAYERS = DEPTH // 2
SSM_GROUP = 16
SSM_GROUPS = D_MODEL // SSM_GROUP
SSM_STATE = 64
N_STATE = SSM_GROUPS * SSM_STATE
N_HEADS = 8
HEAD_DIM = D_MODEL // (2 * N_HEADS)
V_DIM = 2 * HEAD_DIM
D_FF = 4 * D_MODEL
NUM_BUCKETS = 32
MAX_DISTANCE = 128
ALPHA = (2.0 * DEPTH) ** 0.25
LN_EPS = 1e-5
NEG_INF = -1e30

SUBLANES = 8
LANES = 128
VMEM_LIMIT = 48 * 1024 * 1024

MOD_ROWS = DEC_BATCH + 2 * BATCH
CHUNK = LANES
N_CHUNK = D_MODEL // CHUNK
CHUNK_STATE = N_STATE // N_CHUNK
TM_PROMPT = 512
TILE = SUBLANES
TQ = 512
HEADS_PER_STEP = 2
PAGES_PER_STEP = 16
LOG2E = math.log2(math.e)
FF_CHUNK = 1024


def _params(sem, vmem=VMEM_LIMIT):
    return pltpu.CompilerParams(dimension_semantics=sem, vmem_limit_bytes=vmem)


def _mod(ref, n_rows):
    m = ref[...]
    reps = n_rows // m.shape[0]
    return m if m.shape[0] == 1 or reps == 1 else jnp.concatenate([m] * reps, axis=0)


def _layer_norm(z, g, b):
    mu = jnp.mean(z, axis=-1, keepdims=True)
    zc = z - mu
    var = jnp.mean(zc * zc, axis=-1, keepdims=True)
    return zc * lax.rsqrt(var + LN_EPS) * g + b


def _ada_kernel(c_ref, w_ref, b_ref, o_ref):
    c = c_ref[...]
    sc = (c * jax.nn.sigmoid(c)).astype(BF16)
    w = w_ref[...].astype(BF16)
    o_ref[...] = jnp.dot(sc, w, preferred_element_type=F32) + b_ref[...]


def _ada(c_all, w, b):
    n_l, _, width = w.shape
    tn = 2048
    return pl.pallas_call(
        _ada_kernel,
        grid=(n_l, width // tn),
        in_specs=[
            pl.BlockSpec((MOD_ROWS, D_MODEL), lambda l, j: (0, 0)),
            pl.BlockSpec((None, D_MODEL, tn), lambda l, j: (l, 0, j)),
            pl.BlockSpec((None, 1, tn), lambda l, j: (l, 0, j)),
        ],
        out_specs=pl.BlockSpec((None, MOD_ROWS, tn), lambda l, j: (l, 0, j)),
        out_shape=jax.ShapeDtypeStruct((n_l, MOD_ROWS, width), F32),
        compiler_params=_params(("arbitrary", "arbitrary")),
        name="ada_mod",
    )(c_all, w, b.reshape(n_l, 1, width))


class _Mods:
    def __init__(self, arr, mode):
        self.mode = mode
        self.arr = arr if mode == "s" else arr.reshape(arr.shape[0], MOD_ROWS, 1, arr.shape[-1])

    def spec(self, layer, j, batch_of):
        if self.mode == "s":
            return pl.BlockSpec((None, DEC_BATCH, D_MODEL), lambda *g: (layer, 0, j))
        return pl.BlockSpec((None, None, 1, D_MODEL),
                            lambda *g: (layer, DEC_BATCH + batch_of(*g), 0, j))


def _row_tiling(mode):
    if mode == "s":
        return DEC_BATCH * DEC_SEQ, lambda i: 0
    tiles_per_batch = SEQ // TM_PROMPT
    return TM_PROMPT, lambda i: i // tiles_per_batch


def _const_spec(shape):
    zeros = (0,) * len(shape)
    return pl.BlockSpec(shape, lambda *g: zeros, pipeline_mode=pl.Buffered(1))


def _mod_linear_kernel(x_ref, sc_ref, sh_ref, w_ref, *o_refs, blocks):
    n_rows = x_ref.shape[0]
    h = (x_ref[...] * (1.0 + _mod(sc_ref, n_rows)) + _mod(sh_ref, n_rows)).astype(BF16)
    y = jnp.dot(h, w_ref[...], preferred_element_type=F32)
    for (n, transposed), o_ref in zip(blocks, o_refs):
        block = y[:, n * D_MODEL:(n + 1) * D_MODEL].astype(o_ref.dtype)
        o_ref[...] = block.T if transposed else block


def _mod_linear(x, mods, layer, j_scale, j_shift, w, outs):
    n = x.shape[0]
    tm, batch_of = _row_tiling(mods.mode)
    row = pl.BlockSpec((tm, D_MODEL), lambda i: (i, 0))
    tiles_per_batch = SEQ // tm
    col = pl.BlockSpec((None, D_MODEL, tm), lambda i: (i // tiles_per_batch, 0, i % tiles_per_batch))
    return pl.pallas_call(
        functools.partial(_mod_linear_kernel, blocks=tuple((o[0], o[2]) for o in outs)),
        grid=(n // tm,),
        in_specs=[row, mods.spec(layer, j_scale, batch_of), mods.spec(layer, j_shift, batch_of),
                  _const_spec(w.shape)],
        out_specs=[col if o[2] else row for o in outs],
        out_shape=[jax.ShapeDtypeStruct((BATCH, D_MODEL, SEQ) if o[2] else (n, D_MODEL), o[1])
                   for o in outs],
        compiler_params=_params(("arbitrary",)),
        name="mod_linear",
    )(x, mods.arr, mods.arr, w)


def _block_tail_kernel(a_ref, x_ref, gm_ref, sc_ref, sh_ref, gf_ref, wm_ref, w1_ref, w2_ref,
                       g_ref, b_ref, *rest, glu, q_scale):
    o_ref = rest[-2] if q_scale else rest[-1]
    y = jnp.dot(a_ref[...], wm_ref[...], preferred_element_type=F32)
    if glu:
        y = y[:, :D_MODEL] * jax.nn.sigmoid(y[:, D_MODEL:])
    n_rows = x_ref.shape[0]
    x = _layer_norm(ALPHA * x_ref[...] + (1.0 + _mod(gm_ref, n_rows)) * y,
                    g_ref[0:1, :], b_ref[0:1, :])
    h = (x * (1.0 + _mod(sc_ref, n_rows)) + _mod(sh_ref, n_rows)).astype(BF16)
    acc = jnp.zeros(x.shape, F32)
    for c in range(D_FF // FF_CHUNK):
        cols = slice(c * FF_CHUNK, (c + 1) * FF_CHUNK)
        hid = jnp.dot(h, w1_ref[:, cols], preferred_element_type=F32)
        hid = jnp.square(jnp.maximum(hid, 0.0)).astype(BF16)
        acc = acc + jnp.dot(hid, w2_ref[cols, :], preferred_element_type=F32)
    z = ALPHA * x + (1.0 + _mod(gf_ref, n_rows)) * acc
    out = _layer_norm(z, g_ref[1:2, :], b_ref[1:2, :])
    o_ref[...] = out
    if q_scale:
        qsc_ref, qsh_ref, wq_ref, _, q_ref = rest
        hq = (out * (1.0 + _mod(qsc_ref, n_rows)) + _mod(qsh_ref, n_rows)).astype(BF16)
        q = jnp.dot(hq, wq_ref[...], preferred_element_type=F32) * q_scale
        q_ref[...] = q.astype(q_ref.dtype)


def _block_tail(a, x, mods, layer, w_mix, w1, w2, ln_g, ln_b, glu, w_q=None, q_scale=None):
    n = x.shape[0]
    tm, batch_of = _row_tiling(mods.mode)
    row = pl.BlockSpec((tm, D_MODEL), lambda i: (i, 0))
    ln = pl.BlockSpec((None, 2, D_MODEL), lambda i: (layer, 0, 0))
    weight = lambda w, l=layer: pl.BlockSpec((None,) + w.shape[1:],
                                             lambda i: (l % w.shape[0], 0, 0),
                                             pipeline_mode=pl.Buffered(1))
    in_specs = ([row, row] + [mods.spec(layer, j, batch_of) for j in (2, 4, 3, 5)]
                + [weight(w_mix), weight(w1), weight(w2), ln, ln])
    args = [a, x, mods.arr, mods.arr, mods.arr, mods.arr, w_mix, w1, w2, ln_g, ln_b]
    out_specs, out_shape = [row], [jax.ShapeDtypeStruct((n, D_MODEL), F32)]
    if w_q is not None:
        nxt = layer + 1
        in_specs += [mods.spec(nxt, 1, batch_of), mods.spec(nxt, 0, batch_of),
                     weight(w_q, nxt - N_A_LAYERS)]
        args += [mods.arr, mods.arr, w_q]
        out_specs.append(row)
        out_shape.append(jax.ShapeDtypeStruct((n, D_MODEL), BF16))
    return pl.pallas_call(
        functools.partial(_block_tail_kernel, glu=glu, q_scale=q_scale),
        grid=(n // tm,),
        in_specs=in_specs,
        out_specs=out_specs,
        out_shape=out_shape,
        compiler_params=_params(("arbitrary",)),
        name="block_tail",
    )(*args)


def _cmul(ar, ai, br, bi):
    return ar * br - ai * bi, ar * bi + ai * br


def _ssm_prep(lam_re, lam_im, log_dt, b_re, b_im, c_re, c_im):
    n_l = lam_re.shape[0]
    dt = jnp.exp(log_dt)[..., None]
    n_scan = TILE + SUBLANES
    ns = (list(range(TILE + 1)) + [TILE * m for m in range(2, SUBLANES + 1)]
          + list(range(TILE - 1, -1, -1)))
    n = jnp.asarray(ns, F32).reshape(-1, 1, 1, 1)
    mag = jnp.exp(n * (lam_re * dt))
    pr = mag * jnp.cos(n * (lam_im * dt))
    pi = mag * jnp.sin(n * (lam_im * dt))
    ar, ai = pr[1], pi[1]
    den = lam_re * lam_re + lam_im * lam_im
    er = ((ar - 1.0) * lam_re + ai * lam_im) / den
    ei = (ai * lam_re - (ar - 1.0) * lam_im) / den
    bbr = er[..., None] * b_re - ei[..., None] * b_im
    bbi = er[..., None] * b_im + ei[..., None] * b_re
    gpc = CHUNK // SSM_GROUP
    exact = lax.Precision.HIGHEST

    def block_diag(m, rows_per_group, cols_per_group, steps=1):
        y = steps * cols_per_group
        c = jnp.arange(steps * gpc * cols_per_group)
        step, group, inner = c // (gpc * cols_per_group), (c // cols_per_group) % gpc, c % cols_per_group
        select = (jnp.arange(y)[:, None] == (step * cols_per_group + inner)[None, :]).astype(BF16)
        wide = jnp.einsum("...ry,yc->...rc", m.astype(BF16), select, preferred_element_type=BF16)
        row_group = (jnp.arange(m.shape[-2]) // rows_per_group) % gpc
        return jnp.where(row_group[:, None] == group[None, :], wide, jnp.zeros((), BF16))

    def per_chunk(m, order):
        x, _, _, a, b = m.shape
        m = m.reshape(x, n_l, N_CHUNK, gpc, a, b)
        return jnp.transpose(m, (1, 2) + tuple({"x": 0, "g": 3, "a": 4, "b": 5}[o] for o in order))

    swap = lambda m: jnp.swapaxes(m, -1, -2)
    b_in = per_chunk(jnp.concatenate([swap(bbr), swap(bbi)], axis=-1)[None], "gaxb")
    bcat = block_diag(b_in.reshape(n_l, N_CHUNK, CHUNK, 2 * SSM_STATE), SSM_GROUP, SSM_STATE,
                      steps=2)
    c_out = per_chunk(jnp.stack([swap(c_re), -swap(c_im)]), "xgab")
    ccat = block_diag(c_out.reshape(n_l, N_CHUNK, 2 * CHUNK_STATE, SSM_GROUP), SSM_STATE, SSM_GROUP)

    abr, abi = _cmul(pr[:TILE, ..., None], pi[:TILE, ..., None], bbr, bbi)
    dot_p = functools.partial(jnp.einsum, "lghp,jlgpi->jlgih", precision=exact)
    taps = per_chunk(dot_p(c_re, abr) - dot_p(c_im, abi), "xgab")
    kcat = block_diag(taps.reshape(n_l, N_CHUNK, TILE * CHUNK, SSM_GROUP), SSM_GROUP, SSM_GROUP)

    wr, wi = _cmul(pr[n_scan:, ..., None], pi[n_scan:, ..., None], bbr, bbi)
    ends = per_chunk(jnp.concatenate([swap(wr), swap(wi)], axis=-1), "xgab")
    wst = block_diag(ends.reshape(n_l, N_CHUNK, TILE * CHUNK, 2 * SSM_STATE), SSM_GROUP, SSM_STATE,
                     steps=2)

    mr, mi = _cmul(pr[1:TILE + 1, :, :, None, :], pi[1:TILE + 1, :, :, None, :], c_re, c_im)
    outs = jnp.stack([per_chunk(m, "gbxa") for m in (mr, -mi)], axis=2)
    call = block_diag(outs.reshape(n_l, N_CHUNK, 2 * CHUNK_STATE, TILE * SSM_GROUP), SSM_STATE,
                      SSM_GROUP, steps=TILE)

    flat = lambda a: jnp.moveaxis(a, 0, 1).reshape(n_l, -1, N_STATE)
    rows = lax.broadcasted_iota(jnp.int32, (SUBLANES, N_STATE), 0)
    parts = []
    for a in (flat(pr[TILE:n_scan]), flat(pi[TILE:n_scan])):
        masked = lambda m, first: jnp.where(rows >= first, a[:, m - 1:m, :], 0.0)
        parts.append([masked(1, 1), masked(2, 2), masked(4, 4), a])
    tab = jnp.stack([p for pair in zip(*parts) for p in pair], axis=1)
    atab = jnp.stack([jnp.broadcast_to(flat(a[1:2]), (n_l, SUBLANES, N_STATE)) for a in (pr, pi)],
                     axis=1)
    return dict(bcat=bcat, ccat=ccat, kcat=kcat, wst=wst, call=call, tab=tab, atab=atab)


def _ssm_prompt_kernel(x_ref, sc_ref, sh_ref, kcat_ref, wst_ref, call_ref, d_ref, tab_ref,
                       g_ref, hre_ref, him_ref, h_ref, buf_ref, y_ref, taps_ref):
    n_tiles = SEQ // TILE
    half = CHUNK_STATE

    @pl.when(pl.program_id(1) == 0)
    def _():
        zero = jnp.zeros((CHUNK, CHUNK), BF16)
        for r in range(TILE):
            for s in range(TILE):
                lag = s - r
                block = kcat_ref[lag * CHUNK:(lag + 1) * CHUNK, :] if lag >= 0 else zero
                taps_ref[r * CHUNK:(r + 1) * CHUNK, s * CHUNK:(s + 1) * CHUNK] = block

    h_ref[...] = x_ref[...] * (1.0 + sc_ref[...]) + sh_ref[...]

    steps = [h_ref[pl.ds(r, n_tiles, stride=TILE), :].astype(BF16) for r in range(TILE)]
    tiles = jnp.concatenate(steps, axis=1)
    buf_ref[...] = jnp.dot(tiles, wst_ref[...], preferred_element_type=F32)
    y_tiles = jnp.dot(tiles, taps_ref[...], preferred_element_type=F32)

    row = lax.broadcasted_iota(jnp.int32, (SUBLANES, half), 0)

    def tile_scan(i, carry):
        cr, ci = carry
        r0 = pl.multiple_of(i * SUBLANES, SUBLANES)
        vr = buf_ref[pl.ds(r0, SUBLANES), 0:half]
        vi = buf_ref[pl.ds(r0, SUBLANES), half:2 * half]
        for lvl, shift in enumerate((1, 2, 4)):
            pr, pi = _cmul(tab_ref[2 * lvl], tab_ref[2 * lvl + 1],
                           pltpu.roll(vr, shift, 0), pltpu.roll(vi, shift, 0))
            vr, vi = vr + pr, vi + pi
        pr, pi = _cmul(tab_ref[6], tab_ref[7], cr, ci)
        xr, xi = vr + pr, vi + pi
        buf_ref[pl.ds(r0, SUBLANES), 0:half] = jnp.where(row == 0, cr, pltpu.roll(xr, 1, 0))
        buf_ref[pl.ds(r0, SUBLANES), half:2 * half] = jnp.where(row == 0, ci, pltpu.roll(xi, 1, 0))
        last = SUBLANES - 1
        return (jnp.broadcast_to(xr[last:, :], xr.shape), jnp.broadcast_to(xi[last:, :], xi.shape))

    zero = jnp.zeros((SUBLANES, half), F32)
    cr, ci = lax.fori_loop(0, n_tiles // SUBLANES, tile_scan, (zero, zero))
    hre_ref[...] = cr[0:1, :]
    him_ref[...] = ci[0:1, :]

    y_tiles = y_tiles + jnp.dot(buf_ref[...].astype(BF16), call_ref[...],
                                preferred_element_type=F32)
    for s in range(TILE):
        y_ref[pl.ds(s, n_tiles, stride=TILE), :] = y_tiles[:, s * CHUNK:(s + 1) * CHUNK]

    y = y_ref[...] + d_ref[...] * h_ref[...]
    g_ref[...] = jax.nn.gelu(y).astype(g_ref.dtype)


def _ssm_prompt(x, mods, layer, p, d_skip):
    col = pl.BlockSpec((SEQ, CHUNK), lambda k, b: (b, k))
    fin = pl.BlockSpec((None, 1, CHUNK_STATE), lambda k, b: (b, 0, k))
    mod = lambda j: pl.BlockSpec((None, None, 1, CHUNK),
                                 lambda k, b: (layer, DEC_BATCH + b, 0, j * N_CHUNK + k))
    weight = lambda a: pl.BlockSpec((None, None) + a.shape[2:], lambda k, b: (layer, k, 0, 0))
    g, hre, him = pl.pallas_call(
        _ssm_prompt_kernel,
        grid=(N_CHUNK, BATCH),
        in_specs=[col, mod(1), mod(0), weight(p["kcat"]), weight(p["wst"]), weight(p["call"]),
                  pl.BlockSpec((None, 1, CHUNK), lambda k, b: (layer, 0, k)),
                  pl.BlockSpec((None, p["tab"].shape[1], SUBLANES, CHUNK_STATE),
                               lambda k, b: (layer, 0, 0, k))],
        out_specs=[col, fin, fin],
        out_shape=[jax.ShapeDtypeStruct((BATCH * SEQ, D_MODEL), BF16),
                   jax.ShapeDtypeStruct((BATCH, 1, N_STATE), F32),
                   jax.ShapeDtypeStruct((BATCH, 1, N_STATE), F32)],
        scratch_shapes=[pltpu.VMEM((SEQ, CHUNK), F32),
                        pltpu.VMEM((SEQ // TILE, 2 * CHUNK_STATE), F32),
                        pltpu.VMEM((SEQ, CHUNK), F32),
                        pltpu.VMEM((TILE * CHUNK, TILE * CHUNK), BF16)],
        compiler_params=_params(("arbitrary", "arbitrary")),
        name="ssm_prompt",
    )(x, mods.arr, mods.arr, p["kcat"], p["wst"], p["call"], d_skip[:, None, :], p["tab"])
    shape = (BATCH, SSM_GROUPS, SSM_STATE)
    return g, hre.reshape(shape), him.reshape(shape)


def _ssm_sample_kernel(x_ref, sc_ref, sh_ref, bcat_ref, ccat_ref, d_ref, a_ref, h0r_ref, h0i_ref,
                       g_ref, hre_ref, him_ref):
    tile = lambda m: jnp.concatenate([m] * DEC_SEQ, axis=0)
    h = x_ref[...] * (1.0 + tile(sc_ref[...])) + tile(sh_ref[...])
    bu = jnp.dot(h.astype(BF16), bcat_ref[...], preferred_element_type=F32)
    ar = a_ref[0, 0:1, :]
    ai = a_ref[1, 0:1, :]
    sr = h0r_ref[...].T
    si = h0i_ref[...].T
    states = []
    for t in range(DEC_SEQ):
        rows = slice(t * DEC_BATCH, (t + 1) * DEC_BATCH)
        pr, pi = _cmul(ar, ai, sr, si)
        sr = pr + bu[rows, :CHUNK_STATE]
        si = pi + bu[rows, CHUNK_STATE:]
        states.append(jnp.concatenate([sr, si], axis=1))
    xs = jnp.concatenate(states, axis=0).astype(BF16)
    y = jnp.dot(xs, ccat_ref[...], preferred_element_type=F32) + d_ref[...] * h
    g_ref[...] = jax.nn.gelu(y).astype(g_ref.dtype)
    hre_ref[...] = sr.T
    him_ref[...] = si.T


def _ssm_sample(x, mods, layer, p, d_skip, h0_re, h0_im):
    n = DEC_BATCH * DEC_SEQ
    col = pl.BlockSpec((n, CHUNK), lambda k: (0, k))
    st = pl.BlockSpec((None, CHUNK_STATE, DEC_BATCH), lambda k: (layer, k, 0))
    mod = lambda j: pl.BlockSpec((None, DEC_BATCH, CHUNK), lambda k: (layer, 0, j * N_CHUNK + k))
    state_major = lambda a: jnp.transpose(a, (0, 2, 3, 1)).reshape(-1, N_STATE, DEC_BATCH)
    g, hre, him = pl.pallas_call(
        _ssm_sample_kernel,
        grid=(N_CHUNK,),
        in_specs=[col, mod(1), mod(0),
                  pl.BlockSpec((None, None, CHUNK, 2 * CHUNK_STATE), lambda k: (layer, k, 0, 0)),
                  pl.BlockSpec((None, None, 2 * CHUNK_STATE, CHUNK), lambda k: (layer, k, 0, 0)),
                  pl.BlockSpec((None, 1, CHUNK), lambda k: (layer, 0, k)),
                  pl.BlockSpec((None, 2, SUBLANES, CHUNK_STATE), lambda k: (layer, 0, 0, k)),
                  st, st],
        out_specs=[col] + [pl.BlockSpec((CHUNK_STATE, DEC_BATCH), lambda k: (k, 0))] * 2,
        out_shape=[jax.ShapeDtypeStruct((n, D_MODEL), BF16),
                   jax.ShapeDtypeStruct((N_STATE, DEC_BATCH), F32),
                   jax.ShapeDtypeStruct((N_STATE, DEC_BATCH), F32)],
        compiler_params=_params(("arbitrary",)),
        name="ssm_sample",
    )(x, mods.arr, mods.arr, p["bcat"], p["ccat"], d_skip[:, None, :], p["atab"],
      state_major(h0_re), state_major(h0_im))
    batch_major = lambda a: jnp.transpose(a.reshape(SSM_GROUPS, SSM_STATE, DEC_BATCH), (2, 0, 1))
    return g, batch_major(hre), batch_major(him)


def _rel_bucket(rel):
    n = jnp.maximum(rel, 0)
    max_exact = NUM_BUCKETS // 2
    large = max_exact + (jnp.log(jnp.maximum(n, 1).astype(F32) / max_exact)
                         / math.log(MAX_DISTANCE / max_exact) * (NUM_BUCKETS - max_exact)).astype(jnp.int32)
    large = jnp.minimum(large, NUM_BUCKETS - 1)
    return jnp.where(n < max_exact, n, large)


def _attn_scalars(attn_lam, subln_g, layer):
    lam_init = 0.8 - 0.6 * math.exp(-0.3 * layer)
    lam = (jnp.exp(jnp.sum(attn_lam[0] * attn_lam[1])) - jnp.exp(jnp.sum(attn_lam[2] * attn_lam[3]))
           + lam_init)
    lam_tile = jnp.full((SUBLANES, LANES), lam, F32)
    gain = (subln_g * (1.0 - lam_init)).reshape(1, V_DIM)
    return lam_tile, gain


def _bias_by_distance(rel_bias, dist):
    bv = rel_bias[_rel_bucket(dist.astype(jnp.int32))]
    return ((bv - rel_bias[NUM_BUCKETS - 1]) * LOG2E).T


def _softmax_update(s, m_prev):
    m_new = jnp.maximum(m_prev, jnp.max(s, axis=1, keepdims=True))
    p = jnp.exp2(s - jnp.concatenate([m_new] * (s.shape[1] // LANES), axis=1))
    return m_new, p.astype(BF16), jnp.exp2(m_prev - m_new)


def _with_ones(v):
    return jnp.concatenate([v, jnp.ones((v.shape[0], V_DIM), BF16)], axis=1)


def _acc_update(acc, alpha, p, vx):
    return (jnp.concatenate([alpha, alpha], axis=1) * acc
            + jnp.dot(p, vx, preferred_element_type=F32))


_NT = (((1,), (1,)), ((), ()))


def _attn_prompt_kernel(q_ref, k_ref, v_ref, tab_ref, lam_ref, gain_ref, o_ref,
                        q2_ref, m_ref, acc_ref, s_ref):
    i = pl.program_id(2)
    last = pl.num_programs(2) - 1

    def key_rows(j):
        return pl.ds(pl.multiple_of(j * TQ, TQ), TQ)

    def stack_queries(hh, tile):
        q = q_ref[key_rows(tile), hh * V_DIM:(hh + 1) * V_DIM]
        lane = lax.broadcasted_iota(jnp.int32, q.shape, 1)
        zero = jnp.zeros_like(q)
        q2_ref[hh, 0:TQ, :] = jnp.where(lane < HEAD_DIM, q, zero)
        q2_ref[hh, TQ:2 * TQ, :] = jnp.where(lane >= HEAD_DIM, q, zero)

    def logits(hh, j):
        cols = slice(hh * V_DIM, (hh + 1) * V_DIM)
        return lax.dot_general(q2_ref[hh], k_ref[key_rows(j), cols].astype(BF16), _NT,
                               preferred_element_type=F32)

    @pl.when(i == 0)
    def _():
        stack_queries(0, 0)
        s_ref[...] = logits(0, 0)

    for hh in range(1, HEADS_PER_STEP):
        stack_queries(hh, i)
    m_ref[...] = jnp.full(m_ref.shape, NEG_INF, F32)
    acc_ref[...] = jnp.zeros_like(acc_ref)

    def consume(hh, j, s, table):
        cols = slice(hh * V_DIM, (hh + 1) * V_DIM)
        if table is not None:
            bias = tab_ref[hh, table]
            s = s + jnp.concatenate([bias, bias], axis=0)
        m_new, p, alpha = _softmax_update(s, m_ref[hh])
        acc_ref[hh] = _acc_update(acc_ref[hh], alpha, p,
                                  _with_ones(v_ref[key_rows(j), cols].astype(BF16)))
        m_ref[hh] = m_new

    def step(j, table, same_tile):
        consume(0, j, s_ref[...], table)
        for hh in range(1, HEADS_PER_STEP):
            consume(hh, j, logits(hh, j), table)
        if same_tile:
            s_ref[...] = logits(0, j + 1)
        else:
            stack_queries(0, jnp.minimum(i + 1, last))
            s_ref[...] = logits(0, 0)

    def far(j, c):
        step(j, None, True)
        return c

    lax.fori_loop(0, jnp.maximum(i - 1, 0), far, 0)

    @pl.when(i >= 1)
    def _():
        step(i - 1, 1, True)

    step(i, 0, False)

    for hh in range(HEADS_PER_STEP):
        acc = acc_ref[hh]
        o = acc[:, :V_DIM] / acc[:, V_DIM:]
        o = o[0:TQ, :] - lam_ref[0:1, 0:1] * o[TQ:2 * TQ, :]
        o = o * lax.rsqrt(jnp.mean(o * o, axis=-1, keepdims=True) + LN_EPS) * gain_ref[...]
        o_ref[:, hh * V_DIM:(hh + 1) * V_DIM] = o.astype(o_ref.dtype)


def _attn_prompt_tables(rel_bias):
    m = jnp.arange(2 * TQ)
    behind = (2 * TQ - m) % (2 * TQ)
    diag = jnp.where((m >= 1) & (m <= TQ), NEG_INF, _bias_by_distance(rel_bias, behind))
    prev = _bias_by_distance(rel_bias, jnp.where(m <= TQ, TQ - m, 3 * TQ - m))
    vec = jnp.broadcast_to(jnp.stack([diag, prev], axis=1)[:, :, None, :],
                           (N_HEADS, 2, SUBLANES, 2 * TQ))
    return pl.pallas_call(
        _toeplitz_kernel,
        grid=(N_HEADS,),
        in_specs=[pl.BlockSpec((None, 2, SUBLANES, 2 * TQ), lambda h: (h, 0, 0, 0))],
        out_specs=pl.BlockSpec((None, 2, TQ, TQ), lambda h: (h, 0, 0, 0)),
        out_shape=jax.ShapeDtypeStruct((N_HEADS, 2, TQ, TQ), F32),
        compiler_params=_params(("arbitrary",)),
        name="bias_tiles",
    )(vec)


def _toeplitz_kernel(vec_ref, o_ref):
    for t in range(o_ref.shape[0]):
        rows = jnp.broadcast_to(vec_ref[t, 0:1, :], (TQ, 2 * TQ))
        o_ref[t] = pltpu.roll(rows, 0, 1, stride=1, stride_axis=0)[:, :TQ]


def _attn_prompt(q, k, v, tab, lam_tile, gain):
    n_q = SEQ // TQ
    width = HEADS_PER_STEP * V_DIM
    shape3 = (BATCH, SEQ, D_MODEL)
    qspec = pl.BlockSpec((None, TQ, width), lambda b, h, i: (b, i, h))
    kvspec = pl.BlockSpec((None, SEQ, width), lambda b, h, i: (b, 0, h))
    o = pl.pallas_call(
        _attn_prompt_kernel,
        grid=(BATCH, N_HEADS // HEADS_PER_STEP, n_q),
        in_specs=[kvspec, kvspec, kvspec,
                  pl.BlockSpec((HEADS_PER_STEP, 2, TQ, TQ), lambda b, h, i: (h, 0, 0, 0)),
                  pl.BlockSpec((SUBLANES, LANES), lambda b, h, i: (0, 0)),
                  pl.BlockSpec((1, V_DIM), lambda b, h, i: (0, 0))],
        out_specs=qspec,
        out_shape=jax.ShapeDtypeStruct(shape3, BF16),
        scratch_shapes=[pltpu.VMEM((HEADS_PER_STEP, 2 * TQ, V_DIM), BF16),
                        pltpu.VMEM((HEADS_PER_STEP, 2 * TQ, LANES), F32),
                        pltpu.VMEM((HEADS_PER_STEP, 2 * TQ, 2 * V_DIM), F32),
                        pltpu.VMEM((2 * TQ, TQ), F32)],
        compiler_params=_params(("arbitrary", "arbitrary", "arbitrary")),
        name="attn_prompt",
    )(q.reshape(shape3), k.reshape(shape3), v.reshape(shape3), tab, lam_tile, gain)
    return o.reshape(BATCH * SEQ, D_MODEL)


N_SROWS = N_HEADS * 2 * DEC_SEQ


def _attn_sample_kernel(pt_ref, q_ref, qmask_ref, knew_ref, vnew_ref, bias_ref, bnew_ref,
                        lam_ref, gain_ref, *rest):
    del pt_ref
    k_refs = rest[:PAGES_PER_STEP]
    v_refs = rest[PAGES_PER_STEP:2 * PAGES_PER_STEP]
    o_ref, qx_ref, m_ref, acc_ref = rest[2 * PAGES_PER_STEP:]
    step = pl.program_id(1)
    rows_per_head = 2 * DEC_SEQ

    @pl.when(step == 0)
    def _():
        q = q_ref[...].astype(F32)
        q = jnp.concatenate([q, q], axis=0)
        qx_ref[...] = (jnp.concatenate([q] * N_HEADS, axis=0) * qmask_ref[...]).astype(BF16)
        m_ref[...] = jnp.full(m_ref.shape, NEG_INF, F32)
        acc_ref[...] = jnp.zeros_like(acc_ref)

    def update(s, values_of_head):
        m_new, p, alpha = _softmax_update(s, m_ref[...])
        for h in range(N_HEADS):
            r = slice(h * rows_per_head, (h + 1) * rows_per_head)
            acc_ref[r, :] = _acc_update(acc_ref[r, :], alpha[r, :], p[r, :],
                                        _with_ones(values_of_head(h)))
        m_ref[...] = m_new

    qx = qx_ref[...]
    kt = jnp.concatenate([r[...].astype(BF16) for r in k_refs], axis=1)
    s = jnp.dot(qx, kt, preferred_element_type=F32) + bias_ref[...]
    update(s, lambda h: jnp.concatenate(
        [r[pl.ds(h, PAGE_SIZE, stride=N_HEADS), :].astype(BF16) for r in v_refs], axis=0))

    @pl.when(step == pl.num_programs(1) - 1)
    def _():
        pad = jnp.zeros((LANES - SUBLANES, D_MODEL), BF16)
        kn = jnp.concatenate([knew_ref[...], pad], axis=0)
        vn = jnp.concatenate([vnew_ref[...], pad], axis=0)
        sn = lax.dot_general(qx, kn, _NT, preferred_element_type=F32) + bnew_ref[...]
        update(sn, lambda h: vn[:, h * V_DIM:(h + 1) * V_DIM])
        acc = acc_ref[...]
        o = acc[:, :V_DIM] / acc[:, V_DIM:]
        heads = []
        for h in range(N_HEADS):
            r0 = h * rows_per_head
            oh = o[r0:r0 + DEC_SEQ, :] - lam_ref[0:1, 0:1] * o[r0 + DEC_SEQ:r0 + rows_per_head, :]
            oh = oh * lax.rsqrt(jnp.mean(oh * oh, axis=-1, keepdims=True) + LN_EPS) * gain_ref[...]
            heads.append(oh)
        o_ref[...] = jnp.concatenate(heads, axis=1)


def _attn_sample_tables(rel_bias):
    n_dist = PAST_LEN + DEC_SEQ
    far_first = _bias_by_distance(rel_bias, n_dist - 1 - jnp.arange(n_dist))
    past = jnp.stack([far_first[:, DEC_SEQ - 1 - t:DEC_SEQ - 1 - t + PAST_LEN]
                      for t in range(DEC_SEQ)], axis=1)
    rel = (jnp.arange(DEC_SEQ, dtype=jnp.int32)[:, None]
           - jnp.arange(LANES, dtype=jnp.int32)[None, :])
    near = _bias_by_distance(rel_bias, jnp.arange(DEC_SEQ))
    new = jnp.where(rel >= 0, near[:, jnp.clip(rel, 0, DEC_SEQ - 1)], NEG_INF)
    rows = lambda a: jnp.broadcast_to(a[:, None], (N_HEADS, 2) + a.shape[1:]).reshape(
        N_SROWS, a.shape[-1])
    col = jnp.arange(D_MODEL, dtype=jnp.int32)[None, :]
    row = jnp.arange(N_SROWS, dtype=jnp.int32)[:, None]
    qmask = ((col // V_DIM == row // (2 * DEC_SEQ))
             & ((col % V_DIM) // HEAD_DIM == (row // DEC_SEQ) % 2)).astype(F32)
    return rows(past), rows(new), qmask


def _attn_sample(q, k_new, v_new, cache_k, cache_v, page_table, tables, lam_tile, gain):
    past, new, qmask = tables
    n_phys = cache_k.shape[0]
    ck = jnp.transpose(cache_k, (0, 2, 3, 4, 1)).reshape(n_phys, D_MODEL, PAGE_SIZE)
    cv = cache_v.reshape(n_phys, PAGE_SIZE * N_HEADS, V_DIM)
    n_steps = N_PAGES // PAGES_PER_STEP
    keys_per_step = PAGES_PER_STEP * PAGE_SIZE

    def page_spec(r):
        return pl.BlockSpec((None, D_MODEL, PAGE_SIZE),
                            lambda b, s, pt: (pt[b * N_PAGES + s * PAGES_PER_STEP + r], 0, 0))

    per_seq = lambda rows: pl.BlockSpec((None, rows, D_MODEL), lambda b, s, pt: (b, 0, 0))
    const = lambda shape: pl.BlockSpec(shape, lambda b, s, pt: (0,) * len(shape))
    grid_spec = pltpu.PrefetchScalarGridSpec(
        num_scalar_prefetch=1,
        grid=(DEC_BATCH, n_steps),
        in_specs=[per_seq(DEC_SEQ), const(qmask.shape), per_seq(SUBLANES), per_seq(SUBLANES),
                  pl.BlockSpec((N_SROWS, keys_per_step), lambda b, s, pt: (0, s)),
                  const(new.shape), const(lam_tile.shape), const(gain.shape)]
                 + [page_spec(r) for r in range(PAGES_PER_STEP)] * 2,
        out_specs=per_seq(DEC_SEQ),
        scratch_shapes=[pltpu.VMEM((N_SROWS, D_MODEL), BF16), pltpu.VMEM((N_SROWS, LANES), F32),
                        pltpu.VMEM((N_SROWS, 2 * V_DIM), F32)],
    )
    return pl.pallas_call(
        _attn_sample_kernel,
        grid_spec=grid_spec,
        out_shape=jax.ShapeDtypeStruct((DEC_BATCH, DEC_SEQ, D_MODEL), F32),
        compiler_params=_params(("arbitrary", "arbitrary")),
        name="attn_sample",
    )(page_table.reshape(-1), q, qmask, k_new, v_new, past, new, lam_tile, gain,
      *([ck] * PAGES_PER_STEP), *([cv] * PAGES_PER_STEP))


def _to_batch_major(a):
    return jnp.transpose(a.reshape(DEC_SEQ, DEC_BATCH, D_MODEL), (1, 0, 2))


def _to_time_major(a):
    return jnp.transpose(a, (1, 0, 2)).reshape(DEC_SEQ * DEC_BATCH, D_MODEL)


def _trunk(x, mods, mods_kv, h0_re, h0_im, attend, w):
    ssm_re, ssm_im = [], []
    k = v = q = None
    for layer in range(DEPTH):
        if layer < N_A_LAYERS:
            if mods.mode == "p":
                g, fr, fi = _ssm_prompt(x, mods, layer, w["ssm"], w["ssm_d"])
            else:
                g, fr, fi = _ssm_sample(x, mods, layer, w["ssm"], w["ssm_d"], h0_re, h0_im)
            ssm_re.append(fr)
            ssm_im.append(fi)
            mixed, w_mix = g, w["glu"]
        else:
            mixed, w_mix = attend(q, k, v, layer - N_A_LAYERS), w["wo"]
        feeds_attention = N_A_LAYERS <= layer + 1 < DEPTH
        x, *q_next = _block_tail(mixed, x, mods, layer, w_mix, w["w1"], w["w2"], w["ln_g"],
                                 w["ln_b"], glu=layer < N_A_LAYERS,
                                 w_q=w["wq"] if feeds_attention else None,
                                 q_scale=HEAD_DIM ** -0.5 * LOG2E if feeds_attention else None)
        q = q_next[0] if q_next else None
        if layer == N_A_LAYERS - 1:
            outs = [(0, F32, False), (1, F32, False)] + [(0, F32, True)] * (mods.mode == "p")
            k, v, *k_out = _mod_linear(x, mods_kv, 0, 1, 0, w["wkv"], outs)
    return x, jnp.stack(ssm_re), jnp.stack(ssm_im), (k_out or [k])[0], v


def kernel(x_prompt, x_sample, state_ssm_re, state_ssm_im, cache_k, cache_v, page_table, c_prompt, c_sample, rel_bias, w_ada, b_ada, ln_g, ln_b, ssm_lam_re, ssm_lam_im, ssm_log_dt, ssm_b_re, ssm_b_im, ssm_c_re, ssm_c_im, ssm_d, ssm_w_glu_a, ssm_w_glu_b, w_ada_kv, b_ada_kv, w_kv, attn_w_q, attn_lam, attn_subln_g, attn_w_o, mlp_w1, mlp_w2):
    w = {
        "ssm": _ssm_prep(ssm_lam_re, ssm_lam_im, ssm_log_dt, ssm_b_re, ssm_b_im, ssm_c_re, ssm_c_im),
        "ssm_d": ssm_d,
        "glu": jnp.concatenate([ssm_w_glu_a, ssm_w_glu_b], axis=-1).astype(BF16),
        "wq": attn_w_q.astype(BF16),
        "wo": attn_w_o.astype(BF16),
        "wkv": w_kv.astype(BF16),
        "w1": mlp_w1.astype(BF16),
        "w2": mlp_w2.astype(BF16),
        "ln_g": ln_g,
        "ln_b": ln_b,
    }
    c_all = jnp.concatenate([c_sample, c_prompt, jnp.zeros((BATCH, D_MODEL), F32)], axis=0)
    ada = _ada(c_all, w_ada, b_ada)
    ada_kv = _ada(c_all, w_ada_kv[None], b_ada_kv[None])
    scalars = [_attn_scalars(attn_lam[j], attn_subln_g[j], N_A_LAYERS + j)
               for j in range(DEPTH - N_A_LAYERS)]

    prompt_tab = _attn_prompt_tables(rel_bias)

    def attend_prompt(q, k, v, j):
        return _attn_prompt(q, k, v, prompt_tab, *scalars[j])

    y_p, re_p, im_p, k_p, v_p = _trunk(x_prompt.reshape(BATCH * SEQ, D_MODEL), _Mods(ada, "p"),
                                       _Mods(ada_kv, "p"), None, None, attend_prompt, w)

    sample_tab = _attn_sample_tables(rel_bias)

    def attend_sample(q, k, v, j):
        pad = lambda a: jnp.pad(_to_batch_major(a).astype(BF16),
                                ((0, 0), (0, SUBLANES - DEC_SEQ), (0, 0)))
        o = _attn_sample(_to_batch_major(q), pad(k), pad(v), cache_k, cache_v, page_table,
                         sample_tab, *scalars[j])
        return _to_time_major(o).astype(BF16)

    y_s, re_s, im_s, k_s, v_s = _trunk(_to_time_major(x_sample), _Mods(ada, "s"), _Mods(ada_kv, "s"),
                                       state_ssm_re, state_ssm_im, attend_sample, w)

    return (y_p.reshape(BATCH, SEQ, D_MODEL), _to_batch_major(y_s), re_p, im_p,
            jnp.transpose(k_p.reshape(BATCH, N_HEADS, 2, HEAD_DIM, SEQ), (0, 4, 1, 2, 3)),
            v_p.reshape(BATCH, SEQ, N_HEADS, V_DIM),
            re_s, im_s,
            _to_batch_major(k_s).reshape(DEC_BATCH, DEC_SEQ, N_HEADS, 2, HEAD_DIM),
            _to_batch_major(v_s).reshape(DEC_BATCH, DEC_SEQ, N_HEADS, V_DIM))
```
